```python
import jax, jax.numpy as jnp
from jax import lax
import numpy as np

D_MODEL = 1024
BATCH = 8
SEQ = 2048
DEPTH = 1

D_PLE = 256
N_ATTN_HEADS = 8
HEAD_DIM = 64
D_ATTN = N_ATTN_HEADS * HEAD_DIM
N_GMLP_GROUPS = 8
GMLP_GROUP_DIM = 64
D_GMLP = N_GMLP_GROUPS * GMLP_GROUP_DIM
D_MIX = D_ATTN + D_GMLP
D_IN_PROJ = 3 * D_ATTN + 2 * D_GMLP
D_FF = 4 * D_MODEL
MOBA_BLOCK = 256
MOBA_TOPK = 3
Q_CHUNK = 128
GMLP_CHUNK = 128
ROPE_THETA = 500000.0
ROPE_DIM = HEAD_DIM // 4
NORM_EPS = 1e-6
NEG_INF = -1e30

kernel_name = "hymba_moba_gmlp_hybrid_layer"


def rms_norm(x, g):
    xf = x.astype(jnp.float32)
    y = xf * lax.rsqrt(jnp.mean(xf * xf, axis=-1, keepdims=True) + NORM_EPS)
    return (y * g.astype(jnp.float32)).astype(x.dtype)


def layer_norm(x, g, b):
    xf = x.astype(jnp.float32)
    mu = jnp.mean(xf, axis=-1, keepdims=True)
    xc = xf - mu
    var = jnp.mean(xc * xc, axis=-1, keepdims=True)
    y = xc * lax.rsqrt(var + NORM_EPS) * g.astype(jnp.float32) + b.astype(jnp.float32)
    return y.astype(x.dtype)


def partial_rope(x, pos):
    half = ROPE_DIM // 2
    inv_freq = ROPE_THETA ** (-jnp.arange(half, dtype=jnp.float32) / half)
    ang = pos.astype(jnp.float32)[:, None] * inv_freq[None, :]
    cos = jnp.cos(ang)[None, :, None, :]
    sin = jnp.sin(ang)[None, :, None, :]
    xr = x[..., :ROPE_DIM].astype(jnp.float32)
    x1, x2 = xr[..., :half], xr[..., half:]
    rot = jnp.concatenate([x1 * cos - x2 * sin, x2 * cos + x1 * sin], axis=-1)
    return jnp.concatenate([rot.astype(x.dtype), x[..., ROPE_DIM:]], axis=-1)


def moba_attention(q, k, v):
    B, S, H, Dh = q.shape
    nb = -(-S // MOBA_BLOCK)
    pad = nb * MOBA_BLOCK - S
    k_p = jnp.pad(k, ((0, 0), (0, pad), (0, 0), (0, 0)))
    v_p = jnp.pad(v, ((0, 0), (0, pad), (0, 0), (0, 0)))
    k_blk = k_p.reshape(B, nb, MOBA_BLOCK, H, Dh).transpose(0, 3, 1, 2, 4)
    v_blk = v_p.reshape(B, nb, MOBA_BLOCK, H, Dh).transpose(0, 3, 1, 2, 4)

    k_mean = jnp.mean(k_blk.astype(jnp.float32), axis=3)
    pos = jnp.arange(S)
    own_blk = pos // MOBA_BLOCK
    gate = jnp.einsum('bshd,bhnd->bhsn', q.astype(jnp.float32), k_mean)
    past = jnp.arange(nb)[None, :] < own_blk[:, None]
    gate = jnp.where(past[None, None], gate, NEG_INF)
    topk = min(MOBA_TOPK, nb)
    _, sel = lax.top_k(gate, topk)
    sel_valid = sel < own_blk[:, None]

    scale = HEAD_DIM ** -0.5
    n_qc = S // Q_CHUNK
    qs = q * scale
    q_c = qs.reshape(B, n_qc, Q_CHUNK, H, Dh).transpose(0, 1, 3, 2, 4).reshape(B * n_qc, H, Q_CHUNK, Dh)
    sel_c = sel.reshape(B, H, n_qc, Q_CHUNK, topk).transpose(0, 2, 1, 3, 4).reshape(B * n_qc, H, Q_CHUNK, topk)
    val_c = sel_valid.reshape(B, H, n_qc, Q_CHUNK, topk).transpose(0, 2, 1, 3, 4).reshape(B * n_qc, H, Q_CHUNK, topk)
    b_idx = jnp.repeat(jnp.arange(B, dtype=jnp.int32), n_qc)
    c_idx = jnp.tile(jnp.arange(n_qc, dtype=jnp.int32), B)
    h_ix = jnp.arange(H)[:, None, None]
    n_sel = topk * MOBA_BLOCK

    def query_block(args):
        b, c, qc, selc, validc = args
        kb = k_blk[b]
        vb = v_blk[b]
        k_sel = kb[h_ix, selc]
        v_sel = vb[h_ix, selc]
        s_sel = jnp.einsum('hqd,hqnld->hqnl', qc, k_sel).astype(jnp.float32)
        s_sel = jnp.where(validc[..., None], s_sel, NEG_INF).reshape(H, Q_CHUNK, n_sel)
        i = (c * Q_CHUNK) // MOBA_BLOCK
        k_own = kb[:, i]
        v_own = vb[:, i]
        q_pos = c * Q_CHUNK + jnp.arange(Q_CHUNK)
        k_pos = i * MOBA_BLOCK + jnp.arange(MOBA_BLOCK)
        s_own = jnp.einsum('hqd,hld->hql', qc, k_own).astype(jnp.float32)
        s_own = jnp.where(k_pos[None, None, :] <= q_pos[None, :, None], s_own, NEG_INF)
        probs = jax.nn.softmax(jnp.concatenate([s_sel, s_own], axis=-1), axis=-1).astype(v.dtype)
        p_sel = probs[..., :n_sel].reshape(H, Q_CHUNK, topk, MOBA_BLOCK)
        p_own = probs[..., n_sel:]
        return (jnp.einsum('hqnl,hqnld->hqd', p_sel, v_sel)
                + jnp.einsum('hql,hld->hqd', p_own, v_own))

    o = lax.map(query_block, (b_idx, c_idx, q_c, sel_c, val_c))
    return o.reshape(B, n_qc, H, Q_CHUNK, Dh).transpose(0, 1, 3, 2, 4).reshape(B, S, H * Dh)


def gmlp_spatial_gating(u, gv, ln_g, ln_b, w_s, b_s):
    B, S, _ = gv.shape
    vn = layer_norm(gv, ln_g, ln_b)
    nc = S // GMLP_CHUNK
    vc = vn.reshape(B, nc, GMLP_CHUNK, N_GMLP_GROUPS, GMLP_GROUP_DIM)
    causal = jnp.tril(jnp.ones((GMLP_CHUNK, GMLP_CHUNK), dtype=bool))
    w = jnp.where(causal[None], w_s, 0.0)
    mixed = jnp.einsum('gts,bcsgd->bctgd', w, vc) + b_s.T[None, None, :, :, None]
    return u * mixed.reshape(B, S, D_GMLP)


def setup_inputs(seed: int = 0) -> dict:
    key = jax.random.key(seed)
    ks = jax.random.split(key, 20)
    f32 = jnp.float32

    def nrm(k, shape, scale):
        return jax.random.normal(k, shape, f32) * scale

    def gain(k, shape):
        return 1.0 + 0.05 * jax.random.normal(k, shape, f32)

    return {
        "x": jax.random.normal(ks[0], (BATCH, SEQ, D_MODEL), f32),
        "p": jax.random.normal(ks[1], (DEPTH, BATCH, SEQ, D_PLE), f32),
        "mix_pre_g": gain(ks[2], (DEPTH, D_MODEL)),
        "w_in": nrm(ks[3], (DEPTH, D_MODEL, D_IN_PROJ), D_MODEL ** -0.5),
        "gmlp_ln_g": gain(ks[4], (DEPTH, D_GMLP)),
        "gmlp_ln_b": nrm(ks[5], (DEPTH, D_GMLP), 0.02),
        "w_s": nrm(ks[6], (DEPTH, N_GMLP_GROUPS, GMLP_CHUNK, GMLP_CHUNK), GMLP_CHUNK ** -0.5),
        "b_s": 1.0 + 0.1 * jax.random.normal(ks[7], (DEPTH, N_GMLP_GROUPS, GMLP_CHUNK), f32),
        "attn_out_g": gain(ks[8], (DEPTH, D_ATTN)),
        "gmlp_out_g": gain(ks[9], (DEPTH, D_GMLP)),
        "w_o": nrm(ks[10], (DEPTH, D_MIX, D_MODEL), D_MIX ** -0.5),
        "mix_post_g": gain(ks[11], (DEPTH, D_MODEL)),
        "mlp_pre_g": gain(ks[12], (DEPTH, D_MODEL)),
        "w_up": nrm(ks[13], (DEPTH, D_MODEL, D_FF), D_MODEL ** -0.5),
        "w_down": nrm(ks[14], (DEPTH, D_FF, D_MODEL), D_FF ** -0.5),
        "mlp_post_g": gain(ks[15], (DEPTH, D_MODEL)),
        "w_ple": nrm(ks[16], (DEPTH, D_PLE, D_MODEL), D_PLE ** -0.5),
        "w_ple_gate": nrm(ks[17], (DEPTH, D_MODEL, D_MODEL), D_MODEL ** -0.5),
        "b_ple_gate": nrm(ks[18], (DEPTH, D_MODEL), 0.02),
        "ple_post_g": gain(ks[19], (DEPTH, D_MODEL)),
    }


def reference(x, p, mix_pre_g, w_in, gmlp_ln_g, gmlp_ln_b, w_s, b_s, attn_out_g,
              gmlp_out_g, w_o, mix_post_g, mlp_pre_g, w_up, w_down, mlp_post_g,
              w_ple, w_ple_gate, b_ple_gate, ple_post_g):
    B, S, _ = x.shape
    pos = jnp.arange(S, dtype=jnp.int32)
    h = x
    for i in range(DEPTH):
        hn = rms_norm(h, mix_pre_g[i])
        proj = jnp.einsum('bsd,de->bse', hn, w_in[i])
        q, k, v, u, gv = jnp.split(
            proj, [D_ATTN, 2 * D_ATTN, 3 * D_ATTN, 3 * D_ATTN + D_GMLP], axis=-1)
        q = partial_rope(q.reshape(B, S, N_ATTN_HEADS, HEAD_DIM), pos)
        k = partial_rope(k.reshape(B, S, N_ATTN_HEADS, HEAD_DIM), pos)
        v = v.reshape(B, S, N_ATTN_HEADS, HEAD_DIM)
        attn_out = moba_attention(q, k, v)
        gmlp_out = gmlp_spatial_gating(jax.nn.gelu(u, approximate=False),
                                       jax.nn.gelu(gv, approximate=False),
                                       gmlp_ln_g[i], gmlp_ln_b[i], w_s[i], b_s[i])
        mixed = jnp.concatenate([rms_norm(attn_out, attn_out_g[i]),
                                 rms_norm(gmlp_out, gmlp_out_g[i])], axis=-1)
        h = h + rms_norm(jnp.einsum('bse,ed->bsd', mixed, w_o[i]), mix_post_g[i])
        hn = rms_norm(h, mlp_pre_g[i])
        ff = jnp.square(jax.nn.relu(jnp.einsum('bsd,df->bsf', hn, w_up[i])))
        h = h + rms_norm(jnp.einsum('bsf,fd->bsd', ff, w_down[i]), mlp_post_g[i])
        gate = jax.nn.sigmoid(jnp.einsum('bsd,de->bse', h, w_ple_gate[i]) + b_ple_gate[i])
        ple = jnp.einsum('bsp,pd->bsd', p[i], w_ple[i]) * gate
        h = h + rms_norm(ple, ple_post_g[i])
    return h
```

```python
import functools
import math

import jax
import jax.numpy as jnp
from jax import lax
from jax.experimental import pallas as pl
from jax.experimental.pallas import tpu as pltpu

D_MODEL = 1024
D_PLE = 256
N_HEADS = 8
HEAD_DIM = 64
D_ATTN = N_HEADS * HEAD_DIM
N_GROUPS = 8
GROUP_DIM = 64
D_GMLP = N_GROUPS * GROUP_DIM
D_FF = 4 * D_MODEL
MOBA_BLOCK = 256
MOBA_TOPK = 3
GMLP_CHUNK = 128
ROPE_THETA = 500000.0
ROPE_DIM = HEAD_DIM // 4
ROPE_HALF = ROPE_DIM // 2
NORM_EPS = 1e-6
NEG_INF = -1e30

LANES = 128
HEADS_PER_LANE_TILE = LANES // HEAD_DIM
VMEM_LIMIT_BYTES = 48 * 1024 * 1024
MLP_TOKENS = 512
FF_CHUNK = 1024

F32 = jnp.float32
BF16 = jnp.bfloat16


def _dot(a, b):
    return jnp.dot(a, b, preferred_element_type=F32)


def _dot_nt(a, b):
    return lax.dot_general(a, b, (((1,), (1,)), ((), ())), preferred_element_type=F32)


def _dot_tn(a, b):
    return lax.dot_general(a, b, (((0,), (0,)), ((), ())), preferred_element_type=F32)


def _rms_rows(x, g):
    ms = jnp.mean(x * x, axis=-1, keepdims=True)
    return x * lax.rsqrt(ms + NORM_EPS) * g


def _gelu_exact(x):
    return 0.5 * x * (1.0 + lax.erf(x * math.sqrt(0.5)))


def _inproj_kernel(x_ref, g_ref, wft_ref, wtok_ref, cost_ref, sint_ref,
                   cosk_ref, sak_ref, sbk_ref, lng_ref, lnb_ref,
                   qt_ref, k_ref, vt_ref, km_ref, ug_ref, vn_ref):
    x = x_ref[0]
    hn = _rms_rows(x, g_ref[...]).astype(BF16)

    ft = _dot_nt(wft_ref[...], hn)
    cos_t = cost_ref[...]
    sin_t = sint_ref[...]
    pieces = []
    for h in range(N_HEADS):
        r0 = h * HEAD_DIM
        x1 = ft[r0:r0 + ROPE_HALF]
        x2 = ft[r0 + ROPE_HALF:r0 + ROPE_DIM]
        pieces.append(x1 * cos_t - x2 * sin_t)
        pieces.append(x2 * cos_t + x1 * sin_t)
        pieces.append(ft[r0 + ROPE_DIM:r0 + HEAD_DIM])
    qt_ref[0, 0] = jnp.concatenate(pieces, axis=0).astype(BF16)
    vt_ref[0, 0] = ft[D_ATTN:].astype(BF16)

    tok = _dot(hn, wtok_ref[...])
    cos_k = cosk_ref[...]
    sa_k = sak_ref[...]
    sb_k = sbk_ref[...]
    kparts = []
    for j in range(D_ATTN // LANES):
        kb = tok[:, j * LANES:(j + 1) * LANES]
        up = pltpu.roll(kb, LANES - ROPE_HALF, axis=1)
        dn = pltpu.roll(kb, ROPE_HALF, axis=1)
        kparts.append(kb * cos_k + up * sa_k + dn * sb_k)
    k = jnp.concatenate(kparts, axis=1)
    k_ref[0, 0] = k.astype(BF16)
    km_ref[0, 0] = jnp.mean(k, axis=0, keepdims=True)

    u = tok[:, D_ATTN:D_ATTN + D_GMLP]
    gv = tok[:, D_ATTN + D_GMLP:]
    ug_ref[0] = _gelu_exact(u).astype(BF16)
    g2 = _gelu_exact(gv)
    mu = jnp.mean(g2, axis=-1, keepdims=True)
    xc = g2 - mu
    var = jnp.mean(xc * xc, axis=-1, keepdims=True)
    vn = xc * lax.rsqrt(var + NORM_EPS) * lng_ref[...] + lnb_ref[...]
    vn_ref[0] = vn.astype(BF16)


def _mix_kernel(qt_ref, k_ref, vt_ref, km_ref, ug_ref, vn_ref, ws_ref, bs_ref,
                ag_ref, gg_ref, woa_ref, wog_ref, pg_ref, x_ref, o_ref, sel_scr):
    i = pl.program_id(1)
    nb = km_ref.shape[1]
    tq = MOBA_BLOCK

    blk_row = lax.broadcasted_iota(jnp.int32, (nb, tq), 0)
    past = blk_row < i
    kpos = lax.broadcasted_iota(jnp.int32, (MOBA_BLOCK, tq), 0)
    qpos = lax.broadcasted_iota(jnp.int32, (MOBA_BLOCK, tq), 1)
    causal = kpos <= qpos
    sub = lax.broadcasted_iota(jnp.int32, (LANES, tq), 0)

    km = km_ref[0]
    km_hi = km.astype(BF16)
    r1 = km - km_hi.astype(F32)
    km_mid = r1.astype(BF16)
    km_lo = (r1 - km_mid.astype(F32)).astype(BF16)
    km3 = jnp.concatenate([km_hi, km_mid, km_lo], axis=0)

    heads_out = []
    for hp in range(N_HEADS // HEADS_PER_LANE_TILE):
        lanes = slice(hp * LANES, (hp + 1) * LANES)
        qp = qt_ref[0, 0, lanes, :]
        km3p = km3[:, lanes]
        for hh in range(HEADS_PER_LANE_TILE):
            h = hp * HEADS_PER_LANE_TILE + hh
            in_head = (sub >= hh * HEAD_DIM) & (sub < (hh + 1) * HEAD_DIM)
            qz = jnp.where(in_head, qp, jnp.zeros_like(qp))

            g3 = _dot(km3p, qz)
            gate = g3[0:nb] + g3[nb:2 * nb] + g3[2 * nb:3 * nb]
            gate = jnp.where(past, gate, NEG_INF)
            rank = jnp.zeros((nb, tq), jnp.int32)
            for m in range(nb):
                gm = gate[m:m + 1, :]
                beats = (gm > gate) | ((gm == gate) & (m < blk_row))
                rank = rank + beats.astype(jnp.int32)
            sel = past & (rank < MOBA_TOPK)
            sel_scr[h] = sel.astype(F32)

            vrows = slice(h * HEAD_DIM, (h + 1) * HEAD_DIM)

            st = _dot(k_ref[0, i, :, lanes], qz)
            st = jnp.where(causal, st, NEG_INF)
            m0 = jnp.max(st, axis=0, keepdims=True)
            p = jnp.exp(st - m0)
            l0 = jnp.sum(p, axis=0, keepdims=True)
            acc0 = _dot(vt_ref[0, i, vrows, :], p.astype(BF16))

            def body(kb, carry, h=h, lanes=lanes, vrows=vrows, qz=qz):
                m_run, l_run, acc = carry
                s = _dot(k_ref[0, kb, :, lanes], qz)
                keep = sel_scr[h, pl.ds(kb, 1), :] > 0.5
                s = jnp.where(keep, s, NEG_INF)
                m_new = jnp.maximum(m_run, jnp.max(s, axis=0, keepdims=True))
                alpha = jnp.exp(m_run - m_new)
                pb = jnp.exp(s - m_new)
                l_new = alpha * l_run + jnp.sum(pb, axis=0, keepdims=True)
                acc_new = alpha * acc + _dot(vt_ref[0, kb, vrows, :], pb.astype(BF16))
                return m_new, l_new, acc_new

            _, l_fin, acc_fin = lax.fori_loop(0, i, body, (m0, l0, acc0))
            heads_out.append(acc_fin / l_fin)

    attn_t = jnp.concatenate(heads_out, axis=0)
    ms_a = jnp.mean(attn_t * attn_t, axis=0, keepdims=True)
    attn_n = (attn_t * lax.rsqrt(ms_a + NORM_EPS) * ag_ref[...]).astype(BF16)
    y = _dot_tn(attn_n, woa_ref[...])

    vn = vn_ref[0]
    t_idx = lax.broadcasted_iota(jnp.int32, (GMLP_CHUNK, GMLP_CHUNK), 0)
    s_idx = lax.broadcasted_iota(jnp.int32, (GMLP_CHUNK, GMLP_CHUNK), 1)
    tril = s_idx <= t_idx
    lane = lax.broadcasted_iota(jnp.int32, (GMLP_CHUNK, LANES), 1)
    first_group = lane < GROUP_DIM
    wmask = [jnp.where(tril, ws_ref[g], 0.0).astype(BF16) for g in range(N_GROUPS)]
    rows = []
    for c in range(tq // GMLP_CHUNK):
        cols = []
        for gp in range(D_GMLP // LANES):
            vp = vn[c * GMLP_CHUNK:(c + 1) * GMLP_CHUNK, gp * LANES:(gp + 1) * LANES]
            m_a = _dot(wmask[2 * gp], vp)
            m_b = _dot(wmask[2 * gp + 1], vp)
            cols.append(jnp.where(first_group, m_a, m_b))
        rows.append(jnp.concatenate(cols, axis=1) + bs_ref[...])
    mixed = jnp.concatenate(rows, axis=0)
    gm_out = ug_ref[0].astype(F32) * mixed
    gm_n = _rms_rows(gm_out, gg_ref[...]).astype(BF16)
    y = y + _dot(gm_n, wog_ref[...])

    o_ref[0] = x_ref[0] + _rms_rows(y, pg_ref[...])


def _mlp_kernel(h_ref, p_ref, g1_ref, wup_ref, wdn_ref, g2_ref, wgate_ref,
                bgate_ref, wple_ref, g3_ref, o_ref):
    h = h_ref[...]
    hn = _rms_rows(h, g1_ref[...]).astype(BF16)
    acc = jnp.zeros(h.shape, F32)
    for c in range(D_FF // FF_CHUNK):
        cs = slice(c * FF_CHUNK, (c + 1) * FF_CHUNK)
        up = jnp.maximum(_dot(hn, wup_ref[:, cs]), 0.0)
        acc = acc + _dot((up * up).astype(BF16), wdn_ref[cs, :])
    h = h + _rms_rows(acc, g2_ref[...])
    z = _dot(h.astype(BF16), wgate_ref[...]) + bgate_ref[...]
    gate = 1.0 / (1.0 + jnp.exp(-z))
    ple = _dot(p_ref[...].astype(BF16), wple_ref[...]) * gate
    o_ref[...] = h + _rms_rows(ple, g3_ref[...])


def _const_spec(shape):
    zeros = (0,) * len(shape)
    return pl.BlockSpec(shape, lambda *_: zeros, pipeline_mode=pl.Buffered(1))


def _rope_tables(seq):
    inv_freq = ROPE_THETA ** (-jnp.arange(ROPE_HALF, dtype=F32) / ROPE_HALF)
    ang = jnp.arange(seq, dtype=F32)[:, None] * inv_freq[None, :]
    cos, sin = jnp.cos(ang), jnp.sin(ang)
    cos_t, sin_t = cos.T, sin.T
    ones = jnp.ones((seq, HEAD_DIM - ROPE_DIM), F32)
    zeros = jnp.zeros((seq, HEAD_DIM - ROPE_DIM), F32)
    zh = jnp.zeros((seq, ROPE_HALF), F32)
    cos_h = jnp.concatenate([cos, cos, ones], axis=1)
    sa_h = jnp.concatenate([-sin, zh, zeros], axis=1)
    sb_h = jnp.concatenate([zh, sin, zeros], axis=1)
    tile = lambda t: jnp.concatenate([t] * HEADS_PER_LANE_TILE, axis=1)
    return cos_t, sin_t, tile(cos_h), tile(sa_h), tile(sb_h)


def _layer(h, p, mix_pre_g, w_in, gmlp_ln_g, gmlp_ln_b, w_s, b_s, attn_out_g,
           gmlp_out_g, w_o, mix_post_g, mlp_pre_g, w_up, w_down, mlp_post_g,
           w_ple, w_ple_gate, b_ple_gate, ple_post_g):
    B, S, D = h.shape
    assert D == D_MODEL and S % MOBA_BLOCK == 0
    nb = S // MOBA_BLOCK
    T = MOBA_BLOCK
    row = lambda v: v.reshape(1, -1).astype(F32)

    scale = HEAD_DIM ** -0.5
    wq = w_in[:, :D_ATTN] * scale
    wk = w_in[:, D_ATTN:2 * D_ATTN]
    wv = w_in[:, 2 * D_ATTN:3 * D_ATTN]
    wft = jnp.concatenate([wq, wv], axis=1).T.astype(BF16)
    wtok = jnp.concatenate([wk, w_in[:, 3 * D_ATTN:]], axis=1).astype(BF16)
    cos_t, sin_t, cos_k, sa_k, sb_k = _rope_tables(S)

    cparams = functools.partial(pltpu.CompilerParams, vmem_limit_bytes=VMEM_LIMIT_BYTES)

    qt, k4, vt, km, ug, vn = pl.pallas_call(
        _inproj_kernel,
        grid=(B, nb),
        in_specs=[
            pl.BlockSpec((1, T, D), lambda b, i: (b, i, 0)),
            _const_spec((1, D)),
            _const_spec((2 * D_ATTN, D)),
            _const_spec((D, D_ATTN + 2 * D_GMLP)),
            pl.BlockSpec((ROPE_HALF, T), lambda b, i: (0, i)),
            pl.BlockSpec((ROPE_HALF, T), lambda b, i: (0, i)),
            pl.BlockSpec((T, LANES), lambda b, i: (i, 0)),
            pl.BlockSpec((T, LANES), lambda b, i: (i, 0)),
            pl.BlockSpec((T, LANES), lambda b, i: (i, 0)),
            _const_spec((1, D_GMLP)),
            _const_spec((1, D_GMLP)),
        ],
        out_specs=[
            pl.BlockSpec((1, 1, D_ATTN, T), lambda b, i: (b, i, 0, 0)),
            pl.BlockSpec((1, 1, T, D_ATTN), lambda b, i: (b, i, 0, 0)),
            pl.BlockSpec((1, 1, D_ATTN, T), lambda b, i: (b, i, 0, 0)),
            pl.BlockSpec((1, 1, 1, D_ATTN), lambda b, i: (b, i, 0, 0)),
            pl.BlockSpec((1, T, D_GMLP), lambda b, i: (b, i, 0)),
            pl.BlockSpec((1, T, D_GMLP), lambda b, i: (b, i, 0)),
        ],
        out_shape=[
            jax.ShapeDtypeStruct((B, nb, D_ATTN, T), BF16),
            jax.ShapeDtypeStruct((B, nb, T, D_ATTN), BF16),
            jax.ShapeDtypeStruct((B, nb, D_ATTN, T), BF16),
            jax.ShapeDtypeStruct((B, nb, 1, D_ATTN), F32),
            jax.ShapeDtypeStruct((B, S, D_GMLP), BF16),
            jax.ShapeDtypeStruct((B, S, D_GMLP), BF16),
        ],
        compiler_params=cparams(dimension_semantics=("parallel", "parallel")),
        name="inproj",
    )(h, row(mix_pre_g), wft, wtok, cos_t, sin_t, cos_k, sa_k, sb_k,
      row(gmlp_ln_g), row(gmlp_ln_b))

    km = km.reshape(B, nb, D_ATTN)
    bs_exp = jnp.repeat(b_s.T, GROUP_DIM, axis=1).astype(F32)
    woa = w_o[:D_ATTN].astype(BF16)
    wog = w_o[D_ATTN:].astype(BF16)

    h1 = pl.pallas_call(
        _mix_kernel,
        grid=(B, nb),
        in_specs=[
            pl.BlockSpec((1, 1, D_ATTN, T), lambda b, i: (b, i, 0, 0)),
            pl.BlockSpec((1, nb, T, D_ATTN), lambda b, i: (b, 0, 0, 0)),
            pl.BlockSpec((1, nb, D_ATTN, T), lambda b, i: (b, 0, 0, 0)),
            pl.BlockSpec((1, nb, D_ATTN), lambda b, i: (b, 0, 0)),
            pl.BlockSpec((1, T, D_GMLP), lambda b, i: (b, i, 0)),
            pl.BlockSpec((1, T, D_GMLP), lambda b, i: (b, i, 0)),
            _const_spec((N_GROUPS, GMLP_CHUNK, GMLP_CHUNK)),
            _const_spec((GMLP_CHUNK, D_GMLP)),
            _const_spec((D_ATTN, 1)),
            _const_spec((1, D_GMLP)),
            _const_spec((D_ATTN, D)),
            _const_spec((D_GMLP, D)),
            _const_spec((1, D)),
            pl.BlockSpec((1, T, D), lambda b, i: (b, i, 0)),
        ],
        out_specs=pl.BlockSpec((1, T, D), lambda b, i: (b, i, 0)),
        out_shape=jax.ShapeDtypeStruct((B, S, D), F32),
        scratch_shapes=[pltpu.VMEM((N_HEADS, nb, T), F32)],
        compiler_params=cparams(dimension_semantics=("parallel", "arbitrary")),
        name="mix",
    )(qt, k4, vt, km, ug, vn, w_s.astype(F32), bs_exp,
      attn_out_g.reshape(-1, 1).astype(F32), row(gmlp_out_g), woa, wog,
      row(mix_post_g), h)

    n_tok = B * S
    tm = MLP_TOKENS
    out = pl.pallas_call(
        _mlp_kernel,
        grid=(n_tok // tm,),
        in_specs=[
            pl.BlockSpec((tm, D), lambda t: (t, 0)),
            pl.BlockSpec((tm, D_PLE), lambda t: (t, 0)),
            _const_spec((1, D)),
            _const_spec((D, D_FF)),
            _const_spec((D_FF, D)),
            _const_spec((1, D)),
            _const_spec((D, D)),
            _const_spec((1, D)),
            _const_spec((D_PLE, D)),
            _const_spec((1, D)),
        ],
        out_specs=pl.BlockSpec((tm, D), lambda t: (t, 0)),
        out_shape=jax.ShapeDtypeStruct((n_tok, D), F32),
        compiler_params=cparams(dimension_semantics=("parallel",)),
        name="mlp",
    )(h1.reshape(n_tok, D), p.reshape(n_tok, D_PLE), row(mlp_pre_g),
      w_up.astype(BF16), w_down.astype(BF16), row(mlp_post_g),
      w_ple_gate.astype(BF16), row(b_ple_gate), w_ple.astype(BF16), row(ple_post_g))
    return out.reshape(B, S, D)


def kernel(x, p, mix_pre_g, w_in, gmlp_ln_g, gmlp_ln_b, w_s, b_s, attn_out_g,
           gmlp_out_g, w_o, mix_post_g, mlp_pre_g, w_up, w_down, mlp_post_g,
           w_ple, w_ple_gate, b_ple_gate, ple_post_g):
    h = x
    for i in range(w_in.shape[0]):
        h = _layer(h, p[i], mix_pre_g[i], w_in[i], gmlp_ln_g[i], gmlp_ln_b[i],
                   w_s[i], b_s[i], attn_out_g[i], gmlp_out_g[i], w_o[i],
                   mix_post_g[i], mlp_pre_g[i], w_up[i], w_down[i], mlp_post_g[i],
                   w_ple[i], w_ple_gate[i], b_ple_gate[i], ple_post_g[i])
    return h
```

```python
import functools
import math

import jax
import jax.numpy as jnp
from jax import lax
from jax.experimental import pallas as pl
from jax.experimental.pallas import tpu as pltpu

D_MODEL = 1024
D_PLE = 256
N_HEADS = 8
HEAD_DIM = 64
D_ATTN = N_HEADS * HEAD_DIM
N_GROUPS = 8
GROUP_DIM = 64
D_GMLP = N_GROUPS * GROUP_DIM
D_FF = 4 * D_MODEL
MOBA_BLOCK = 256
MOBA_TOPK = 3
GMLP_CHUNK = 128
ROPE_THETA = 500000.0
ROPE_DIM = HEAD_DIM // 4
ROPE_HALF = ROPE_DIM // 2
NORM_EPS = 1e-6
NEG_INF = -1e30

LANES = 128
HEADS_PER_LANE_TILE = LANES // HEAD_DIM
BF16_SUBLANES = 16
V_ROWS = HEAD_DIM + BF16_SUBLANES
BIG = 1e30
VMEM_LIMIT_BYTES = 48 * 1024 * 1024
MLP_TOKENS = 512
FF_CHUNK = 1024

F32 = jnp.float32
BF16 = jnp.bfloat16


def _dot(a, b):
    return jnp.dot(a, b, preferred_element_type=F32)


def _dot_nt(a, b):
    return lax.dot_general(a, b, (((1,), (1,)), ((), ())), preferred_element_type=F32)


def _dot_tn(a, b):
    return lax.dot_general(a, b, (((0,), (0,)), ((), ())), preferred_element_type=F32)


def _rms_rows(x, g):
    ms = jnp.mean(x * x, axis=-1, keepdims=True)
    return x * lax.rsqrt(ms + NORM_EPS) * g


def _gelu_exact(x):
    return 0.5 * x * (1.0 + lax.erf(x * math.sqrt(0.5)))


def _inproj_kernel(x_ref, g_ref, wft_ref, wtok_ref, cost_ref, sint_ref,
                   cosk_ref, sak_ref, sbk_ref, lng_ref, lnb_ref,
                   qt_ref, k_ref, vt_ref, km_ref, ug_ref, vn_ref):
    x = x_ref[0]
    hn = _rms_rows(x, g_ref[...]).astype(BF16)

    ft = _dot_nt(wft_ref[...], hn)
    cos_t = cost_ref[...]
    sin_t = sint_ref[...]
    pieces = []
    for h in range(N_HEADS):
        r0 = h * HEAD_DIM
        x1 = ft[r0:r0 + ROPE_HALF]
        x2 = ft[r0 + ROPE_HALF:r0 + ROPE_DIM]
        pieces.append(x1 * cos_t - x2 * sin_t)
        pieces.append(x2 * cos_t + x1 * sin_t)
        pieces.append(ft[r0 + ROPE_DIM:r0 + HEAD_DIM])
    qt_ref[0, 0] = jnp.concatenate(pieces, axis=0).astype(BF16)
    ones = jnp.ones((V_ROWS - HEAD_DIM, ft.shape[1]), F32)
    vparts = []
    for h in range(N_HEADS):
        vparts.append(ft[D_ATTN + h * HEAD_DIM:D_ATTN + (h + 1) * HEAD_DIM])
        vparts.append(ones)
    vt_ref[0, 0] = jnp.concatenate(vparts, axis=0).astype(BF16)

    tok = _dot(hn, wtok_ref[...])
    cos_k = cosk_ref[...]
    sa_k = sak_ref[...]
    sb_k = sbk_ref[...]
    kparts = []
    for j in range(D_ATTN // LANES):
        kb = tok[:, j * LANES:(j + 1) * LANES]
        up = pltpu.roll(kb, LANES - ROPE_HALF, axis=1)
        dn = pltpu.roll(kb, ROPE_HALF, axis=1)
        kparts.append(kb * cos_k + up * sa_k + dn * sb_k)
    k = jnp.concatenate(kparts, axis=1)
    k_ref[0, 0] = k.astype(BF16)
    km_ref[0, 0] = jnp.mean(k, axis=0, keepdims=True)

    u = tok[:, D_ATTN:D_ATTN + D_GMLP]
    gv = tok[:, D_ATTN + D_GMLP:]
    ug_ref[0] = _gelu_exact(u).astype(BF16)
    g2 = _gelu_exact(gv)
    mu = jnp.mean(g2, axis=-1, keepdims=True)
    xc = g2 - mu
    var = jnp.mean(xc * xc, axis=-1, keepdims=True)
    vn = xc * lax.rsqrt(var + NORM_EPS) * lng_ref[...] + lnb_ref[...]
    vn_ref[0] = vn.astype(BF16)


def _mix_kernel(qt_ref, k_ref, vt_ref, km_ref, ug_ref, vn_ref, ws_ref, bs_ref,
                ag_ref, gg_ref, woa_ref, wog_ref, pg_ref, x_ref, o_ref,
                *scratch):
    sel_scr, qz_scr, m_scr, acc_scr, s_scr = (
        scratch[j * N_HEADS:(j + 1) * N_HEADS] for j in range(5))
    i = pl.program_id(1)
    nb = km_ref.shape[1]
    tq = MOBA_BLOCK

    blk_row = lax.broadcasted_iota(jnp.int32, (nb, tq), 0)
    past = blk_row < i
    sub = lax.broadcasted_iota(jnp.int32, (LANES, tq), 0)

    km = km_ref[0]
    km_hi = km.astype(BF16)
    r1 = km - km_hi.astype(F32)
    km_mid = r1.astype(BF16)
    km_lo = (r1 - km_mid.astype(F32)).astype(BF16)
    km3 = jnp.concatenate([km_hi, km_mid, km_lo], axis=0)

    def scores(blk, h):
        hp = h // HEADS_PER_LANE_TILE
        return _dot(k_ref[0, blk, :, hp * LANES:(hp + 1) * LANES], qz_scr[h][...])

    def v_rows(blk, h):
        return vt_ref[0, blk, h * V_ROWS:(h + 1) * V_ROWS, :]

    for hp in range(N_HEADS // HEADS_PER_LANE_TILE):
        lanes = slice(hp * LANES, (hp + 1) * LANES)
        qp = qt_ref[0, 0, lanes, :]
        km3p = km3[:, lanes]
        for hh in range(HEADS_PER_LANE_TILE):
            h = hp * HEADS_PER_LANE_TILE + hh
            in_head = (sub >= hh * HEAD_DIM) & (sub < (hh + 1) * HEAD_DIM)
            qz = jnp.where(in_head, qp, jnp.zeros_like(qp))
            qz_scr[h][...] = qz
            s_scr[h][...] = scores(0, h)

            g3 = _dot(km3p, qz)
            gate = g3[0:nb] + g3[nb:2 * nb] + g3[2 * nb:3 * nb]
            gate = jnp.where(past, gate, NEG_INF)
            rank = jnp.zeros((nb, tq), jnp.int32)
            for m in range(nb):
                gm = gate[m:m + 1, :]
                beats = (gm > gate) | ((gm == gate) & (m < blk_row))
                rank = rank + beats.astype(jnp.int32)
            sel = past & (rank < MOBA_TOPK)
            sel_scr[h][...] = sel.astype(F32)
            m_scr[h][...] = jnp.full((1, tq), NEG_INF, F32)
            acc_scr[h][...] = jnp.zeros((V_ROWS, tq), F32)

    vn = vn_ref[0]
    t_idx = lax.broadcasted_iota(jnp.int32, (GMLP_CHUNK, GMLP_CHUNK), 0)
    s_idx = lax.broadcasted_iota(jnp.int32, (GMLP_CHUNK, GMLP_CHUNK), 1)
    tril = s_idx <= t_idx
    lane = lax.broadcasted_iota(jnp.int32, (GMLP_CHUNK, LANES), 1)
    first_group = lane < GROUP_DIM
    wmask = [jnp.where(tril, ws_ref[g], 0.0).astype(BF16) for g in range(N_GROUPS)]
    rows = []
    for c in range(tq // GMLP_CHUNK):
        cols = []
        for gp in range(D_GMLP // LANES):
            vp = vn[c * GMLP_CHUNK:(c + 1) * GMLP_CHUNK, gp * LANES:(gp + 1) * LANES]
            m_a = _dot(wmask[2 * gp], vp)
            m_b = _dot(wmask[2 * gp + 1], vp)
            cols.append(jnp.where(first_group, m_a, m_b))
        rows.append(jnp.concatenate(cols, axis=1) + bs_ref[...])
    mixed = jnp.concatenate(rows, axis=0)
    gm_out = ug_ref[0].astype(F32) * mixed
    gm_n = _rms_rows(gm_out, gg_ref[...]).astype(BF16)
    o_ref[0] = _dot(gm_n, wog_ref[...])

    def body(kb, carry):
        for h in range(N_HEADS):
            s_ahead = scores(kb + 1, h)
            s = s_scr[h][...]
            keep = sel_scr[h][pl.ds(kb, 1), :] > 0.5
            m_run = m_scr[h][...]
            m_new = jnp.where(keep, jnp.maximum(m_run, jnp.max(s, axis=0, keepdims=True)), m_run)
            m_sub = jnp.where(keep, m_new, BIG)
            alpha = jnp.exp(m_run - m_new)
            pb = jnp.exp(s - m_sub).astype(BF16)
            m_scr[h][...] = m_new
            acc_scr[h][...] = alpha * acc_scr[h][...] + _dot(v_rows(kb, h), pb)
            s_scr[h][...] = s_ahead
        return carry

    lax.fori_loop(0, i, body, 0)

    kpos = lax.broadcasted_iota(jnp.int32, (MOBA_BLOCK, tq), 0)
    qpos = lax.broadcasted_iota(jnp.int32, (MOBA_BLOCK, tq), 1)
    causal = kpos <= qpos
    heads_out = []
    for h in range(N_HEADS):
        st = jnp.where(causal, s_scr[h][...], NEG_INF)
        m_run = m_scr[h][...]
        m_new = jnp.maximum(m_run, jnp.max(st, axis=0, keepdims=True))
        alpha = jnp.exp(m_run - m_new)
        p = jnp.exp(st - m_new).astype(BF16)
        acc = alpha * acc_scr[h][...] + _dot(v_rows(i, h), p)
        heads_out.append(acc[0:HEAD_DIM] / acc[HEAD_DIM:HEAD_DIM + 1])
    attn_t = jnp.concatenate(heads_out, axis=0)
    ms_a = jnp.mean(attn_t * attn_t, axis=0, keepdims=True)
    attn_n = (attn_t * lax.rsqrt(ms_a + NORM_EPS) * ag_ref[...]).astype(BF16)
    y = o_ref[0] + _dot_tn(attn_n, woa_ref[...])
    o_ref[0] = x_ref[0] + _rms_rows(y, pg_ref[...])


def _mlp_kernel(h_ref, p_ref, g1_ref, wup_ref, wdn_ref, g2_ref, wgate_ref,
                bgate_ref, wple_ref, g3_ref, o_ref):
    h = h_ref[...]
    hn = _rms_rows(h, g1_ref[...]).astype(BF16)
    acc = jnp.zeros(h.shape, F32)
    for c in range(D_FF // FF_CHUNK):
        cs = slice(c * FF_CHUNK, (c + 1) * FF_CHUNK)
        up = jnp.maximum(_dot(hn, wup_ref[:, cs]), 0.0)
        acc = acc + _dot((up * up).astype(BF16), wdn_ref[cs, :])
    h = h + _rms_rows(acc, g2_ref[...])
    z = _dot(h.astype(BF16), wgate_ref[...]) + bgate_ref[...]
    gate = 1.0 / (1.0 + jnp.exp(-z))
    ple = _dot(p_ref[...].astype(BF16), wple_ref[...]) * gate
    o_ref[...] = h + _rms_rows(ple, g3_ref[...])


def _const_spec(shape):
    zeros = (0,) * len(shape)
    return pl.BlockSpec(shape, lambda *_: zeros, pipeline_mode=pl.Buffered(1))


def _rope_tables(seq):
    inv_freq = ROPE_THETA ** (-jnp.arange(ROPE_HALF, dtype=F32) / ROPE_HALF)
    ang = jnp.arange(seq, dtype=F32)[:, None] * inv_freq[None, :]
    cos, sin = jnp.cos(ang), jnp.sin(ang)
    cos_t, sin_t = cos.T, sin.T
    ones = jnp.ones((seq, HEAD_DIM - ROPE_DIM), F32)
    zeros = jnp.zeros((seq, HEAD_DIM - ROPE_DIM), F32)
    zh = jnp.zeros((seq, ROPE_HALF), F32)
    cos_h = jnp.concatenate([cos, cos, ones], axis=1)
    sa_h = jnp.concatenate([-sin, zh, zeros], axis=1)
    sb_h = jnp.concatenate([zh, sin, zeros], axis=1)
    tile = lambda t: jnp.concatenate([t] * HEADS_PER_LANE_TILE, axis=1)
    return cos_t, sin_t, tile(cos_h), tile(sa_h), tile(sb_h)


def _layer(h, p, mix_pre_g, w_in, gmlp_ln_g, gmlp_ln_b, w_s, b_s, attn_out_g,
           gmlp_out_g, w_o, mix_post_g, mlp_pre_g, w_up, w_down, mlp_post_g,
           w_ple, w_ple_gate, b_ple_gate, ple_post_g):
    B, S, D = h.shape
    assert D == D_MODEL and S % MOBA_BLOCK == 0
    nb = S // MOBA_BLOCK
    T = MOBA_BLOCK
    row = lambda v: v.reshape(1, -1).astype(F32)

    scale = HEAD_DIM ** -0.5
    wq = w_in[:, :D_ATTN] * scale
    wk = w_in[:, D_ATTN:2 * D_ATTN]
    wv = w_in[:, 2 * D_ATTN:3 * D_ATTN]
    wft = jnp.concatenate([wq, wv], axis=1).T.astype(BF16)
    wtok = jnp.concatenate([wk, w_in[:, 3 * D_ATTN:]], axis=1).astype(BF16)
    cos_t, sin_t, cos_k, sa_k, sb_k = _rope_tables(S)

    cparams = functools.partial(pltpu.CompilerParams, vmem_limit_bytes=VMEM_LIMIT_BYTES)

    qt, k4, vt, km, ug, vn = pl.pallas_call(
        _inproj_kernel,
        grid=(B, nb),
        in_specs=[
            pl.BlockSpec((1, T, D), lambda b, i: (b, i, 0)),
            _const_spec((1, D)),
            _const_spec((2 * D_ATTN, D)),
            _const_spec((D, D_ATTN + 2 * D_GMLP)),
            pl.BlockSpec((ROPE_HALF, T), lambda b, i: (0, i)),
            pl.BlockSpec((ROPE_HALF, T), lambda b, i: (0, i)),
            pl.BlockSpec((T, LANES), lambda b, i: (i, 0)),
            pl.BlockSpec((T, LANES), lambda b, i: (i, 0)),
            pl.BlockSpec((T, LANES), lambda b, i: (i, 0)),
            _const_spec((1, D_GMLP)),
            _const_spec((1, D_GMLP)),
        ],
        out_specs=[
            pl.BlockSpec((1, 1, D_ATTN, T), lambda b, i: (b, i, 0, 0)),
            pl.BlockSpec((1, 1, T, D_ATTN), lambda b, i: (b, i, 0, 0)),
            pl.BlockSpec((1, 1, N_HEADS * V_ROWS, T), lambda b, i: (b, i, 0, 0)),
            pl.BlockSpec((1, 1, 1, D_ATTN), lambda b, i: (b, i, 0, 0)),
            pl.BlockSpec((1, T, D_GMLP), lambda b, i: (b, i, 0)),
            pl.BlockSpec((1, T, D_GMLP), lambda b, i: (b, i, 0)),
        ],
        out_shape=[
            jax.ShapeDtypeStruct((B, nb, D_ATTN, T), BF16),
            jax.ShapeDtypeStruct((B, nb, T, D_ATTN), BF16),
            jax.ShapeDtypeStruct((B, nb, N_HEADS * V_ROWS, T), BF16),
            jax.ShapeDtypeStruct((B, nb, 1, D_ATTN), F32),
            jax.ShapeDtypeStruct((B, S, D_GMLP), BF16),
            jax.ShapeDtypeStruct((B, S, D_GMLP), BF16),
        ],
        compiler_params=cparams(dimension_semantics=("parallel", "parallel")),
        name="inproj",
    )(h, row(mix_pre_g), wft, wtok, cos_t, sin_t, cos_k, sa_k, sb_k,
      row(gmlp_ln_g), row(gmlp_ln_b))

    km = km.reshape(B, nb, D_ATTN)
    bs_exp = jnp.repeat(b_s.T, GROUP_DIM, axis=1).astype(F32)
    woa = w_o[:D_ATTN].astype(BF16)
    wog = w_o[D_ATTN:].astype(BF16)

    h1 = pl.pallas_call(
        _mix_kernel,
        grid=(B, nb),
        in_specs=[
            pl.BlockSpec((1, 1, D_ATTN, T), lambda b, i: (b, i, 0, 0)),
            pl.BlockSpec((1, nb, T, D_ATTN), lambda b, i: (b, 0, 0, 0)),
            pl.BlockSpec((1, nb, N_HEADS * V_ROWS, T), lambda b, i: (b, 0, 0, 0)),
            pl.BlockSpec((1, nb, D_ATTN), lambda b, i: (b, 0, 0)),
            pl.BlockSpec((1, T, D_GMLP), lambda b, i: (b, i, 0)),
            pl.BlockSpec((1, T, D_GMLP), lambda b, i: (b, i, 0)),
            _const_spec((N_GROUPS, GMLP_CHUNK, GMLP_CHUNK)),
            _const_spec((GMLP_CHUNK, D_GMLP)),
            _const_spec((D_ATTN, 1)),
            _const_spec((1, D_GMLP)),
            _const_spec((D_ATTN, D)),
            _const_spec((D_GMLP, D)),
            _const_spec((1, D)),
            pl.BlockSpec((1, T, D), lambda b, i: (b, i, 0)),
        ],
        out_specs=pl.BlockSpec((1, T, D), lambda b, i: (b, i, 0)),
        out_shape=jax.ShapeDtypeStruct((B, S, D), F32),
        scratch_shapes=([pltpu.VMEM((nb, T), F32)] * N_HEADS
                        + [pltpu.VMEM((LANES, T), BF16)] * N_HEADS
                        + [pltpu.VMEM((1, T), F32)] * N_HEADS
                        + [pltpu.VMEM((V_ROWS, T), F32)] * N_HEADS
                        + [pltpu.VMEM((MOBA_BLOCK, T), F32)] * N_HEADS),
        compiler_params=cparams(dimension_semantics=("parallel", "arbitrary")),
        name="mix",
    )(qt, k4, vt, km, ug, vn, w_s.astype(F32), bs_exp,
      attn_out_g.reshape(-1, 1).astype(F32), row(gmlp_out_g), woa, wog,
      row(mix_post_g), h)

    n_tok = B * S
    tm = MLP_TOKENS
    out = pl.pallas_call(
        _mlp_kernel,
        grid=(n_tok // tm,),
        in_specs=[
            pl.BlockSpec((tm, D), lambda t: (t, 0)),
            pl.BlockSpec((tm, D_PLE), lambda t: (t, 0)),
            _const_spec((1, D)),
            _const_spec((D, D_FF)),
            _const_spec((D_FF, D)),
            _const_spec((1, D)),
            _const_spec((D, D)),
            _const_spec((1, D)),
            _const_spec((D_PLE, D)),
            _const_spec((1, D)),
        ],
        out_specs=pl.BlockSpec((tm, D), lambda t: (t, 0)),
        out_shape=jax.ShapeDtypeStruct((n_tok, D), F32),
        compiler_params=cparams(dimension_semantics=("parallel",)),
        name="mlp",
    )(h1.reshape(n_tok, D), p.reshape(n_tok, D_PLE), row(mlp_pre_g),
      w_up.astype(BF16), w_down.astype(BF16), row(mlp_post_g),
      w_ple_gate.astype(BF16), row(b_ple_gate), w_ple.astype(BF16), row(ple_post_g))
    return out.reshape(B, S, D)


def kernel(x, p, mix_pre_g, w_in, gmlp_ln_g, gmlp_ln_b, w_s, b_s, attn_out_g,
           gmlp_out_g, w_o, mix_post_g, mlp_pre_g, w_up, w_down, mlp_post_g,
           w_ple, w_ple_gate, b_ple_gate, ple_post_g):
    h = x
    for i in range(w_in.shape[0]):
        h = _layer(h, p[i], mix_pre_g[i], w_in[i], gmlp_ln_g[i], gmlp_ln_b[i],
                   w_s[i], b_s[i], attn_out_g[i], gmlp_out_g[i], w_o[i],
                   mix_post_g[i], mlp_pre_g[i], w_up[i], w_down[i], mlp_post_g[i],
                   w_ple[i], w_ple_gate[i], b_ple_gate[i], ple_post_g[i])
    return h
```

```python
import functools
import math

import jax
import jax.numpy as jnp
from jax import lax
from jax.experimental import pallas as pl
from jax.experimental.pallas import tpu as pltpu

D_MODEL = 1024
D_PLE = 256
N_HEADS = 8
HEAD_DIM = 64
D_ATTN = N_HEADS * HEAD_DIM
N_GROUPS = 8
GROUP_DIM = 64
D_GMLP = N_GROUPS * GROUP_DIM
D_FF = 4 * D_MODEL
MOBA_BLOCK = 256
MOBA_TOPK = 3
GMLP_CHUNK = 128
ROPE_THETA = 500000.0
ROPE_DIM = HEAD_DIM // 4
ROPE_HALF = ROPE_DIM // 2
NORM_EPS = 1e-6
NEG_INF = -1e30

LANES = 128
HEADS_PER_LANE_TILE = LANES // HEAD_DIM
BF16_SUBLANES = 16
V_ROWS = HEAD_DIM + BF16_SUBLANES
BIG = 1e30
VMEM_LIMIT_BYTES = 48 * 1024 * 1024
MLP_TOKENS = 512
FF_CHUNK = 1024

F32 = jnp.float32
BF16 = jnp.bfloat16


def _dot(a, b):
    return jnp.dot(a, b, preferred_element_type=F32)


def _dot_nt(a, b):
    return lax.dot_general(a, b, (((1,), (1,)), ((), ())), preferred_element_type=F32)


def _dot_tn(a, b):
    return lax.dot_general(a, b, (((0,), (0,)), ((), ())), preferred_element_type=F32)


def _rms_rows(x, g):
    ms = jnp.mean(x * x, axis=-1, keepdims=True)
    return x * lax.rsqrt(ms + NORM_EPS) * g


def _gelu_exact(x):
    return 0.5 * x * (1.0 + lax.erf(x * math.sqrt(0.5)))


def _inproj_kernel(x_ref, g_ref, wft_ref, wtok_ref, cost_ref, sint_ref,
                   cosk_ref, sak_ref, sbk_ref, lng_ref, lnb_ref,
                   qt_ref, k_ref, vt_ref, km_ref, ug_ref, vn_ref):
    x = x_ref[0]
    hn = _rms_rows(x, g_ref[...]).astype(BF16)

    ft = _dot_nt(wft_ref[...], hn)
    cos_t = cost_ref[...]
    sin_t = sint_ref[...]
    pieces = []
    for h in range(N_HEADS):
        r0 = h * HEAD_DIM
        x1 = ft[r0:r0 + ROPE_HALF]
        x2 = ft[r0 + ROPE_HALF:r0 + ROPE_DIM]
        pieces.append(x1 * cos_t - x2 * sin_t)
        pieces.append(x2 * cos_t + x1 * sin_t)
        pieces.append(ft[r0 + ROPE_DIM:r0 + HEAD_DIM])
    qt_ref[0, 0] = jnp.concatenate(pieces, axis=0).astype(BF16)
    ones = jnp.ones((V_ROWS - HEAD_DIM, ft.shape[1]), F32)
    vparts = []
    for h in range(N_HEADS):
        vparts.append(ft[D_ATTN + h * HEAD_DIM:D_ATTN + (h + 1) * HEAD_DIM])
        vparts.append(ones)
    vt_ref[0, 0] = jnp.concatenate(vparts, axis=0).astype(BF16)

    tok = _dot(hn, wtok_ref[...])
    cos_k = cosk_ref[...]
    sa_k = sak_ref[...]
    sb_k = sbk_ref[...]
    kparts = []
    for j in range(D_ATTN // LANES):
        kb = tok[:, j * LANES:(j + 1) * LANES]
        up = pltpu.roll(kb, LANES - ROPE_HALF, axis=1)
        dn = pltpu.roll(kb, ROPE_HALF, axis=1)
        kparts.append(kb * cos_k + up * sa_k + dn * sb_k)
    k = jnp.concatenate(kparts, axis=1)
    k_ref[0, 0] = k.astype(BF16)
    km_ref[0, 0] = jnp.mean(k, axis=0, keepdims=True)

    u = tok[:, D_ATTN:D_ATTN + D_GMLP]
    gv = tok[:, D_ATTN + D_GMLP:]
    ug_ref[0] = _gelu_exact(u).astype(BF16)
    g2 = _gelu_exact(gv)
    mu = jnp.mean(g2, axis=-1, keepdims=True)
    xc = g2 - mu
    var = jnp.mean(xc * xc, axis=-1, keepdims=True)
    vn = xc * lax.rsqrt(var + NORM_EPS) * lng_ref[...] + lnb_ref[...]
    vn_ref[0] = vn.astype(BF16)


def _mix_kernel(qt_ref, k_ref, vt_ref, km_ref, ug_ref, vn_ref, ws_ref, bs_ref,
                ag_ref, gg_ref, woa_ref, wog_ref, pg_ref, x_ref, o_ref,
                *scratch):
    sel_scr, qz_scr, m_scr, acc_scr, s_scr = (
        scratch[j * N_HEADS:(j + 1) * N_HEADS] for j in range(5))
    i = pl.program_id(1)
    nb = km_ref.shape[1]
    tq = MOBA_BLOCK

    blk_row = lax.broadcasted_iota(jnp.int32, (nb, tq), 0)
    past = blk_row < i
    sub = lax.broadcasted_iota(jnp.int32, (LANES, tq), 0)

    km = km_ref[0]
    km_hi = km.astype(BF16)
    r1 = km - km_hi.astype(F32)
    km_mid = r1.astype(BF16)
    km_lo = (r1 - km_mid.astype(F32)).astype(BF16)
    km3 = jnp.concatenate([km_hi, km_mid, km_lo], axis=0)

    def scores(blk, h):
        hp = h // HEADS_PER_LANE_TILE
        return _dot(k_ref[0, blk, :, hp * LANES:(hp + 1) * LANES], qz_scr[h][...])

    def v_rows(blk, h):
        return vt_ref[0, blk, h * V_ROWS:(h + 1) * V_ROWS, :]

    for hp in range(N_HEADS // HEADS_PER_LANE_TILE):
        lanes = slice(hp * LANES, (hp + 1) * LANES)
        qp = qt_ref[0, 0, lanes, :]
        km3p = km3[:, lanes]
        for hh in range(HEADS_PER_LANE_TILE):
            h = hp * HEADS_PER_LANE_TILE + hh
            in_head = (sub >= hh * HEAD_DIM) & (sub < (hh + 1) * HEAD_DIM)
            qz = jnp.where(in_head, qp, jnp.zeros_like(qp))
            qz_scr[h][...] = qz
            s_scr[h][...] = scores(0, h)

            g3 = _dot(km3p, qz)
            gate = g3[0:nb] + g3[nb:2 * nb] + g3[2 * nb:3 * nb]
            gate = jnp.where(past, gate, NEG_INF)
            rank = jnp.zeros((nb, tq), jnp.int32)
            for m in range(nb):
                gm = gate[m:m + 1, :]
                beats = (gm > gate) | ((gm == gate) & (m < blk_row))
                rank = rank + beats.astype(jnp.int32)
            sel = past & (rank < MOBA_TOPK)
            sel_scr[h][...] = sel.astype(F32)
            m_scr[h][...] = jnp.full((1, tq), NEG_INF, F32)
            acc_scr[h][...] = jnp.zeros((V_ROWS, tq), F32)

    def body(kb, carry):
        for h in range(N_HEADS):
            s_ahead = scores(kb + 1, h)
            s = s_scr[h][...]
            keep = sel_scr[h][pl.ds(kb, 1), :] > 0.5
            m_run = m_scr[h][...]
            m_new = jnp.where(keep, jnp.maximum(m_run, jnp.max(s, axis=0, keepdims=True)), m_run)
            m_sub = jnp.where(keep, m_new, BIG)
            alpha = jnp.exp2(m_run - m_new)
            pb = jnp.exp2(s - m_sub).astype(BF16)
            m_scr[h][...] = m_new
            acc_scr[h][...] = alpha * acc_scr[h][...] + _dot(v_rows(kb, h), pb)
            s_scr[h][...] = s_ahead
        return carry

    lax.fori_loop(0, i, body, 0)

    kpos = lax.broadcasted_iota(jnp.int32, (MOBA_BLOCK, tq), 0)
    qpos = lax.broadcasted_iota(jnp.int32, (MOBA_BLOCK, tq), 1)
    causal = kpos <= qpos
    t_idx = lax.broadcasted_iota(jnp.int32, (GMLP_CHUNK, GMLP_CHUNK), 0)
    s_idx = lax.broadcasted_iota(jnp.int32, (GMLP_CHUNK, GMLP_CHUNK), 1)
    tril = s_idx <= t_idx
    lane = lax.broadcasted_iota(jnp.int32, (GMLP_CHUNK, LANES), 1)
    first_group = lane < GROUP_DIM
    n_lane_tiles = D_GMLP // LANES
    n_chunks = tq // GMLP_CHUNK
    assert n_lane_tiles * n_chunks == N_HEADS
    heads_out = []
    cols = [[None] * n_lane_tiles for _ in range(n_chunks)]
    for h in range(N_HEADS):
        c, gp = divmod(h, n_lane_tiles)
        vp = vn_ref[0, c * GMLP_CHUNK:(c + 1) * GMLP_CHUNK, gp * LANES:(gp + 1) * LANES]
        w_a = jnp.where(tril, ws_ref[2 * gp], 0.0).astype(BF16)
        w_b = jnp.where(tril, ws_ref[2 * gp + 1], 0.0).astype(BF16)
        cols[c][gp] = jnp.where(first_group, _dot(w_a, vp), _dot(w_b, vp))

        st = jnp.where(causal, s_scr[h][...], NEG_INF)
        m_run = m_scr[h][...]
        m_new = jnp.maximum(m_run, jnp.max(st, axis=0, keepdims=True))
        alpha = jnp.exp2(m_run - m_new)
        p = jnp.exp2(st - m_new).astype(BF16)
        acc = alpha * acc_scr[h][...] + _dot(v_rows(i, h), p)
        heads_out.append(acc[0:HEAD_DIM] / acc[HEAD_DIM:HEAD_DIM + 1])

    mixed = jnp.concatenate(
        [jnp.concatenate(cols[c], axis=1) + bs_ref[...] for c in range(n_chunks)], axis=0)
    gm_out = ug_ref[0].astype(F32) * mixed
    gm_n = _rms_rows(gm_out, gg_ref[...]).astype(BF16)
    y = _dot(gm_n, wog_ref[...])

    attn_t = jnp.concatenate(heads_out, axis=0)
    ms_a = jnp.mean(attn_t * attn_t, axis=0, keepdims=True)
    attn_n = (attn_t * lax.rsqrt(ms_a + NORM_EPS) * ag_ref[...]).astype(BF16)
    y = y + _dot_tn(attn_n, woa_ref[...])
    o_ref[0] = x_ref[0] + _rms_rows(y, pg_ref[...])


def _mlp_kernel(h_ref, p_ref, g1_ref, wup_ref, wdn_ref, g2_ref, wgate_ref,
                bgate_ref, wple_ref, g3_ref, o_ref):
    h = h_ref[...]
    hn = _rms_rows(h, g1_ref[...]).astype(BF16)
    acc = jnp.zeros(h.shape, F32)
    for c in range(D_FF // FF_CHUNK):
        cs = slice(c * FF_CHUNK, (c + 1) * FF_CHUNK)
        up = jnp.maximum(_dot(hn, wup_ref[:, cs]), 0.0)
        acc = acc + _dot((up * up).astype(BF16), wdn_ref[cs, :])
    h = h + _rms_rows(acc, g2_ref[...])
    z = _dot(h.astype(BF16), wgate_ref[...]) + bgate_ref[...]
    gate = 1.0 / (1.0 + jnp.exp(-z))
    ple = _dot(p_ref[...].astype(BF16), wple_ref[...]) * gate
    o_ref[...] = h + _rms_rows(ple, g3_ref[...])


def _const_spec(shape):
    zeros = (0,) * len(shape)
    return pl.BlockSpec(shape, lambda *_: zeros, pipeline_mode=pl.Buffered(1))


def _rope_tables(seq):
    inv_freq = ROPE_THETA ** (-jnp.arange(ROPE_HALF, dtype=F32) / ROPE_HALF)
    ang = jnp.arange(seq, dtype=F32)[:, None] * inv_freq[None, :]
    cos, sin = jnp.cos(ang), jnp.sin(ang)
    cos_t, sin_t = cos.T, sin.T
    ones = jnp.ones((seq, HEAD_DIM - ROPE_DIM), F32)
    zeros = jnp.zeros((seq, HEAD_DIM - ROPE_DIM), F32)
    zh = jnp.zeros((seq, ROPE_HALF), F32)
    cos_h = jnp.concatenate([cos, cos, ones], axis=1)
    sa_h = jnp.concatenate([-sin, zh, zeros], axis=1)
    sb_h = jnp.concatenate([zh, sin, zeros], axis=1)
    tile = lambda t: jnp.concatenate([t] * HEADS_PER_LANE_TILE, axis=1)
    return cos_t, sin_t, tile(cos_h), tile(sa_h), tile(sb_h)


def _layer(h, p, mix_pre_g, w_in, gmlp_ln_g, gmlp_ln_b, w_s, b_s, attn_out_g,
           gmlp_out_g, w_o, mix_post_g, mlp_pre_g, w_up, w_down, mlp_post_g,
           w_ple, w_ple_gate, b_ple_gate, ple_post_g):
    B, S, D = h.shape
    assert D == D_MODEL and S % MOBA_BLOCK == 0
    nb = S // MOBA_BLOCK
    T = MOBA_BLOCK
    row = lambda v: v.reshape(1, -1).astype(F32)

    wq = w_in[:, :D_ATTN] * (HEAD_DIM ** -0.5 * math.log2(math.e))
    wk = w_in[:, D_ATTN:2 * D_ATTN]
    wv = w_in[:, 2 * D_ATTN:3 * D_ATTN]
    wft = jnp.concatenate([wq, wv], axis=1).T.astype(BF16)
    wtok = jnp.concatenate([wk, w_in[:, 3 * D_ATTN:]], axis=1).astype(BF16)
    cos_t, sin_t, cos_k, sa_k, sb_k = _rope_tables(S)

    cparams = functools.partial(pltpu.CompilerParams, vmem_limit_bytes=VMEM_LIMIT_BYTES)

    qt, k4, vt, km, ug, vn = pl.pallas_call(
        _inproj_kernel,
        grid=(B, nb),
        in_specs=[
            pl.BlockSpec((1, T, D), lambda b, i: (b, i, 0)),
            _const_spec((1, D)),
            _const_spec((2 * D_ATTN, D)),
            _const_spec((D, D_ATTN + 2 * D_GMLP)),
            pl.BlockSpec((ROPE_HALF, T), lambda b, i: (0, i)),
            pl.BlockSpec((ROPE_HALF, T), lambda b, i: (0, i)),
            pl.BlockSpec((T, LANES), lambda b, i: (i, 0)),
            pl.BlockSpec((T, LANES), lambda b, i: (i, 0)),
            pl.BlockSpec((T, LANES), lambda b, i: (i, 0)),
            _const_spec((1, D_GMLP)),
            _const_spec((1, D_GMLP)),
        ],
        out_specs=[
            pl.BlockSpec((1, 1, D_ATTN, T), lambda b, i: (b, i, 0, 0)),
            pl.BlockSpec((1, 1, T, D_ATTN), lambda b, i: (b, i, 0, 0)),
            pl.BlockSpec((1, 1, N_HEADS * V_ROWS, T), lambda b, i: (b, i, 0, 0)),
            pl.BlockSpec((1, 1, 1, D_ATTN), lambda b, i: (b, i, 0, 0)),
            pl.BlockSpec((1, T, D_GMLP), lambda b, i: (b, i, 0)),
            pl.BlockSpec((1, T, D_GMLP), lambda b, i: (b, i, 0)),
        ],
        out_shape=[
            jax.ShapeDtypeStruct((B, nb, D_ATTN, T), BF16),
            jax.ShapeDtypeStruct((B, nb, T, D_ATTN), BF16),
            jax.ShapeDtypeStruct((B, nb, N_HEADS * V_ROWS, T), BF16),
            jax.ShapeDtypeStruct((B, nb, 1, D_ATTN), F32),
            jax.ShapeDtypeStruct((B, S, D_GMLP), BF16),
            jax.ShapeDtypeStruct((B, S, D_GMLP), BF16),
        ],
        compiler_params=cparams(dimension_semantics=("parallel", "parallel")),
        name="inproj",
    )(h, row(mix_pre_g), wft, wtok, cos_t, sin_t, cos_k, sa_k, sb_k,
      row(gmlp_ln_g), row(gmlp_ln_b))

    km = km.reshape(B, nb, D_ATTN)
    bs_exp = jnp.repeat(b_s.T, GROUP_DIM, axis=1).astype(F32)
    woa = w_o[:D_ATTN].astype(BF16)
    wog = w_o[D_ATTN:].astype(BF16)

    h1 = pl.pallas_call(
        _mix_kernel,
        grid=(B, nb),
        in_specs=[
            pl.BlockSpec((1, 1, D_ATTN, T), lambda b, i: (b, i, 0, 0)),
            pl.BlockSpec((1, nb, T, D_ATTN), lambda b, i: (b, 0, 0, 0)),
            pl.BlockSpec((1, nb, N_HEADS * V_ROWS, T), lambda b, i: (b, 0, 0, 0)),
            pl.BlockSpec((1, nb, D_ATTN), lambda b, i: (b, 0, 0)),
            pl.BlockSpec((1, T, D_GMLP), lambda b, i: (b, i, 0)),
            pl.BlockSpec((1, T, D_GMLP), lambda b, i: (b, i, 0)),
            _const_spec((N_GROUPS, GMLP_CHUNK, GMLP_CHUNK)),
            _const_spec((GMLP_CHUNK, D_GMLP)),
            _const_spec((D_ATTN, 1)),
            _const_spec((1, D_GMLP)),
            _const_spec((D_ATTN, D)),
            _const_spec((D_GMLP, D)),
            _const_spec((1, D)),
            pl.BlockSpec((1, T, D), lambda b, i: (b, i, 0)),
        ],
        out_specs=pl.BlockSpec((1, T, D), lambda b, i: (b, i, 0)),
        out_shape=jax.ShapeDtypeStruct((B, S, D), F32),
        scratch_shapes=([pltpu.VMEM((nb, T), F32)] * N_HEADS
                        + [pltpu.VMEM((LANES, T), BF16)] * N_HEADS
                        + [pltpu.VMEM((1, T), F32)] * N_HEADS
                        + [pltpu.VMEM((V_ROWS, T), F32)] * N_HEADS
                        + [pltpu.VMEM((MOBA_BLOCK, T), F32)] * N_HEADS),
        compiler_params=cparams(dimension_semantics=("parallel", "arbitrary")),
        name="mix",
    )(qt, k4, vt, km, ug, vn, w_s.astype(F32), bs_exp,
      attn_out_g.reshape(-1, 1).astype(F32), row(gmlp_out_g), woa, wog,
      row(mix_post_g), h)

    n_tok = B * S
    tm = MLP_TOKENS
    out = pl.pallas_call(
        _mlp_kernel,
        grid=(n_tok // tm,),
        in_specs=[
            pl.BlockSpec((tm, D), lambda t: (t, 0)),
            pl.BlockSpec((tm, D_PLE), lambda t: (t, 0)),
            _const_spec((1, D)),
            _const_spec((D, D_FF)),
            _const_spec((D_FF, D)),
            _const_spec((1, D)),
            _const_spec((D, D)),
            _const_spec((1, D)),
            _const_spec((D_PLE, D)),
            _const_spec((1, D)),
        ],
        out_specs=pl.BlockSpec((tm, D), lambda t: (t, 0)),
        out_shape=jax.ShapeDtypeStruct((n_tok, D), F32),
        compiler_params=cparams(dimension_semantics=("parallel",)),
        name="mlp",
    )(h1.reshape(n_tok, D), p.reshape(n_tok, D_PLE), row(mlp_pre_g),
      w_up.astype(BF16), w_down.astype(BF16), row(mlp_post_g),
      w_ple_gate.astype(BF16), row(b_ple_gate), w_ple.astype(BF16), row(ple_post_g))
    return out.reshape(B, S, D)


def kernel(x, p, mix_pre_g, w_in, gmlp_ln_g, gmlp_ln_b, w_s, b_s, attn_out_g,
           gmlp_out_g, w_o, mix_post_g, mlp_pre_g, w_up, w_down, mlp_post_g,
           w_ple, w_ple_gate, b_ple_gate, ple_post_g):
    h = x
    for i in range(w_in.shape[0]):
        h = _layer(h, p[i], mix_pre_g[i], w_in[i], gmlp_ln_g[i], gmlp_ln_b[i],
                   w_s[i], b_s[i], attn_out_g[i], gmlp_out_g[i], w_o[i],
                   mix_post_g[i], mlp_pre_g[i], w_up[i], w_down[i], mlp_post_g[i],
                   w_ple[i], w_ple_gate[i], b_ple_gate[i], ple_post_g[i])
    return h
```

```python
import functools
import math

import jax
import jax.numpy as jnp
import numpy as np
from jax import lax
from jax.experimental import pallas as pl
from jax.experimental.pallas import tpu as pltpu

D_MODEL = 1024
D_PLE = 256
N_HEADS = 8
HEAD_DIM = 64
D_ATTN = N_HEADS * HEAD_DIM
N_GROUPS = 8
GROUP_DIM = 64
D_GMLP = N_GROUPS * GROUP_DIM
D_FF = 4 * D_MODEL
MOBA_BLOCK = 256
MOBA_TOPK = 3
GMLP_CHUNK = 128
ROPE_THETA = 500000.0
ROPE_DIM = HEAD_DIM // 4
ROPE_HALF = ROPE_DIM // 2
NORM_EPS = 1e-6
NEG_INF = -1e30

LANES = 128
HEADS_PER_LANE_TILE = LANES // HEAD_DIM
BF16_SUBLANES = 16
V_ROWS = HEAD_DIM + BF16_SUBLANES
BIG = 1e30
VMEM_LIMIT_BYTES = 48 * 1024 * 1024
MLP_TOKENS = 512
FF_CHUNK = 1024

F32 = jnp.float32
BF16 = jnp.bfloat16


def _dot(a, b):
    return jnp.dot(a, b, preferred_element_type=F32)


def _dot_nt(a, b):
    return lax.dot_general(a, b, (((1,), (1,)), ((), ())), preferred_element_type=F32)


def _dot_tn(a, b):
    return lax.dot_general(a, b, (((0,), (0,)), ((), ())), preferred_element_type=F32)


def _rms_rows(x, g):
    ms = jnp.mean(x * x, axis=-1, keepdims=True)
    return x * lax.rsqrt(ms + NORM_EPS) * g


def _gelu_exact(x):
    return 0.5 * x * (1.0 + lax.erf(x * math.sqrt(0.5)))


def _inproj_kernel(x_ref, g_ref, wft_ref, wtok_ref, cost_ref, sint_ref,
                   cosk_ref, sak_ref, sbk_ref, lng_ref, lnb_ref,
                   qt_ref, k_ref, vt_ref, km_ref, ug_ref, vn_ref):
    x = x_ref[0]
    hn = _rms_rows(x, g_ref[...]).astype(BF16)

    gv = _dot(hn, wtok_ref[:, D_ATTN + D_GMLP:])
    g2 = _gelu_exact(gv)
    mu = jnp.mean(g2, axis=-1, keepdims=True)
    xc = g2 - mu
    var = jnp.mean(xc * xc, axis=-1, keepdims=True)
    vn = xc * lax.rsqrt(var + NORM_EPS) * lng_ref[...] + lnb_ref[...]
    vn_ref[0] = vn.astype(BF16)

    u = _dot(hn, wtok_ref[:, D_ATTN:D_ATTN + D_GMLP])
    ug_ref[0] = _gelu_exact(u).astype(BF16)

    tok = _dot(hn, wtok_ref[:, :D_ATTN])
    cos_k = cosk_ref[...]
    sa_k = sak_ref[...]
    sb_k = sbk_ref[...]
    kparts = []
    for j in range(D_ATTN // LANES):
        kb = tok[:, j * LANES:(j + 1) * LANES]
        up = pltpu.roll(kb, LANES - ROPE_HALF, axis=1)
        dn = pltpu.roll(kb, ROPE_HALF, axis=1)
        kparts.append(kb * cos_k + up * sa_k + dn * sb_k)
    k = jnp.concatenate(kparts, axis=1)
    k_ref[0, 0] = k.astype(BF16)
    km_ref[0, 0] = jnp.mean(k, axis=0, keepdims=True)

    ft = _dot_nt(wft_ref[...], hn)
    cos_t = cost_ref[...]
    sin_t = sint_ref[...]
    pieces = []
    for h in range(N_HEADS):
        r0 = h * HEAD_DIM
        x1 = ft[r0:r0 + ROPE_HALF]
        x2 = ft[r0 + ROPE_HALF:r0 + ROPE_DIM]
        pieces.append(x1 * cos_t - x2 * sin_t)
        pieces.append(x2 * cos_t + x1 * sin_t)
        pieces.append(ft[r0 + ROPE_DIM:r0 + HEAD_DIM])
    qt_ref[0, 0] = jnp.concatenate(pieces, axis=0).astype(BF16)
    ones = jnp.ones((V_ROWS - HEAD_DIM, ft.shape[1]), F32)
    vparts = []
    for h in range(N_HEADS):
        vparts.append(ft[D_ATTN + h * HEAD_DIM:D_ATTN + (h + 1) * HEAD_DIM])
        vparts.append(ones)
    vt_ref[0, 0] = jnp.concatenate(vparts, axis=0).astype(BF16)


def _mix_kernel(qt_ref, k_ref, vt_ref, km_ref, ug_ref, vn_ref, ws_ref, bs_ref,
                ag_ref, gg_ref, woa_ref, wog_ref, pg_ref, x_ref, o_ref,
                *scratch):
    sel_scr, qz_scr, m_scr, acc_scr, s_scr = (
        scratch[j * N_HEADS:(j + 1) * N_HEADS] for j in range(5))
    i = pl.program_id(1)
    nb = km_ref.shape[1]
    tq = MOBA_BLOCK

    blk_row = lax.broadcasted_iota(jnp.int32, (nb, tq), 0)
    past = blk_row < i
    sub = lax.broadcasted_iota(jnp.int32, (LANES, tq), 0)

    km = km_ref[0]
    km_hi = km.astype(BF16)
    r1 = km - km_hi.astype(F32)
    km_mid = r1.astype(BF16)
    km_lo = (r1 - km_mid.astype(F32)).astype(BF16)
    km3 = jnp.concatenate([km_hi, km_mid, km_lo], axis=0)

    def scores(blk, h):
        hp = h // HEADS_PER_LANE_TILE
        return _dot(k_ref[0, blk, :, hp * LANES:(hp + 1) * LANES], qz_scr[h][...])

    def v_rows(blk, h):
        return vt_ref[0, blk, h * V_ROWS:(h + 1) * V_ROWS, :]

    for hp in range(N_HEADS // HEADS_PER_LANE_TILE):
        lanes = slice(hp * LANES, (hp + 1) * LANES)
        qp = qt_ref[0, 0, lanes, :]
        km3p = km3[:, lanes]
        for hh in range(HEADS_PER_LANE_TILE):
            h = hp * HEADS_PER_LANE_TILE + hh
            in_head = (sub >= hh * HEAD_DIM) & (sub < (hh + 1) * HEAD_DIM)
            qz = jnp.where(in_head, qp, jnp.zeros_like(qp))
            qz_scr[h][...] = qz
            s_scr[h][...] = scores(0, h)

            g3 = _dot(km3p, qz)
            gate = g3[0:nb] + g3[nb:2 * nb] + g3[2 * nb:3 * nb]
            gate = jnp.where(past, gate, NEG_INF)
            rank = jnp.zeros((nb, tq), jnp.int32)
            for m in range(nb):
                gm = gate[m:m + 1, :]
                beats = (gm > gate) | ((gm == gate) & (m < blk_row))
                rank = rank + beats.astype(jnp.int32)
            sel = past & (rank < MOBA_TOPK)
            sel_scr[h][...] = sel.astype(F32)
            m_scr[h][...] = jnp.full((1, tq), NEG_INF, F32)
            acc_scr[h][...] = jnp.zeros((V_ROWS, tq), F32)

    def one_block(kb):
        for h in range(N_HEADS):
            s_ahead = scores(kb + 1, h)
            s = s_scr[h][...]
            keep = sel_scr[h][pl.ds(kb, 1), :] > 0.5
            m_run = m_scr[h][...]
            m_new = jnp.where(keep, jnp.maximum(m_run, jnp.max(s, axis=0, keepdims=True)), m_run)
            m_sub = jnp.where(keep, m_new, BIG)
            alpha = jnp.exp2(m_run - m_new)
            pb = jnp.exp2(s - m_sub).astype(BF16)
            m_scr[h][...] = m_new
            acc_scr[h][...] = alpha * acc_scr[h][...] + _dot(v_rows(kb, h), pb)
            s_scr[h][...] = s_ahead

    def two_blocks(t, carry):
        one_block(2 * t)
        one_block(2 * t + 1)
        return carry

    lax.fori_loop(0, i // 2, two_blocks, 0)

    @pl.when(i % 2 == 1)
    def _():
        one_block(i - 1)

    kpos = lax.broadcasted_iota(jnp.int32, (MOBA_BLOCK, tq), 0)
    qpos = lax.broadcasted_iota(jnp.int32, (MOBA_BLOCK, tq), 1)
    causal = kpos <= qpos
    t_idx = lax.broadcasted_iota(jnp.int32, (GMLP_CHUNK, GMLP_CHUNK), 0)
    s_idx = lax.broadcasted_iota(jnp.int32, (GMLP_CHUNK, GMLP_CHUNK), 1)
    tril = s_idx <= t_idx
    lane = lax.broadcasted_iota(jnp.int32, (GMLP_CHUNK, LANES), 1)
    first_group = lane < GROUP_DIM
    n_lane_tiles = D_GMLP // LANES
    n_chunks = tq // GMLP_CHUNK
    assert n_lane_tiles * n_chunks == N_HEADS
    heads_out = []
    cols = [[None] * n_lane_tiles for _ in range(n_chunks)]
    for h in range(N_HEADS):
        c, gp = divmod(h, n_lane_tiles)
        vp = vn_ref[0, c * GMLP_CHUNK:(c + 1) * GMLP_CHUNK, gp * LANES:(gp + 1) * LANES]
        w_a = jnp.where(tril, ws_ref[2 * gp], 0.0).astype(BF16)
        w_b = jnp.where(tril, ws_ref[2 * gp + 1], 0.0).astype(BF16)
        cols[c][gp] = jnp.where(first_group, _dot(w_a, vp), _dot(w_b, vp))

        st = jnp.where(causal, s_scr[h][...], NEG_INF)
        m_run = m_scr[h][...]
        m_new = jnp.maximum(m_run, jnp.max(st, axis=0, keepdims=True))
        alpha = jnp.exp2(m_run - m_new)
        p = jnp.exp2(st - m_new).astype(BF16)
        acc = alpha * acc_scr[h][...] + _dot(v_rows(i, h), p)
        heads_out.append(acc[0:HEAD_DIM] / acc[HEAD_DIM:HEAD_DIM + 1])

    mixed = jnp.concatenate(
        [jnp.concatenate(cols[c], axis=1) + bs_ref[...] for c in range(n_chunks)], axis=0)
    gm_out = ug_ref[0].astype(F32) * mixed
    gm_n = _rms_rows(gm_out, gg_ref[...]).astype(BF16)
    y = _dot(gm_n, wog_ref[...])

    attn_t = jnp.concatenate(heads_out, axis=0)
    ms_a = jnp.mean(attn_t * attn_t, axis=0, keepdims=True)
    attn_n = (attn_t * lax.rsqrt(ms_a + NORM_EPS) * ag_ref[...]).astype(BF16)
    y = y + _dot_tn(attn_n, woa_ref[...])
    o_ref[0] = x_ref[0] + _rms_rows(y, pg_ref[...])


def _mlp_kernel(h_ref, p_ref, g1_ref, wup_ref, wdn_ref, g2_ref, wgate_ref,
                bgate_ref, wple_ref, g3_ref, o_ref):
    h = h_ref[...]
    hn = _rms_rows(h, g1_ref[...]).astype(BF16)
    acc = jnp.zeros(h.shape, F32)
    for c in range(D_FF // FF_CHUNK):
        cs = slice(c * FF_CHUNK, (c + 1) * FF_CHUNK)
        up = jnp.maximum(_dot(hn, wup_ref[:, cs]), 0.0)
        acc = acc + _dot((up * up).astype(BF16), wdn_ref[cs, :])
    h = h + _rms_rows(acc, g2_ref[...])
    z = _dot(h.astype(BF16), wgate_ref[...]) + bgate_ref[...]
    gate = 1.0 / (1.0 + jnp.exp(-z))
    ple = _dot(p_ref[...].astype(BF16), wple_ref[...]) * gate
    o_ref[...] = h + _rms_rows(ple, g3_ref[...])


def _const_spec(shape):
    zeros = (0,) * len(shape)
    return pl.BlockSpec(shape, lambda *_: zeros, pipeline_mode=pl.Buffered(1))


def _rope_tables(seq):
    inv_freq = ROPE_THETA ** (-np.arange(ROPE_HALF, dtype=np.float64) / ROPE_HALF)
    ang = np.arange(seq, dtype=np.float64)[:, None] * inv_freq[None, :]
    cos, sin = np.cos(ang), np.sin(ang)
    cos_t, sin_t = cos.T, sin.T
    ones = np.ones((seq, HEAD_DIM - ROPE_DIM))
    zeros = np.zeros((seq, HEAD_DIM - ROPE_DIM))
    zh = np.zeros((seq, ROPE_HALF))
    cos_h = np.concatenate([cos, cos, ones], axis=1)
    sa_h = np.concatenate([-sin, zh, zeros], axis=1)
    sb_h = np.concatenate([zh, sin, zeros], axis=1)
    tile = lambda t: np.concatenate([t] * HEADS_PER_LANE_TILE, axis=1)
    tables = (cos_t, sin_t, tile(cos_h), tile(sa_h), tile(sb_h))
    return tuple(jnp.asarray(np.ascontiguousarray(t), dtype=F32) for t in tables)


def _layer(h, p, mix_pre_g, w_in, gmlp_ln_g, gmlp_ln_b, w_s, b_s, attn_out_g,
           gmlp_out_g, w_o, mix_post_g, mlp_pre_g, w_up, w_down, mlp_post_g,
           w_ple, w_ple_gate, b_ple_gate, ple_post_g):
    B, S, D = h.shape
    assert D == D_MODEL and S % MOBA_BLOCK == 0
    nb = S // MOBA_BLOCK
    T = MOBA_BLOCK
    row = lambda v: v.reshape(1, -1).astype(F32)

    wq = w_in[:, :D_ATTN] * (HEAD_DIM ** -0.5 * math.log2(math.e))
    wk = w_in[:, D_ATTN:2 * D_ATTN]
    wv = w_in[:, 2 * D_ATTN:3 * D_ATTN]
    wft = jnp.concatenate([wq, wv], axis=1).T.astype(BF16)
    wtok = jnp.concatenate([wk, w_in[:, 3 * D_ATTN:]], axis=1).astype(BF16)
    cos_t, sin_t, cos_k, sa_k, sb_k = _rope_tables(S)

    cparams = functools.partial(pltpu.CompilerParams, vmem_limit_bytes=VMEM_LIMIT_BYTES)

    qt, k4, vt, km, ug, vn = pl.pallas_call(
        _inproj_kernel,
        grid=(B, nb),
        in_specs=[
            pl.BlockSpec((1, T, D), lambda b, i: (b, i, 0)),
            _const_spec((1, D)),
            _const_spec((2 * D_ATTN, D)),
            _const_spec((D, D_ATTN + 2 * D_GMLP)),
            pl.BlockSpec((ROPE_HALF, T), lambda b, i: (0, i)),
            pl.BlockSpec((ROPE_HALF, T), lambda b, i: (0, i)),
            pl.BlockSpec((T, LANES), lambda b, i: (i, 0)),
            pl.BlockSpec((T, LANES), lambda b, i: (i, 0)),
            pl.BlockSpec((T, LANES), lambda b, i: (i, 0)),
            _const_spec((1, D_GMLP)),
            _const_spec((1, D_GMLP)),
        ],
        out_specs=[
            pl.BlockSpec((1, 1, D_ATTN, T), lambda b, i: (b, i, 0, 0)),
            pl.BlockSpec((1, 1, T, D_ATTN), lambda b, i: (b, i, 0, 0)),
            pl.BlockSpec((1, 1, N_HEADS * V_ROWS, T), lambda b, i: (b, i, 0, 0)),
            pl.BlockSpec((1, 1, 1, D_ATTN), lambda b, i: (b, i, 0, 0)),
            pl.BlockSpec((1, T, D_GMLP), lambda b, i: (b, i, 0)),
            pl.BlockSpec((1, T, D_GMLP), lambda b, i: (b, i, 0)),
        ],
        out_shape=[
            jax.ShapeDtypeStruct((B, nb, D_ATTN, T), BF16),
            jax.ShapeDtypeStruct((B, nb, T, D_ATTN), BF16),
            jax.ShapeDtypeStruct((B, nb, N_HEADS * V_ROWS, T), BF16),
            jax.ShapeDtypeStruct((B, nb, 1, D_ATTN), F32),
            jax.ShapeDtypeStruct((B, S, D_GMLP), BF16),
            jax.ShapeDtypeStruct((B, S, D_GMLP), BF16),
        ],
        compiler_params=cparams(dimension_semantics=("parallel", "parallel")),
        name="inproj",
    )(h, row(mix_pre_g), wft, wtok, cos_t, sin_t, cos_k, sa_k, sb_k,
      row(gmlp_ln_g), row(gmlp_ln_b))

    km = km.reshape(B, nb, D_ATTN)
    bs_exp = jnp.repeat(b_s.T, GROUP_DIM, axis=1).astype(F32)
    woa = w_o[:D_ATTN].astype(BF16)
    wog = w_o[D_ATTN:].astype(BF16)

    h1 = pl.pallas_call(
        _mix_kernel,
        grid=(B, nb),
        in_specs=[
            pl.BlockSpec((1, 1, D_ATTN, T), lambda b, i: (b, i, 0, 0)),
            pl.BlockSpec((1, nb, T, D_ATTN), lambda b, i: (b, 0, 0, 0)),
            pl.BlockSpec((1, nb, N_HEADS * V_ROWS, T), lambda b, i: (b, 0, 0, 0)),
            pl.BlockSpec((1, nb, D_ATTN), lambda b, i: (b, 0, 0)),
            pl.BlockSpec((1, T, D_GMLP), lambda b, i: (b, i, 0)),
            pl.BlockSpec((1, T, D_GMLP), lambda b, i: (b, i, 0)),
            _const_spec((N_GROUPS, GMLP_CHUNK, GMLP_CHUNK)),
            _const_spec((GMLP_CHUNK, D_GMLP)),
            _const_spec((D_ATTN, 1)),
            _const_spec((1, D_GMLP)),
            _const_spec((D_ATTN, D)),
            _const_spec((D_GMLP, D)),
            _const_spec((1, D)),
            pl.BlockSpec((1, T, D), lambda b, i: (b, i, 0)),
        ],
        out_specs=pl.BlockSpec((1, T, D), lambda b, i: (b, i, 0)),
        out_shape=jax.ShapeDtypeStruct((B, S, D), F32),
        scratch_shapes=([pltpu.VMEM((nb, T), F32)] * N_HEADS
                        + [pltpu.VMEM((LANES, T), BF16)] * N_HEADS
                        + [pltpu.VMEM((1, T), F32)] * N_HEADS
                        + [pltpu.VMEM((V_ROWS, T), F32)] * N_HEADS
                        + [pltpu.VMEM((MOBA_BLOCK, T), F32)] * N_HEADS),
        compiler_params=cparams(dimension_semantics=("parallel", "arbitrary")),
        name="mix",
    )(qt, k4, vt, km, ug, vn, w_s.astype(F32), bs_exp,
      attn_out_g.reshape(-1, 1).astype(F32), row(gmlp_out_g), woa, wog,
      row(mix_post_g), h)

    n_tok = B * S
    tm = MLP_TOKENS
    out = pl.pallas_call(
        _mlp_kernel,
        grid=(n_tok // tm,),
        in_specs=[
            pl.BlockSpec((tm, D), lambda t: (t, 0)),
            pl.BlockSpec((tm, D_PLE), lambda t: (t, 0)),
            _const_spec((1, D)),
            _const_spec((D, D_FF)),
            _const_spec((D_FF, D)),
            _const_spec((1, D)),
            _const_spec((D, D)),
            _const_spec((1, D)),
            _const_spec((D_PLE, D)),
            _const_spec((1, D)),
        ],
        out_specs=pl.BlockSpec((tm, D), lambda t: (t, 0)),
        out_shape=jax.ShapeDtypeStruct((n_tok, D), F32),
        compiler_params=cparams(dimension_semantics=("parallel",)),
        name="mlp",
    )(h1.reshape(n_tok, D), p.reshape(n_tok, D_PLE), row(mlp_pre_g),
      w_up.astype(BF16), w_down.astype(BF16), row(mlp_post_g),
      w_ple_gate.astype(BF16), row(b_ple_gate), w_ple.astype(BF16), row(ple_post_g))
    return out.reshape(B, S, D)


def kernel(x, p, mix_pre_g, w_in, gmlp_ln_g, gmlp_ln_b, w_s, b_s, attn_out_g,
           gmlp_out_g, w_o, mix_post_g, mlp_pre_g, w_up, w_down, mlp_post_g,
           w_ple, w_ple_gate, b_ple_gate, ple_post_g):
    h = x
    for i in range(w_in.shape[0]):
        h = _layer(h, p[i], mix_pre_g[i], w_in[i], gmlp_ln_g[i], gmlp_ln_b[i],
                   w_s[i], b_s[i], attn_out_g[i], gmlp_out_g[i], w_o[i],
                   mix_post_g[i], mlp_pre_g[i], w_up[i], w_down[i], mlp_post_g[i],
                   w_ple[i], w_ple_gate[i], b_ple_gate[i], ple_post_g[i])
    return h
```

```python
import functools
import math

import jax
import jax.numpy as jnp
import numpy as np
from jax import lax
from jax.experimental import pallas as pl
from jax.experimental.pallas import tpu as pltpu

D_MODEL = 1024
D_PLE = 256
N_HEADS = 8
HEAD_DIM = 64
D_ATTN = N_HEADS * HEAD_DIM
N_GROUPS = 8
GROUP_DIM = 64
D_GMLP = N_GROUPS * GROUP_DIM
D_FF = 4 * D_MODEL
MOBA_BLOCK = 256
MOBA_TOPK = 3
GMLP_CHUNK = 128
ROPE_THETA = 500000.0
ROPE_DIM = HEAD_DIM // 4
ROPE_HALF = ROPE_DIM // 2
NORM_EPS = 1e-6
NEG_INF = -1e30

LANES = 128
HEADS_PER_LANE_TILE = LANES // HEAD_DIM
BF16_SUBLANES = 16
V_ROWS = HEAD_DIM + BF16_SUBLANES
BIG = 1e30
SCORE_LEAD = 2
VMEM_LIMIT_BYTES = 48 * 1024 * 1024
MLP_TOKENS = 512
FF_CHUNK = 1024
WEIGHT_STAGE_BYTES = 2 * 1024 * 1024

F32 = jnp.float32
BF16 = jnp.bfloat16


def _dot(a, b):
    return jnp.dot(a, b, preferred_element_type=F32)


def _dot_nt(a, b):
    return lax.dot_general(a, b, (((1,), (1,)), ((), ())), preferred_element_type=F32)


def _dot_tn(a, b):
    return lax.dot_general(a, b, (((0,), (0,)), ((), ())), preferred_element_type=F32)


def _rms_rows(x, g):
    ms = jnp.mean(x * x, axis=-1, keepdims=True)
    return x * lax.rsqrt(ms + NORM_EPS) * g


def _gelu_exact(x):
    return 0.5 * x * (1.0 + lax.erf(x * math.sqrt(0.5)))


def _inproj_kernel(x_ref, g_ref, wft_ref, wtok_ref, cost_ref, sint_ref,
                   cosk_ref, sak_ref, sbk_ref, lng_ref, lnb_ref,
                   qt_ref, k_ref, vt_ref, km_ref, ug_ref, vn_ref):
    x = x_ref[0]
    hn = _rms_rows(x, g_ref[...]).astype(BF16)

    gv = _dot(hn, wtok_ref[:, D_ATTN + D_GMLP:])
    g2 = _gelu_exact(gv)
    mu = jnp.mean(g2, axis=-1, keepdims=True)
    xc = g2 - mu
    var = jnp.mean(xc * xc, axis=-1, keepdims=True)
    vn = xc * lax.rsqrt(var + NORM_EPS) * lng_ref[...] + lnb_ref[...]
    vn_ref[0] = vn.astype(BF16)

    u = _dot(hn, wtok_ref[:, D_ATTN:D_ATTN + D_GMLP])
    ug_ref[0] = _gelu_exact(u).astype(BF16)

    tok = _dot(hn, wtok_ref[:, :D_ATTN])
    cos_k = cosk_ref[...]
    sa_k = sak_ref[...]
    sb_k = sbk_ref[...]
    kparts = []
    for j in range(D_ATTN // LANES):
        kb = tok[:, j * LANES:(j + 1) * LANES]
        up = pltpu.roll(kb, LANES - ROPE_HALF, axis=1)
        dn = pltpu.roll(kb, ROPE_HALF, axis=1)
        kparts.append(kb * cos_k + up * sa_k + dn * sb_k)
    k = jnp.concatenate(kparts, axis=1)
    k_ref[0, 0] = k.astype(BF16)
    km_ref[0, 0] = jnp.mean(k, axis=0, keepdims=True)

    ft = _dot_nt(wft_ref[...], hn)
    cos_t = cost_ref[...]
    sin_t = sint_ref[...]
    pieces = []
    for h in range(N_HEADS):
        r0 = h * HEAD_DIM
        x1 = ft[r0:r0 + ROPE_HALF]
        x2 = ft[r0 + ROPE_HALF:r0 + ROPE_DIM]
        pieces.append(x1 * cos_t - x2 * sin_t)
        pieces.append(x2 * cos_t + x1 * sin_t)
        pieces.append(ft[r0 + ROPE_DIM:r0 + HEAD_DIM])
    qt_ref[0, 0] = jnp.concatenate(pieces, axis=0).astype(BF16)
    ones = jnp.ones((V_ROWS - HEAD_DIM, ft.shape[1]), F32)
    vparts = []
    for h in range(N_HEADS):
        vparts.append(ft[D_ATTN + h * HEAD_DIM:D_ATTN + (h + 1) * HEAD_DIM])
        vparts.append(ones)
    vt_ref[0, 0] = jnp.concatenate(vparts, axis=0).astype(BF16)


def _mix_kernel(qt_ref, k_ref, vt_ref, km_ref, ug_ref, vn_ref, ws_ref, bs_ref,
                ag_ref, gg_ref, woa_ref, wog_ref, pg_ref, x_ref, o_ref,
                *scratch):
    sel_scr, qz_scr, m_scr, acc_scr, s_scr = (
        scratch[j * N_HEADS:(j + 1) * N_HEADS] for j in range(5))
    i = pl.program_id(1)
    nb = km_ref.shape[1]
    tq = MOBA_BLOCK

    blk_row = lax.broadcasted_iota(jnp.int32, (nb, tq), 0)
    past = blk_row < i
    sub = lax.broadcasted_iota(jnp.int32, (LANES, tq), 0)

    km = km_ref[0]
    km_hi = km.astype(BF16)
    r1 = km - km_hi.astype(F32)
    km_mid = r1.astype(BF16)
    km_lo = (r1 - km_mid.astype(F32)).astype(BF16)
    km3 = jnp.concatenate([km_hi, km_mid, km_lo], axis=0)

    def scores(blk, h):
        hp = h // HEADS_PER_LANE_TILE
        return _dot(k_ref[0, blk, :, hp * LANES:(hp + 1) * LANES], qz_scr[h][...])

    def v_rows(blk, h):
        return vt_ref[0, blk, h * V_ROWS:(h + 1) * V_ROWS, :]

    for hp in range(N_HEADS // HEADS_PER_LANE_TILE):
        lanes = slice(hp * LANES, (hp + 1) * LANES)
        qp = qt_ref[0, 0, lanes, :]
        km3p = km3[:, lanes]
        for hh in range(HEADS_PER_LANE_TILE):
            h = hp * HEADS_PER_LANE_TILE + hh
            in_head = (sub >= hh * HEAD_DIM) & (sub < (hh + 1) * HEAD_DIM)
            qz = jnp.where(in_head, qp, jnp.zeros_like(qp))
            qz_scr[h][...] = qz
            s_scr[h][...] = scores(0, h)

            g3 = _dot(km3p, qz)
            gate = g3[0:nb] + g3[nb:2 * nb] + g3[2 * nb:3 * nb]
            gate = jnp.where(past, gate, NEG_INF)
            rank = jnp.zeros((nb, tq), jnp.int32)
            for m in range(nb):
                gm = gate[m:m + 1, :]
                beats = (gm > gate) | ((gm == gate) & (m < blk_row))
                rank = rank + beats.astype(jnp.int32)
            sel = past & (rank < MOBA_TOPK)
            sel_scr[h][...] = sel.astype(F32)
            m_scr[h][...] = jnp.full((1, tq), NEG_INF, F32)
            acc_scr[h][...] = jnp.zeros((V_ROWS, tq), F32)

    def past_blocks(kb0, n_blocks):
        units = [(kb0 + j, h) for j in range(n_blocks) for h in range(N_HEADS)]
        ahead = {}
        for u in range(-SCORE_LEAD, len(units)):
            if 0 <= u + SCORE_LEAD < len(units):
                blk, head = units[u + SCORE_LEAD]
                ahead[u + SCORE_LEAD] = scores(blk + 1, head)
            if u < 0:
                continue
            kb, h = units[u]
            s = s_scr[h][...]
            keep = sel_scr[h][pl.ds(kb, 1), :] > 0.5
            m_run = m_scr[h][...]
            m_new = jnp.where(keep, jnp.maximum(m_run, jnp.max(s, axis=0, keepdims=True)), m_run)
            m_sub = jnp.where(keep, m_new, BIG)
            alpha = jnp.exp2(m_run - m_new)
            pb = jnp.exp2(s - m_sub).astype(BF16)
            m_scr[h][...] = m_new
            acc_scr[h][...] = alpha * acc_scr[h][...] + _dot(v_rows(kb, h), pb)
            s_scr[h][...] = ahead.pop(u)

    def two_blocks(t, carry):
        past_blocks(2 * t, 2)
        return carry

    lax.fori_loop(0, i // 2, two_blocks, 0)

    @pl.when(i % 2 == 1)
    def _():
        past_blocks(i - 1, 1)

    kpos = lax.broadcasted_iota(jnp.int32, (MOBA_BLOCK, tq), 0)
    qpos = lax.broadcasted_iota(jnp.int32, (MOBA_BLOCK, tq), 1)
    causal = kpos <= qpos
    t_idx = lax.broadcasted_iota(jnp.int32, (GMLP_CHUNK, GMLP_CHUNK), 0)
    s_idx = lax.broadcasted_iota(jnp.int32, (GMLP_CHUNK, GMLP_CHUNK), 1)
    tril = s_idx <= t_idx
    lane = lax.broadcasted_iota(jnp.int32, (GMLP_CHUNK, LANES), 1)
    first_group = lane < GROUP_DIM
    n_lane_tiles = D_GMLP // LANES
    n_chunks = tq // GMLP_CHUNK
    assert n_lane_tiles * n_chunks == N_HEADS
    heads_out = []
    cols = [[None] * n_lane_tiles for _ in range(n_chunks)]
    for h in range(N_HEADS):
        c, gp = divmod(h, n_lane_tiles)
        vp = vn_ref[0, c * GMLP_CHUNK:(c + 1) * GMLP_CHUNK, gp * LANES:(gp + 1) * LANES]
        w_a = jnp.where(tril, ws_ref[2 * gp], 0.0).astype(BF16)
        w_b = jnp.where(tril, ws_ref[2 * gp + 1], 0.0).astype(BF16)
        cols[c][gp] = jnp.where(first_group, _dot(w_a, vp), _dot(w_b, vp))

        st = jnp.where(causal, s_scr[h][...], NEG_INF)
        m_run = m_scr[h][...]
        m_new = jnp.maximum(m_run, jnp.max(st, axis=0, keepdims=True))
        alpha = jnp.exp2(m_run - m_new)
        p = jnp.exp2(st - m_new).astype(BF16)
        acc = alpha * acc_scr[h][...] + _dot(v_rows(i, h), p)
        heads_out.append(acc[0:HEAD_DIM] / acc[HEAD_DIM:HEAD_DIM + 1])

    mixed = jnp.concatenate(
        [jnp.concatenate(cols[c], axis=1) + bs_ref[...] for c in range(n_chunks)], axis=0)
    gm_out = ug_ref[0].astype(F32) * mixed
    gm_n = _rms_rows(gm_out, gg_ref[...]).astype(BF16)
    y = _dot(gm_n, wog_ref[...])

    attn_t = jnp.concatenate(heads_out, axis=0)
    ms_a = jnp.mean(attn_t * attn_t, axis=0, keepdims=True)
    attn_n = (attn_t * lax.rsqrt(ms_a + NORM_EPS) * ag_ref[...]).astype(BF16)
    y = y + _dot_tn(attn_n, woa_ref[...])
    o_ref[0] = x_ref[0] + _rms_rows(y, pg_ref[...])


def _load_weight_bf16(src_hbm, dst_ref, stage_ref, sem_ref):
    chunk = stage_ref.shape[1]
    n_rows = src_hbm.shape[0]
    chunk = min(chunk, n_rows)
    n_chunks = n_rows // chunk
    assert n_chunks * chunk == n_rows

    def copy(c, slot):
        return pltpu.make_async_copy(src_hbm.at[pl.ds(c * chunk, chunk), :],
                                     stage_ref.at[slot, pl.ds(0, chunk), :],
                                     sem_ref.at[slot])

    copy(0, 0).start()

    def body(c, carry):
        slot = lax.rem(c, 2)

        @pl.when(c + 1 < n_chunks)
        def _():
            copy(c + 1, 1 - slot).start()

        copy(c, slot).wait()
        rows = pl.ds(pl.multiple_of(c * chunk, chunk), chunk)
        dst_ref[rows, :] = stage_ref[slot, pl.ds(0, chunk), :].astype(BF16)
        return carry

    lax.fori_loop(0, n_chunks, body, 0)


def _mlp_kernel(h_ref, p_ref, g1_ref, wup_hbm, wdn_hbm, g2_ref, wgate_hbm,
                bgate_ref, wple_hbm, g3_ref, o_ref,
                wup_ref, wdn_ref, wgate_ref, wple_ref, wide_stage, narrow_stage, sem):
    @pl.when(pl.program_id(0) == 0)
    def _():
        _load_weight_bf16(wup_hbm, wup_ref, wide_stage, sem)
        _load_weight_bf16(wdn_hbm, wdn_ref, narrow_stage, sem)
        _load_weight_bf16(wgate_hbm, wgate_ref, narrow_stage, sem)
        _load_weight_bf16(wple_hbm, wple_ref, narrow_stage, sem)

    h = h_ref[...]
    hn = _rms_rows(h, g1_ref[...]).astype(BF16)
    acc = jnp.zeros(h.shape, F32)
    for c in range(D_FF // FF_CHUNK):
        cs = slice(c * FF_CHUNK, (c + 1) * FF_CHUNK)
        up = jnp.maximum(_dot(hn, wup_ref[:, cs]), 0.0)
        acc = acc + _dot((up * up).astype(BF16), wdn_ref[cs, :])
    h = h + _rms_rows(acc, g2_ref[...])
    z = _dot(h.astype(BF16), wgate_ref[...]) + bgate_ref[...]
    gate = 1.0 / (1.0 + jnp.exp(-z))
    ple = _dot(p_ref[...].astype(BF16), wple_ref[...]) * gate
    o_ref[...] = h + _rms_rows(ple, g3_ref[...])


def _const_spec(shape):
    zeros = (0,) * len(shape)
    return pl.BlockSpec(shape, lambda *_: zeros, pipeline_mode=pl.Buffered(1))


def _rope_tables(seq):
    inv_freq = ROPE_THETA ** (-np.arange(ROPE_HALF, dtype=np.float64) / ROPE_HALF)
    ang = np.arange(seq, dtype=np.float64)[:, None] * inv_freq[None, :]
    cos, sin = np.cos(ang), np.sin(ang)
    cos_t, sin_t = cos.T, sin.T
    ones = np.ones((seq, HEAD_DIM - ROPE_DIM))
    zeros = np.zeros((seq, HEAD_DIM - ROPE_DIM))
    zh = np.zeros((seq, ROPE_HALF))
    cos_h = np.concatenate([cos, cos, ones], axis=1)
    sa_h = np.concatenate([-sin, zh, zeros], axis=1)
    sb_h = np.concatenate([zh, sin, zeros], axis=1)
    tile = lambda t: np.concatenate([t] * HEADS_PER_LANE_TILE, axis=1)
    tables = (cos_t, sin_t, tile(cos_h), tile(sa_h), tile(sb_h))
    return tuple(jnp.asarray(np.ascontiguousarray(t), dtype=F32) for t in tables)


def _layer(h, p, mix_pre_g, w_in, gmlp_ln_g, gmlp_ln_b, w_s, b_s, attn_out_g,
           gmlp_out_g, w_o, mix_post_g, mlp_pre_g, w_up, w_down, mlp_post_g,
           w_ple, w_ple_gate, b_ple_gate, ple_post_g):
    B, S, D = h.shape
    assert D == D_MODEL and S % MOBA_BLOCK == 0
    nb = S // MOBA_BLOCK
    T = MOBA_BLOCK
    row = lambda v: v.reshape(1, -1).astype(F32)

    wq = w_in[:, :D_ATTN] * (HEAD_DIM ** -0.5 * math.log2(math.e))
    wk = w_in[:, D_ATTN:2 * D_ATTN]
    wv = w_in[:, 2 * D_ATTN:3 * D_ATTN]
    wft = jnp.concatenate([wq, wv], axis=1).T.astype(BF16)
    wtok = jnp.concatenate([wk, w_in[:, 3 * D_ATTN:]], axis=1).astype(BF16)
    cos_t, sin_t, cos_k, sa_k, sb_k = _rope_tables(S)

    cparams = functools.partial(pltpu.CompilerParams, vmem_limit_bytes=VMEM_LIMIT_BYTES)

    qt, k4, vt, km, ug, vn = pl.pallas_call(
        _inproj_kernel,
        grid=(B, nb),
        in_specs=[
            pl.BlockSpec((1, T, D), lambda b, i: (b, i, 0)),
            _const_spec((1, D)),
            _const_spec((2 * D_ATTN, D)),
            _const_spec((D, D_ATTN + 2 * D_GMLP)),
            pl.BlockSpec((ROPE_HALF, T), lambda b, i: (0, i)),
            pl.BlockSpec((ROPE_HALF, T), lambda b, i: (0, i)),
            pl.BlockSpec((T, LANES), lambda b, i: (i, 0)),
            pl.BlockSpec((T, LANES), lambda b, i: (i, 0)),
            pl.BlockSpec((T, LANES), lambda b, i: (i, 0)),
            _const_spec((1, D_GMLP)),
            _const_spec((1, D_GMLP)),
        ],
        out_specs=[
            pl.BlockSpec((1, 1, D_ATTN, T), lambda b, i: (b, i, 0, 0)),
            pl.BlockSpec((1, 1, T, D_ATTN), lambda b, i: (b, i, 0, 0)),
            pl.BlockSpec((1, 1, N_HEADS * V_ROWS, T), lambda b, i: (b, i, 0, 0)),
            pl.BlockSpec((1, 1, 1, D_ATTN), lambda b, i: (b, i, 0, 0)),
            pl.BlockSpec((1, T, D_GMLP), lambda b, i: (b, i, 0)),
            pl.BlockSpec((1, T, D_GMLP), lambda b, i: (b, i, 0)),
        ],
        out_shape=[
            jax.ShapeDtypeStruct((B, nb, D_ATTN, T), BF16),
            jax.ShapeDtypeStruct((B, nb, T, D_ATTN), BF16),
            jax.ShapeDtypeStruct((B, nb, N_HEADS * V_ROWS, T), BF16),
            jax.ShapeDtypeStruct((B, nb, 1, D_ATTN), F32),
            jax.ShapeDtypeStruct((B, S, D_GMLP), BF16),
            jax.ShapeDtypeStruct((B, S, D_GMLP), BF16),
        ],
        compiler_params=cparams(dimension_semantics=("parallel", "parallel")),
        name="inproj",
    )(h, row(mix_pre_g), wft, wtok, cos_t, sin_t, cos_k, sa_k, sb_k,
      row(gmlp_ln_g), row(gmlp_ln_b))

    km = km.reshape(B, nb, D_ATTN)
    bs_exp = jnp.repeat(b_s.T, GROUP_DIM, axis=1).astype(F32)
    woa = w_o[:D_ATTN].astype(BF16)
    wog = w_o[D_ATTN:].astype(BF16)

    h1 = pl.pallas_call(
        _mix_kernel,
        grid=(B, nb),
        in_specs=[
            pl.BlockSpec((1, 1, D_ATTN, T), lambda b, i: (b, i, 0, 0)),
            pl.BlockSpec((1, nb, T, D_ATTN), lambda b, i: (b, 0, 0, 0)),
            pl.BlockSpec((1, nb, N_HEADS * V_ROWS, T), lambda b, i: (b, 0, 0, 0)),
            pl.BlockSpec((1, nb, D_ATTN), lambda b, i: (b, 0, 0)),
            pl.BlockSpec((1, T, D_GMLP), lambda b, i: (b, i, 0)),
            pl.BlockSpec((1, T, D_GMLP), lambda b, i: (b, i, 0)),
            _const_spec((N_GROUPS, GMLP_CHUNK, GMLP_CHUNK)),
            _const_spec((GMLP_CHUNK, D_GMLP)),
            _const_spec((D_ATTN, 1)),
            _const_spec((1, D_GMLP)),
            _const_spec((D_ATTN, D)),
            _const_spec((D_GMLP, D)),
            _const_spec((1, D)),
            pl.BlockSpec((1, T, D), lambda b, i: (b, i, 0)),
        ],
        out_specs=pl.BlockSpec((1, T, D), lambda b, i: (b, i, 0)),
        out_shape=jax.ShapeDtypeStruct((B, S, D), F32),
        scratch_shapes=([pltpu.VMEM((nb, T), F32)] * N_HEADS
                        + [pltpu.VMEM((LANES, T), BF16)] * N_HEADS
                        + [pltpu.VMEM((1, T), F32)] * N_HEADS
                        + [pltpu.VMEM((V_ROWS, T), F32)] * N_HEADS
                        + [pltpu.VMEM((MOBA_BLOCK, T), F32)] * N_HEADS),
        compiler_params=cparams(dimension_semantics=("parallel", "arbitrary")),
        name="mix",
    )(qt, k4, vt, km, ug, vn, w_s.astype(F32), bs_exp,
      attn_out_g.reshape(-1, 1).astype(F32), row(gmlp_out_g), woa, wog,
      row(mix_post_g), h)

    n_tok = B * S
    tm = MLP_TOKENS
    out = pl.pallas_call(
        _mlp_kernel,
        grid=(n_tok // tm,),
        in_specs=[
            pl.BlockSpec((tm, D), lambda t: (t, 0)),
            pl.BlockSpec((tm, D_PLE), lambda t: (t, 0)),
            _const_spec((1, D)),
            pl.BlockSpec(memory_space=pl.ANY),
            pl.BlockSpec(memory_space=pl.ANY),
            _const_spec((1, D)),
            pl.BlockSpec(memory_space=pl.ANY),
            _const_spec((1, D)),
            pl.BlockSpec(memory_space=pl.ANY),
            _const_spec((1, D)),
        ],
        out_specs=pl.BlockSpec((tm, D), lambda t: (t, 0)),
        out_shape=jax.ShapeDtypeStruct((n_tok, D), F32),
        scratch_shapes=[
            pltpu.VMEM((D, D_FF), BF16),
            pltpu.VMEM((D_FF, D), BF16),
            pltpu.VMEM((D, D), BF16),
            pltpu.VMEM((D_PLE, D), BF16),
            pltpu.VMEM((2, WEIGHT_STAGE_BYTES // (4 * D_FF), D_FF), F32),
            pltpu.VMEM((2, WEIGHT_STAGE_BYTES // (4 * D), D), F32),
            pltpu.SemaphoreType.DMA((2,)),
        ],
        compiler_params=cparams(dimension_semantics=("arbitrary",)),
        name="mlp",
    )(h1.reshape(n_tok, D), p.reshape(n_tok, D_PLE), row(mlp_pre_g),
      w_up.astype(F32), w_down.astype(F32), row(mlp_post_g),
      w_ple_gate.astype(F32), row(b_ple_gate), w_ple.astype(F32), row(ple_post_g))
    return out.reshape(B, S, D)


def kernel(x, p, mix_pre_g, w_in, gmlp_ln_g, gmlp_ln_b, w_s, b_s, attn_out_g,
           gmlp_out_g, w_o, mix_post_g, mlp_pre_g, w_up, w_down, mlp_post_g,
           w_ple, w_ple_gate, b_ple_gate, ple_post_g):
    h = x
    for i in range(w_in.shape[0]):
        h = _layer(h, p[i], mix_pre_g[i], w_in[i], gmlp_ln_g[i], gmlp_ln_b[i],
                   w_s[i], b_s[i], attn_out_g[i], gmlp_out_g[i], w_o[i],
                   mix_post_g[i], mlp_pre_g[i], w_up[i], w_down[i], mlp_post_g[i],
                   w_ple[i], w_ple_gate[i], b_ple_gate[i], ple_post_g[i])
    return h
```

```python
import functools
import math

import jax
import jax.numpy as jnp
import numpy as np
from jax import lax
from jax.experimental import pallas as pl
from jax.experimental.pallas import tpu as pltpu

D_MODEL = 1024
D_PLE = 256
N_HEADS = 8
HEAD_DIM = 64
D_ATTN = N_HEADS * HEAD_DIM
N_GROUPS = 8
GROUP_DIM = 64
D_GMLP = N_GROUPS * GROUP_DIM
D_FF = 4 * D_MODEL
MOBA_BLOCK = 256
MOBA_TOPK = 3
GMLP_CHUNK = 128
ROPE_THETA = 500000.0
ROPE_DIM = HEAD_DIM // 4
ROPE_HALF = ROPE_DIM // 2
NORM_EPS = 1e-6
NEG_INF = -1e30

LANES = 128
HEADS_PER_LANE_TILE = LANES // HEAD_DIM
BF16_SUBLANES = 16
V_ROWS = HEAD_DIM + BF16_SUBLANES
BIG = 1e30
SCORE_LEAD = 2
VMEM_LIMIT_BYTES = 48 * 1024 * 1024
MLP_TOKENS = 512
FF_CHUNK = 1024
WEIGHT_STAGE_BYTES = 2 * 1024 * 1024

F32 = jnp.float32
BF16 = jnp.bfloat16


def _dot(a, b):
    return jnp.dot(a, b, preferred_element_type=F32)


def _dot_nt(a, b):
    return lax.dot_general(a, b, (((1,), (1,)), ((), ())), preferred_element_type=F32)


def _dot_tn(a, b):
    return lax.dot_general(a, b, (((0,), (0,)), ((), ())), preferred_element_type=F32)


def _rms_rows(x, g):
    ms = jnp.mean(x * x, axis=-1, keepdims=True)
    return x * lax.rsqrt(ms + NORM_EPS) * g


def _gelu_exact(x):
    return 0.5 * x * (1.0 + lax.erf(x * math.sqrt(0.5)))


def _mixer_kernel(x_ref, g_ref, wft_ref, wtok_ref, cost_ref, sint_ref,
                  cosk_ref, sak_ref, sbk_ref, lng_ref, lnb_ref,
                  ws_ref, bs_ref, ag_ref, gg_ref, woa_ref, wog_ref, pg_ref,
                  o_ref, k_scr, vt_scr, km_scr, ug_scr, vn_scr, *scratch):
    sel_scr, qz_scr, m_scr, acc_scr, s_scr = (
        scratch[j * N_HEADS:(j + 1) * N_HEADS] for j in range(5))
    i = pl.program_id(1)
    nb = k_scr.shape[0]
    tq = MOBA_BLOCK

    @pl.when((pl.program_id(0) == 0) & (i == 0))
    def _():
        km_scr[...] = jnp.zeros(km_scr.shape, F32)

    x = x_ref[0]
    hn = _rms_rows(x, g_ref[...]).astype(BF16)

    gv = _dot(hn, wtok_ref[:, D_ATTN + D_GMLP:])
    g2 = _gelu_exact(gv)
    mu = jnp.mean(g2, axis=-1, keepdims=True)
    xc = g2 - mu
    var = jnp.mean(xc * xc, axis=-1, keepdims=True)
    vn = xc * lax.rsqrt(var + NORM_EPS) * lng_ref[...] + lnb_ref[...]
    vn_scr[...] = vn.astype(BF16)

    u = _dot(hn, wtok_ref[:, D_ATTN:D_ATTN + D_GMLP])
    ug_scr[...] = _gelu_exact(u).astype(BF16)

    tok = _dot(hn, wtok_ref[:, :D_ATTN])
    cos_k = cosk_ref[...]
    sa_k = sak_ref[...]
    sb_k = sbk_ref[...]
    kparts = []
    for j in range(D_ATTN // LANES):
        kb = tok[:, j * LANES:(j + 1) * LANES]
        up = pltpu.roll(kb, LANES - ROPE_HALF, axis=1)
        dn = pltpu.roll(kb, ROPE_HALF, axis=1)
        kparts.append(kb * cos_k + up * sa_k + dn * sb_k)
    k = jnp.concatenate(kparts, axis=1)
    k_scr[i] = k.astype(BF16)
    km_scr[pl.ds(i, 1), :] = jnp.mean(k, axis=0, keepdims=True)

    ft = _dot_nt(wft_ref[...], hn)
    cos_t = cost_ref[...]
    sin_t = sint_ref[...]
    ones = jnp.ones((V_ROWS - HEAD_DIM, tq), F32)
    vparts = []
    for h in range(N_HEADS):
        vparts.append(ft[D_ATTN + h * HEAD_DIM:D_ATTN + (h + 1) * HEAD_DIM])
        vparts.append(ones)
    vt_scr[i] = jnp.concatenate(vparts, axis=0).astype(BF16)

    blk_row = lax.broadcasted_iota(jnp.int32, (nb, tq), 0)
    past = blk_row < i

    km = km_scr[...]
    km_hi = km.astype(BF16)
    r1 = km - km_hi.astype(F32)
    km_mid = r1.astype(BF16)
    km_lo = (r1 - km_mid.astype(F32)).astype(BF16)
    km3 = jnp.concatenate([km_hi, km_mid, km_lo], axis=0)

    def scores(blk, h):
        hp = h // HEADS_PER_LANE_TILE
        return _dot(k_scr[blk, :, hp * LANES:(hp + 1) * LANES], qz_scr[h][...])

    def v_rows(blk, h):
        return vt_scr[blk, h * V_ROWS:(h + 1) * V_ROWS, :]

    zeros_head = jnp.zeros((HEAD_DIM, tq), F32)
    for h in range(N_HEADS):
        hh = h % HEADS_PER_LANE_TILE
        r0 = h * HEAD_DIM
        x1 = ft[r0:r0 + ROPE_HALF]
        x2 = ft[r0 + ROPE_HALF:r0 + ROPE_DIM]
        q_h = [x1 * cos_t - x2 * sin_t, x2 * cos_t + x1 * sin_t, ft[r0 + ROPE_DIM:r0 + HEAD_DIM]]
        padded = [zeros_head] * hh + q_h + [zeros_head] * (HEADS_PER_LANE_TILE - 1 - hh)
        qz_scr[h][...] = jnp.concatenate(padded, axis=0).astype(BF16)

    @pl.when(i >= 0)
    def _():
        for h in range(N_HEADS):
            hp = h // HEADS_PER_LANE_TILE
            s_scr[h][...] = scores(0, h)

            g3 = _dot(km3[:, hp * LANES:(hp + 1) * LANES], qz_scr[h][...])
            gate = g3[0:nb] + g3[nb:2 * nb] + g3[2 * nb:3 * nb]
            gate = jnp.where(past, gate, NEG_INF)
            rank = jnp.zeros((nb, tq), jnp.int32)
            for m in range(nb):
                gm = gate[m:m + 1, :]
                beats = (gm > gate) | ((gm == gate) & (m < blk_row))
                rank = rank + beats.astype(jnp.int32)
            sel = past & (rank < MOBA_TOPK)
            sel_scr[h][...] = sel.astype(F32)
            m_scr[h][...] = jnp.full((1, tq), NEG_INF, F32)
            acc_scr[h][...] = jnp.zeros((V_ROWS, tq), F32)

    def past_blocks(kb0, n_blocks):
        units = [(kb0 + j, h) for j in range(n_blocks) for h in range(N_HEADS)]
        ahead = {}
        for u in range(-SCORE_LEAD, len(units)):
            if 0 <= u + SCORE_LEAD < len(units):
                blk, head = units[u + SCORE_LEAD]
                ahead[u + SCORE_LEAD] = scores(blk + 1, head)
            if u < 0:
                continue
            kb, h = units[u]
            s = s_scr[h][...]
            keep = sel_scr[h][pl.ds(kb, 1), :] > 0.5
            m_run = m_scr[h][...]
            m_new = jnp.where(keep, jnp.maximum(m_run, jnp.max(s, axis=0, keepdims=True)), m_run)
            m_sub = jnp.where(keep, m_new, BIG)
            alpha = jnp.exp2(m_run - m_new)
            pb = jnp.exp2(s - m_sub).astype(BF16)
            m_scr[h][...] = m_new
            acc_scr[h][...] = alpha * acc_scr[h][...] + _dot(v_rows(kb, h), pb)
            s_scr[h][...] = ahead.pop(u)

    def two_blocks(t, carry):
        past_blocks(2 * t, 2)
        return carry

    lax.fori_loop(0, i // 2, two_blocks, 0)

    @pl.when(i % 2 == 1)
    def _():
        past_blocks(i - 1, 1)

    kpos = lax.broadcasted_iota(jnp.int32, (MOBA_BLOCK, tq), 0)
    qpos = lax.broadcasted_iota(jnp.int32, (MOBA_BLOCK, tq), 1)
    causal = kpos <= qpos
    t_idx = lax.broadcasted_iota(jnp.int32, (GMLP_CHUNK, GMLP_CHUNK), 0)
    s_idx = lax.broadcasted_iota(jnp.int32, (GMLP_CHUNK, GMLP_CHUNK), 1)
    tril = s_idx <= t_idx
    lane = lax.broadcasted_iota(jnp.int32, (GMLP_CHUNK, LANES), 1)
    first_group = lane < GROUP_DIM
    n_lane_tiles = D_GMLP // LANES
    n_chunks = tq // GMLP_CHUNK
    assert n_lane_tiles * n_chunks == N_HEADS
    heads_out = []
    cols = [[None] * n_lane_tiles for _ in range(n_chunks)]
    for h in range(N_HEADS):
        c, gp = divmod(h, n_lane_tiles)
        vp = vn_scr[c * GMLP_CHUNK:(c + 1) * GMLP_CHUNK, gp * LANES:(gp + 1) * LANES]
        w_a = jnp.where(tril, ws_ref[2 * gp], 0.0).astype(BF16)
        w_b = jnp.where(tril, ws_ref[2 * gp + 1], 0.0).astype(BF16)
        cols[c][gp] = jnp.where(first_group, _dot(w_a, vp), _dot(w_b, vp))

        st = jnp.where(causal, s_scr[h][...], NEG_INF)
        m_run = m_scr[h][...]
        m_new = jnp.maximum(m_run, jnp.max(st, axis=0, keepdims=True))
        alpha = jnp.exp2(m_run - m_new)
        p = jnp.exp2(st - m_new).astype(BF16)
        acc = alpha * acc_scr[h][...] + _dot(v_rows(i, h), p)
        heads_out.append(acc[0:HEAD_DIM] / acc[HEAD_DIM:HEAD_DIM + 1])

    mixed = jnp.concatenate(
        [jnp.concatenate(cols[c], axis=1) + bs_ref[...] for c in range(n_chunks)], axis=0)
    gm_out = ug_scr[...].astype(F32) * mixed
    gm_n = _rms_rows(gm_out, gg_ref[...]).astype(BF16)
    y = _dot(gm_n, wog_ref[...])

    attn_t = jnp.concatenate(heads_out, axis=0)
    ms_a = jnp.mean(attn_t * attn_t, axis=0, keepdims=True)
    attn_n = (attn_t * lax.rsqrt(ms_a + NORM_EPS) * ag_ref[...]).astype(BF16)
    y = y + _dot_tn(attn_n, woa_ref[...])
    o_ref[0] = x_ref[0] + _rms_rows(y, pg_ref[...])


def _load_weight_bf16(src_hbm, dst_ref, stage_ref, sem_ref):
    chunk = stage_ref.shape[1]
    n_rows = src_hbm.shape[0]
    chunk = min(chunk, n_rows)
    n_chunks = n_rows // chunk
    assert n_chunks * chunk == n_rows

    def copy(c, slot):
        return pltpu.make_async_copy(src_hbm.at[pl.ds(c * chunk, chunk), :],
                                     stage_ref.at[slot, pl.ds(0, chunk), :],
                                     sem_ref.at[slot])

    copy(0, 0).start()

    def body(c, carry):
        slot = lax.rem(c, 2)

        @pl.when(c + 1 < n_chunks)
        def _():
            copy(c + 1, 1 - slot).start()

        copy(c, slot).wait()
        rows = pl.ds(pl.multiple_of(c * chunk, chunk), chunk)
        dst_ref[rows, :] = stage_ref[slot, pl.ds(0, chunk), :].astype(BF16)
        return carry

    lax.fori_loop(0, n_chunks, body, 0)


def _mlp_kernel(h_ref, p_ref, g1_ref, wup_hbm, wdn_hbm, g2_ref, wgate_hbm,
                bgate_ref, wple_hbm, g3_ref, o_ref,
                wup_ref, wdn_ref, wgate_ref, wple_ref, wide_stage, narrow_stage, sem):
    @pl.when(pl.program_id(0) == 0)
    def _():
        _load_weight_bf16(wup_hbm, wup_ref, wide_stage, sem)
        _load_weight_bf16(wdn_hbm, wdn_ref, narrow_stage, sem)
        _load_weight_bf16(wgate_hbm, wgate_ref, narrow_stage, sem)
        _load_weight_bf16(wple_hbm, wple_ref, narrow_stage, sem)

    h = h_ref[...]
    hn = _rms_rows(h, g1_ref[...]).astype(BF16)
    acc = jnp.zeros(h.shape, F32)
    for c in range(D_FF // FF_CHUNK):
        cs = slice(c * FF_CHUNK, (c + 1) * FF_CHUNK)
        up = jnp.maximum(_dot(hn, wup_ref[:, cs]), 0.0)
        acc = acc + _dot((up * up).astype(BF16), wdn_ref[cs, :])
    h = h + _rms_rows(acc, g2_ref[...])
    z = _dot(h.astype(BF16), wgate_ref[...]) + bgate_ref[...]
    gate = 1.0 / (1.0 + jnp.exp(-z))
    ple = _dot(p_ref[...].astype(BF16), wple_ref[...]) * gate
    o_ref[...] = h + _rms_rows(ple, g3_ref[...])


def _const_spec(shape):
    zeros = (0,) * len(shape)
    return pl.BlockSpec(shape, lambda *_: zeros, pipeline_mode=pl.Buffered(1))


def _rope_tables(seq):
    inv_freq = ROPE_THETA ** (-np.arange(ROPE_HALF, dtype=np.float64) / ROPE_HALF)
    ang = np.arange(seq, dtype=np.float64)[:, None] * inv_freq[None, :]
    cos, sin = np.cos(ang), np.sin(ang)
    cos_t, sin_t = cos.T, sin.T
    ones = np.ones((seq, HEAD_DIM - ROPE_DIM))
    zeros = np.zeros((seq, HEAD_DIM - ROPE_DIM))
    zh = np.zeros((seq, ROPE_HALF))
    cos_h = np.concatenate([cos, cos, ones], axis=1)
    sa_h = np.concatenate([-sin, zh, zeros], axis=1)
    sb_h = np.concatenate([zh, sin, zeros], axis=1)
    tile = lambda t: np.concatenate([t] * HEADS_PER_LANE_TILE, axis=1)
    tables = (cos_t, sin_t, tile(cos_h), tile(sa_h), tile(sb_h))
    return tuple(jnp.asarray(np.ascontiguousarray(t), dtype=F32) for t in tables)


def _layer(h, p, mix_pre_g, w_in, gmlp_ln_g, gmlp_ln_b, w_s, b_s, attn_out_g,
           gmlp_out_g, w_o, mix_post_g, mlp_pre_g, w_up, w_down, mlp_post_g,
           w_ple, w_ple_gate, b_ple_gate, ple_post_g):
    B, S, D = h.shape
    assert D == D_MODEL and S % MOBA_BLOCK == 0
    nb = S // MOBA_BLOCK
    T = MOBA_BLOCK
    row = lambda v: v.reshape(1, -1).astype(F32)

    wq = w_in[:, :D_ATTN] * (HEAD_DIM ** -0.5 * math.log2(math.e))
    wk = w_in[:, D_ATTN:2 * D_ATTN]
    wv = w_in[:, 2 * D_ATTN:3 * D_ATTN]
    wft = jnp.concatenate([wq, wv], axis=1).T.astype(BF16)
    wtok = jnp.concatenate([wk, w_in[:, 3 * D_ATTN:]], axis=1).astype(BF16)
    cos_t, sin_t, cos_k, sa_k, sb_k = _rope_tables(S)

    cparams = functools.partial(pltpu.CompilerParams, vmem_limit_bytes=VMEM_LIMIT_BYTES)

    bs_exp = jnp.repeat(b_s.T, GROUP_DIM, axis=1).astype(F32)
    woa = w_o[:D_ATTN].astype(BF16)
    wog = w_o[D_ATTN:].astype(BF16)

    h1 = pl.pallas_call(
        _mixer_kernel,
        grid=(B, nb),
        in_specs=[
            pl.BlockSpec((1, T, D), lambda b, i: (b, i, 0)),
            _const_spec((1, D)),
            _const_spec((2 * D_ATTN, D)),
            _const_spec((D, D_ATTN + 2 * D_GMLP)),
            pl.BlockSpec((ROPE_HALF, T), lambda b, i: (0, i)),
            pl.BlockSpec((ROPE_HALF, T), lambda b, i: (0, i)),
            pl.BlockSpec((T, LANES), lambda b, i: (i, 0)),
            pl.BlockSpec((T, LANES), lambda b, i: (i, 0)),
            pl.BlockSpec((T, LANES), lambda b, i: (i, 0)),
            _const_spec((1, D_GMLP)),
            _const_spec((1, D_GMLP)),
            _const_spec((N_GROUPS, GMLP_CHUNK, GMLP_CHUNK)),
            _const_spec((GMLP_CHUNK, D_GMLP)),
            _const_spec((D_ATTN, 1)),
            _const_spec((1, D_GMLP)),
            _const_spec((D_ATTN, D)),
            _const_spec((D_GMLP, D)),
            _const_spec((1, D)),
        ],
        out_specs=pl.BlockSpec((1, T, D), lambda b, i: (b, i, 0)),
        out_shape=jax.ShapeDtypeStruct((B, S, D), F32),
        scratch_shapes=([pltpu.VMEM((nb, T, D_ATTN), BF16),
                         pltpu.VMEM((nb, N_HEADS * V_ROWS, T), BF16),
                         pltpu.VMEM((nb, D_ATTN), F32),
                         pltpu.VMEM((T, D_GMLP), BF16),
                         pltpu.VMEM((T, D_GMLP), BF16)]
                        + [pltpu.VMEM((nb, T), F32)] * N_HEADS
                        + [pltpu.VMEM((LANES, T), BF16)] * N_HEADS
                        + [pltpu.VMEM((1, T), F32)] * N_HEADS
                        + [pltpu.VMEM((V_ROWS, T), F32)] * N_HEADS
                        + [pltpu.VMEM((MOBA_BLOCK, T), F32)] * N_HEADS),
        compiler_params=cparams(dimension_semantics=("arbitrary", "arbitrary")),
        name="mixer",
    )(h, row(mix_pre_g), wft, wtok, cos_t, sin_t, cos_k, sa_k, sb_k,
      row(gmlp_ln_g), row(gmlp_ln_b), w_s.astype(F32), bs_exp,
      attn_out_g.reshape(-1, 1).astype(F32), row(gmlp_out_g), woa, wog,
      row(mix_post_g))

    n_tok = B * S
    tm = MLP_TOKENS
    out = pl.pallas_call(
        _mlp_kernel,
        grid=(n_tok // tm,),
        in_specs=[
            pl.BlockSpec((tm, D), lambda t: (t, 0)),
            pl.BlockSpec((tm, D_PLE), lambda t: (t, 0)),
            _const_spec((1, D)),
            pl.BlockSpec(memory_space=pl.ANY),
            pl.BlockSpec(memory_space=pl.ANY),
            _const_spec((1, D)),
            pl.BlockSpec(memory_space=pl.ANY),
            _const_spec((1, D)),
            pl.BlockSpec(memory_space=pl.ANY),
            _const_spec((1, D)),
        ],
        out_specs=pl.BlockSpec((tm, D), lambda t: (t, 0)),
        out_shape=jax.ShapeDtypeStruct((n_tok, D), F32),
        scratch_shapes=[
            pltpu.VMEM((D, D_FF), BF16),
            pltpu.VMEM((D_FF, D), BF16),
            pltpu.VMEM((D, D), BF16),
            pltpu.VMEM((D_PLE, D), BF16),
            pltpu.VMEM((2, WEIGHT_STAGE_BYTES // (4 * D_FF), D_FF), F32),
            pltpu.VMEM((2, WEIGHT_STAGE_BYTES // (4 * D), D), F32),
            pltpu.SemaphoreType.DMA((2,)),
        ],
        compiler_params=cparams(dimension_semantics=("arbitrary",)),
        name="mlp",
    )(h1.reshape(n_tok, D), p.reshape(n_tok, D_PLE), row(mlp_pre_g),
      w_up.astype(F32), w_down.astype(F32), row(mlp_post_g),
      w_ple_gate.astype(F32), row(b_ple_gate), w_ple.astype(F32), row(ple_post_g))
    return out.reshape(B, S, D)


def kernel(x, p, mix_pre_g, w_in, gmlp_ln_g, gmlp_ln_b, w_s, b_s, attn_out_g,
           gmlp_out_g, w_o, mix_post_g, mlp_pre_g, w_up, w_down, mlp_post_g,
           w_ple, w_ple_gate, b_ple_gate, ple_post_g):
    h = x
    for i in range(w_in.shape[0]):
        h = _layer(h, p[i], mix_pre_g[i], w_in[i], gmlp_ln_g[i], gmlp_ln_b[i],
                   w_s[i], b_s[i], attn_out_g[i], gmlp_out_g[i], w_o[i],
                   mix_post_g[i], mlp_pre_g[i], w_up[i], w_down[i], mlp_post_g[i],
                   w_ple[i], w_ple_gate[i], b_ple_gate[i], ple_post_g[i])
    return h
```

```python
import functools
import math

import jax
import jax.numpy as jnp
import numpy as np
from jax import lax
from jax.experimental import pallas as pl
from jax.experimental.pallas import tpu as pltpu

D_MODEL = 1024
D_PLE = 256
N_HEADS = 8
HEAD_DIM = 64
D_ATTN = N_HEADS * HEAD_DIM
N_GROUPS = 8
GROUP_DIM = 64
D_GMLP = N_GROUPS * GROUP_DIM
D_FF = 4 * D_MODEL
MOBA_BLOCK = 256
MOBA_TOPK = 3
GMLP_CHUNK = 128
ROPE_THETA = 500000.0
ROPE_DIM = HEAD_DIM // 4
ROPE_HALF = ROPE_DIM // 2
NORM_EPS = 1e-6
NEG_INF = -1e30

LANES = 128
HEADS_PER_LANE_TILE = LANES // HEAD_DIM
BF16_SUBLANES = 16
V_ROWS = HEAD_DIM + BF16_SUBLANES
BIG = 1e30
SCORE_LEAD = 2
VMEM_LIMIT_BYTES = 48 * 1024 * 1024
MLP_TOKENS = 512
FF_CHUNK = 1024
WEIGHT_STAGE_BYTES = 2 * 1024 * 1024

F32 = jnp.float32
BF16 = jnp.bfloat16


def _dot(a, b):
    return jnp.dot(a, b, preferred_element_type=F32)


def _dot_nt(a, b):
    return lax.dot_general(a, b, (((1,), (1,)), ((), ())), preferred_element_type=F32)


def _dot_tn(a, b):
    return lax.dot_general(a, b, (((0,), (0,)), ((), ())), preferred_element_type=F32)


def _rms_rows(x, g):
    ms = jnp.mean(x * x, axis=-1, keepdims=True)
    return x * lax.rsqrt(ms + NORM_EPS) * g


def _gelu_exact(x):
    return 0.5 * x * (1.0 + lax.erf(x * math.sqrt(0.5)))


def _mixer_kernel(x_ref, g_ref, win_hbm, qscale_ref, cost_ref, sint_ref,
                  cosk_ref, sak_ref, sbk_ref, lng_ref, lnb_ref,
                  ws_ref, bs_ref, ag_ref, gg_ref, wo_hbm, pg_ref,
                  o_ref, k_scr, vt_scr, km_scr, ug_scr, vn_scr,
                  win_ref, wo_ref, win_stage, wo_stage, sem, *scratch):
    sel_scr, qz_scr, m_scr, acc_scr, s_scr = (
        scratch[j * N_HEADS:(j + 1) * N_HEADS] for j in range(5))
    i = pl.program_id(1)
    nb = k_scr.shape[0]
    tq = MOBA_BLOCK

    @pl.when((pl.program_id(0) == 0) & (i == 0))
    def _():
        km_scr[...] = jnp.zeros(km_scr.shape, F32)
        _load_weight_bf16(win_hbm, win_ref, win_stage, sem, col_scale=qscale_ref[...])
        _load_weight_bf16(wo_hbm, wo_ref, wo_stage, sem)

    x = x_ref[0]
    hn = _rms_rows(x, g_ref[...]).astype(BF16)

    gv = _dot(hn, win_ref[:, 3 * D_ATTN + D_GMLP:])
    g2 = _gelu_exact(gv)
    mu = jnp.mean(g2, axis=-1, keepdims=True)
    xc = g2 - mu
    var = jnp.mean(xc * xc, axis=-1, keepdims=True)
    vn = xc * lax.rsqrt(var + NORM_EPS) * lng_ref[...] + lnb_ref[...]
    vn_scr[...] = vn.astype(BF16)

    qt = _dot(hn, win_ref[:, :D_ATTN]).T

    tok = _dot(hn, win_ref[:, D_ATTN:2 * D_ATTN])
    cos_k = cosk_ref[...]
    sa_k = sak_ref[...]
    sb_k = sbk_ref[...]
    kparts = []
    for j in range(D_ATTN // LANES):
        kb = tok[:, j * LANES:(j + 1) * LANES]
        up = pltpu.roll(kb, LANES - ROPE_HALF, axis=1)
        dn = pltpu.roll(kb, ROPE_HALF, axis=1)
        kparts.append(kb * cos_k + up * sa_k + dn * sb_k)
    k = jnp.concatenate(kparts, axis=1)
    k_scr[i] = k.astype(BF16)
    km_scr[pl.ds(i, 1), :] = jnp.mean(k, axis=0, keepdims=True)

    u = _dot(hn, win_ref[:, 3 * D_ATTN:3 * D_ATTN + D_GMLP])
    ug_scr[...] = _gelu_exact(u).astype(BF16)

    vt = _dot(hn, win_ref[:, 2 * D_ATTN:3 * D_ATTN]).T
    cos_t = cost_ref[...]
    sin_t = sint_ref[...]
    ones = jnp.ones((V_ROWS - HEAD_DIM, tq), F32)
    vparts = []
    for h in range(N_HEADS):
        vparts.append(vt[h * HEAD_DIM:(h + 1) * HEAD_DIM])
        vparts.append(ones)
    vt_scr[i] = jnp.concatenate(vparts, axis=0).astype(BF16)

    blk_row = lax.broadcasted_iota(jnp.int32, (nb, tq), 0)
    past = blk_row < i

    km = km_scr[...]
    km_hi = km.astype(BF16)
    r1 = km - km_hi.astype(F32)
    km_mid = r1.astype(BF16)
    km_lo = (r1 - km_mid.astype(F32)).astype(BF16)
    km3 = jnp.concatenate([km_hi, km_mid, km_lo], axis=0)

    def scores(blk, h):
        hp = h // HEADS_PER_LANE_TILE
        return _dot(k_scr[blk, :, hp * LANES:(hp + 1) * LANES], qz_scr[h][...])

    def v_rows(blk, h):
        return vt_scr[blk, h * V_ROWS:(h + 1) * V_ROWS, :]

    zeros_head = jnp.zeros((HEAD_DIM, tq), F32)
    for h in range(N_HEADS):
        hh = h % HEADS_PER_LANE_TILE
        r0 = h * HEAD_DIM
        x1 = qt[r0:r0 + ROPE_HALF]
        x2 = qt[r0 + ROPE_HALF:r0 + ROPE_DIM]
        q_h = [x1 * cos_t - x2 * sin_t, x2 * cos_t + x1 * sin_t, qt[r0 + ROPE_DIM:r0 + HEAD_DIM]]
        padded = [zeros_head] * hh + q_h + [zeros_head] * (HEADS_PER_LANE_TILE - 1 - hh)
        qz_scr[h][...] = jnp.concatenate(padded, axis=0).astype(BF16)

    @pl.when(i >= 0)
    def _():
        for h in range(N_HEADS):
            hp = h // HEADS_PER_LANE_TILE
            s_scr[h][...] = scores(0, h)

            g3 = _dot(km3[:, hp * LANES:(hp + 1) * LANES], qz_scr[h][...])
            gate = g3[0:nb] + g3[nb:2 * nb] + g3[2 * nb:3 * nb]
            gate = jnp.where(past, gate, NEG_INF)
            rank = jnp.zeros((nb, tq), jnp.int32)
            for m in range(nb):
                gm = gate[m:m + 1, :]
                beats = (gm > gate) | ((gm == gate) & (m < blk_row))
                rank = rank + beats.astype(jnp.int32)
            sel = past & (rank < MOBA_TOPK)
            sel_scr[h][...] = sel.astype(F32)
            m_scr[h][...] = jnp.full((1, tq), NEG_INF, F32)
            acc_scr[h][...] = jnp.zeros((V_ROWS, tq), F32)

    def past_blocks(kb0, n_blocks):
        units = [(kb0 + j, h) for j in range(n_blocks) for h in range(N_HEADS)]
        ahead = {}
        for u in range(-SCORE_LEAD, len(units)):
            if 0 <= u + SCORE_LEAD < len(units):
                blk, head = units[u + SCORE_LEAD]
                ahead[u + SCORE_LEAD] = scores(blk + 1, head)
            if u < 0:
                continue
            kb, h = units[u]
            s = s_scr[h][...]
            keep = sel_scr[h][pl.ds(kb, 1), :] > 0.5
            m_run = m_scr[h][...]
            m_new = jnp.where(keep, jnp.maximum(m_run, jnp.max(s, axis=0, keepdims=True)), m_run)
            m_sub = jnp.where(keep, m_new, BIG)
            alpha = jnp.exp2(m_run - m_new)
            pb = jnp.exp2(s - m_sub).astype(BF16)
            m_scr[h][...] = m_new
            acc_scr[h][...] = alpha * acc_scr[h][...] + _dot(v_rows(kb, h), pb)
            s_scr[h][...] = ahead.pop(u)

    def two_blocks(t, carry):
        past_blocks(2 * t, 2)
        return carry

    lax.fori_loop(0, i // 2, two_blocks, 0)

    @pl.when(i % 2 == 1)
    def _():
        past_blocks(i - 1, 1)

    kpos = lax.broadcasted_iota(jnp.int32, (MOBA_BLOCK, tq), 0)
    qpos = lax.broadcasted_iota(jnp.int32, (MOBA_BLOCK, tq), 1)
    causal = kpos <= qpos
    t_idx = lax.broadcasted_iota(jnp.int32, (GMLP_CHUNK, GMLP_CHUNK), 0)
    s_idx = lax.broadcasted_iota(jnp.int32, (GMLP_CHUNK, GMLP_CHUNK), 1)
    tril = s_idx <= t_idx
    lane = lax.broadcasted_iota(jnp.int32, (GMLP_CHUNK, LANES), 1)
    first_group = lane < GROUP_DIM
    n_lane_tiles = D_GMLP // LANES
    n_chunks = tq // GMLP_CHUNK
    assert n_lane_tiles * n_chunks == N_HEADS
    heads_out = []
    cols = [[None] * n_lane_tiles for _ in range(n_chunks)]
    for h in range(N_HEADS):
        c, gp = divmod(h, n_lane_tiles)
        vp = vn_scr[c * GMLP_CHUNK:(c + 1) * GMLP_CHUNK, gp * LANES:(gp + 1) * LANES]
        w_a = jnp.where(tril, ws_ref[2 * gp], 0.0).astype(BF16)
        w_b = jnp.where(tril, ws_ref[2 * gp + 1], 0.0).astype(BF16)
        cols[c][gp] = jnp.where(first_group, _dot(w_a, vp), _dot(w_b, vp))

        st = jnp.where(causal, s_scr[h][...], NEG_INF)
        m_run = m_scr[h][...]
        m_new = jnp.maximum(m_run, jnp.max(st, axis=0, keepdims=True))
        alpha = jnp.exp2(m_run - m_new)
        p = jnp.exp2(st - m_new).astype(BF16)
        acc = alpha * acc_scr[h][...] + _dot(v_rows(i, h), p)
        heads_out.append(acc[0:HEAD_DIM] / acc[HEAD_DIM:HEAD_DIM + 1])

    mixed = jnp.concatenate(
        [jnp.concatenate(cols[c], axis=1) + bs_ref[...] for c in range(n_chunks)], axis=0)
    gm_out = ug_scr[...].astype(F32) * mixed
    gm_n = _rms_rows(gm_out, gg_ref[...]).astype(BF16)
    y = _dot(gm_n, wo_ref[D_ATTN:, :])

    attn_t = jnp.concatenate(heads_out, axis=0)
    ms_a = jnp.mean(attn_t * attn_t, axis=0, keepdims=True)
    attn_n = (attn_t * lax.rsqrt(ms_a + NORM_EPS) * ag_ref[...]).astype(BF16)
    y = y + _dot_tn(attn_n, wo_ref[:D_ATTN, :])
    o_ref[0] = x_ref[0] + _rms_rows(y, pg_ref[...])


def _load_weight_bf16(src_hbm, dst_ref, stage_ref, sem_ref, col_scale=None):
    chunk = stage_ref.shape[1]
    n_rows = src_hbm.shape[0]
    chunk = min(chunk, n_rows)
    n_chunks = n_rows // chunk
    assert n_chunks * chunk == n_rows

    def copy(c, slot):
        return pltpu.make_async_copy(src_hbm.at[pl.ds(c * chunk, chunk), :],
                                     stage_ref.at[slot, pl.ds(0, chunk), :],
                                     sem_ref.at[slot])

    copy(0, 0).start()

    def body(c, carry):
        slot = lax.rem(c, 2)

        @pl.when(c + 1 < n_chunks)
        def _():
            copy(c + 1, 1 - slot).start()

        copy(c, slot).wait()
        rows = pl.ds(pl.multiple_of(c * chunk, chunk), chunk)
        w = stage_ref[slot, pl.ds(0, chunk), :]
        if col_scale is not None:
            w = w * col_scale
        dst_ref[rows, :] = w.astype(BF16)
        return carry

    lax.fori_loop(0, n_chunks, body, 0)


def _mlp_kernel(h_ref, p_ref, g1_ref, wup_hbm, wdn_hbm, g2_ref, wgate_hbm,
                bgate_ref, wple_hbm, g3_ref, o_ref,
                wup_ref, wdn_ref, wgate_ref, wple_ref, wide_stage, narrow_stage, sem):
    @pl.when(pl.program_id(0) == 0)
    def _():
        _load_weight_bf16(wup_hbm, wup_ref, wide_stage, sem)
        _load_weight_bf16(wdn_hbm, wdn_ref, narrow_stage, sem)
        _load_weight_bf16(wgate_hbm, wgate_ref, narrow_stage, sem)
        _load_weight_bf16(wple_hbm, wple_ref, narrow_stage, sem)

    h = h_ref[...]
    hn = _rms_rows(h, g1_ref[...]).astype(BF16)
    acc = jnp.zeros(h.shape, F32)
    for c in range(D_FF // FF_CHUNK):
        cs = slice(c * FF_CHUNK, (c + 1) * FF_CHUNK)
        up = jnp.maximum(_dot(hn, wup_ref[:, cs]), 0.0)
        acc = acc + _dot((up * up).astype(BF16), wdn_ref[cs, :])
    h = h + _rms_rows(acc, g2_ref[...])
    z = _dot(h.astype(BF16), wgate_ref[...]) + bgate_ref[...]
    gate = 1.0 / (1.0 + jnp.exp(-z))
    ple = _dot(p_ref[...].astype(BF16), wple_ref[...]) * gate
    o_ref[...] = h + _rms_rows(ple, g3_ref[...])


def _const_spec(shape):
    zeros = (0,) * len(shape)
    return pl.BlockSpec(shape, lambda *_: zeros, pipeline_mode=pl.Buffered(1))


def _rope_tables(seq):
    inv_freq = ROPE_THETA ** (-np.arange(ROPE_HALF, dtype=np.float64) / ROPE_HALF)
    ang = np.arange(seq, dtype=np.float64)[:, None] * inv_freq[None, :]
    cos, sin = np.cos(ang), np.sin(ang)
    cos_t, sin_t = cos.T, sin.T
    ones = np.ones((seq, HEAD_DIM - ROPE_DIM))
    zeros = np.zeros((seq, HEAD_DIM - ROPE_DIM))
    zh = np.zeros((seq, ROPE_HALF))
    cos_h = np.concatenate([cos, cos, ones], axis=1)
    sa_h = np.concatenate([-sin, zh, zeros], axis=1)
    sb_h = np.concatenate([zh, sin, zeros], axis=1)
    tile = lambda t: np.concatenate([t] * HEADS_PER_LANE_TILE, axis=1)
    tables = (cos_t, sin_t, tile(cos_h), tile(sa_h), tile(sb_h))
    return tuple(jnp.asarray(np.ascontiguousarray(t), dtype=F32) for t in tables)


def _layer(h, p, mix_pre_g, w_in, gmlp_ln_g, gmlp_ln_b, w_s, b_s, attn_out_g,
           gmlp_out_g, w_o, mix_post_g, mlp_pre_g, w_up, w_down, mlp_post_g,
           w_ple, w_ple_gate, b_ple_gate, ple_post_g):
    B, S, D = h.shape
    assert D == D_MODEL and S % MOBA_BLOCK == 0
    nb = S // MOBA_BLOCK
    T = MOBA_BLOCK
    row = lambda v: v.reshape(1, -1).astype(F32)

    d_in_proj = 3 * D_ATTN + 2 * D_GMLP
    qscale = np.ones((1, d_in_proj), np.float32)
    qscale[:, :D_ATTN] = HEAD_DIM ** -0.5 * math.log2(math.e)
    cos_t, sin_t, cos_k, sa_k, sb_k = _rope_tables(S)

    cparams = functools.partial(pltpu.CompilerParams, vmem_limit_bytes=VMEM_LIMIT_BYTES)

    bs_exp = jnp.repeat(b_s.T, GROUP_DIM, axis=1).astype(F32)

    h1 = pl.pallas_call(
        _mixer_kernel,
        grid=(B, nb),
        in_specs=[
            pl.BlockSpec((1, T, D), lambda b, i: (b, i, 0)),
            _const_spec((1, D)),
            pl.BlockSpec(memory_space=pl.ANY),
            _const_spec((1, d_in_proj)),
            pl.BlockSpec((ROPE_HALF, T), lambda b, i: (0, i)),
            pl.BlockSpec((ROPE_HALF, T), lambda b, i: (0, i)),
            pl.BlockSpec((T, LANES), lambda b, i: (i, 0)),
            pl.BlockSpec((T, LANES), lambda b, i: (i, 0)),
            pl.BlockSpec((T, LANES), lambda b, i: (i, 0)),
            _const_spec((1, D_GMLP)),
            _const_spec((1, D_GMLP)),
            _const_spec((N_GROUPS, GMLP_CHUNK, GMLP_CHUNK)),
            _const_spec((GMLP_CHUNK, D_GMLP)),
            _const_spec((D_ATTN, 1)),
            _const_spec((1, D_GMLP)),
            pl.BlockSpec(memory_space=pl.ANY),
            _const_spec((1, D)),
        ],
        out_specs=pl.BlockSpec((1, T, D), lambda b, i: (b, i, 0)),
        out_shape=jax.ShapeDtypeStruct((B, S, D), F32),
        scratch_shapes=([pltpu.VMEM((nb, T, D_ATTN), BF16),
                         pltpu.VMEM((nb, N_HEADS * V_ROWS, T), BF16),
                         pltpu.VMEM((nb, D_ATTN), F32),
                         pltpu.VMEM((T, D_GMLP), BF16),
                         pltpu.VMEM((T, D_GMLP), BF16),
                         pltpu.VMEM((D, d_in_proj), BF16),
                         pltpu.VMEM((D_ATTN + D_GMLP, D), BF16),
                         pltpu.VMEM((2, LANES, d_in_proj), F32),
                         pltpu.VMEM((2, WEIGHT_STAGE_BYTES // (4 * D), D), F32),
                         pltpu.SemaphoreType.DMA((2,))]
                        + [pltpu.VMEM((nb, T), F32)] * N_HEADS
                        + [pltpu.VMEM((LANES, T), BF16)] * N_HEADS
                        + [pltpu.VMEM((1, T), F32)] * N_HEADS
                        + [pltpu.VMEM((V_ROWS, T), F32)] * N_HEADS
                        + [pltpu.VMEM((MOBA_BLOCK, T), F32)] * N_HEADS),
        compiler_params=cparams(dimension_semantics=("arbitrary", "arbitrary")),
        name="mixer",
    )(h, row(mix_pre_g), w_in.astype(F32), jnp.asarray(qscale), cos_t, sin_t, cos_k, sa_k, sb_k,
      row(gmlp_ln_g), row(gmlp_ln_b), w_s.astype(F32), bs_exp,
      attn_out_g.reshape(-1, 1).astype(F32), row(gmlp_out_g), w_o.astype(F32),
      row(mix_post_g))

    n_tok = B * S
    tm = MLP_TOKENS
    out = pl.pallas_call(
        _mlp_kernel,
        grid=(n_tok // tm,),
        in_specs=[
            pl.BlockSpec((tm, D), lambda t: (t, 0)),
            pl.BlockSpec((tm, D_PLE), lambda t: (t, 0)),
            _const_spec((1, D)),
            pl.BlockSpec(memory_space=pl.ANY),
            pl.BlockSpec(memory_space=pl.ANY),
            _const_spec((1, D)),
            pl.BlockSpec(memory_space=pl.ANY),
            _const_spec((1, D)),
            pl.BlockSpec(memory_space=pl.ANY),
            _const_spec((1, D)),
        ],
        out_specs=pl.BlockSpec((tm, D), lambda t: (t, 0)),
        out_shape=jax.ShapeDtypeStruct((n_tok, D), F32),
        scratch_shapes=[
            pltpu.VMEM((D, D_FF), BF16),
            pltpu.VMEM((D_FF, D), BF16),
            pltpu.VMEM((D, D), BF16),
            pltpu.VMEM((D_PLE, D), BF16),
            pltpu.VMEM((2, WEIGHT_STAGE_BYTES // (4 * D_FF), D_FF), F32),
            pltpu.VMEM((2, WEIGHT_STAGE_BYTES // (4 * D), D), F32),
            pltpu.SemaphoreType.DMA((2,)),
        ],
        compiler_params=cparams(dimension_semantics=("arbitrary",)),
        name="mlp",
    )(h1.reshape(n_tok, D), p.reshape(n_tok, D_PLE), row(mlp_pre_g),
      w_up.astype(F32), w_down.astype(F32), row(mlp_post_g),
      w_ple_gate.astype(F32), row(b_ple_gate), w_ple.astype(F32), row(ple_post_g))
    return out.reshape(B, S, D)


def kernel(x, p, mix_pre_g, w_in, gmlp_ln_g, gmlp_ln_b, w_s, b_s, attn_out_g,
           gmlp_out_g, w_o, mix_post_g, mlp_pre_g, w_up, w_down, mlp_post_g,
           w_ple, w_ple_gate, b_ple_gate, ple_post_g):
    h = x
    for i in range(w_in.shape[0]):
        h = _layer(h, p[i], mix_pre_g[i], w_in[i], gmlp_ln_g[i], gmlp_ln_b[i],
                   w_s[i], b_s[i], attn_out_g[i], gmlp_out_g[i], w_o[i],
                   mix_post_g[i], mlp_pre_g[i], w_up[i], w_down[i], mlp_post_g[i],
                   w_ple[i], w_ple_gate[i], b_ple_gate[i], ple_post_g[i])
    return h
```

```python
import functools
import math

import jax
import jax.numpy as jnp
import numpy as np
from jax import lax
from jax.experimental import pallas as pl
from jax.experimental.pallas import tpu as pltpu

D_MODEL = 1024
D_PLE = 256
N_HEADS = 8
HEAD_DIM = 64
D_ATTN = N_HEADS * HEAD_DIM
N_GROUPS = 8
GROUP_DIM = 64
D_GMLP = N_GROUPS * GROUP_DIM
D_FF = 4 * D_MODEL
MOBA_BLOCK = 256
MOBA_TOPK = 3
GMLP_CHUNK = 128
ROPE_THETA = 500000.0
ROPE_DIM = HEAD_DIM // 4
ROPE_HALF = ROPE_DIM // 2
NORM_EPS = 1e-6
NEG_INF = -1e30

LANES = 128
HEADS_PER_LANE_TILE = LANES // HEAD_DIM
BF16_SUBLANES = 16
V_ROWS = HEAD_DIM + BF16_SUBLANES
BIG = 1e30
SCORE_LEAD = 2
VMEM_LIMIT_BYTES = 48 * 1024 * 1024
MLP_TOKENS = 512
FF_CHUNK = 1024
WEIGHT_STAGE_BYTES = 2 * 1024 * 1024

F32 = jnp.float32
BF16 = jnp.bfloat16


def _dot(a, b):
    return jnp.dot(a, b, preferred_element_type=F32)


def _dot_nt(a, b):
    return lax.dot_general(a, b, (((1,), (1,)), ((), ())), preferred_element_type=F32)


def _dot_tn(a, b):
    return lax.dot_general(a, b, (((0,), (0,)), ((), ())), preferred_element_type=F32)


def _rms_rows(x, g):
    ms = jnp.mean(x * x, axis=-1, keepdims=True)
    return x * lax.rsqrt(ms + NORM_EPS) * g


def _gelu_exact(x):
    return 0.5 * x * (1.0 + lax.erf(x * math.sqrt(0.5)))


def _mixer_kernel(x_ref, g_ref, win_hbm, qscale_ref, cost_ref, sint_ref,
                  cosk_ref, sak_ref, sbk_ref, lng_ref, lnb_ref,
                  ws_ref, bs_ref, ag_ref, gg_ref, wo_hbm, pg_ref,
                  o_ref, k_scr, vt_scr, km_scr, ug_scr, vn_scr,
                  win_ref, wo_ref, win_stage, wo_stage, sem, *scratch):
    sel_scr, qz_scr, m_scr, acc_scr, s_scr = (
        scratch[j * N_HEADS:(j + 1) * N_HEADS] for j in range(5))
    i = pl.program_id(1)
    nb = k_scr.shape[0]
    tq = MOBA_BLOCK

    @pl.when((pl.program_id(0) == 0) & (i == 0))
    def _():
        km_scr[...] = jnp.zeros(km_scr.shape, F32)
        _load_weight_bf16(win_hbm, win_ref, win_stage, sem, col_scale=qscale_ref[...])
        _load_weight_bf16(wo_hbm, wo_ref, wo_stage, sem)

    x = x_ref[0]
    hn = _rms_rows(x, g_ref[...]).astype(BF16)

    gv = _dot(hn, win_ref[:, 3 * D_ATTN + D_GMLP:])
    g2 = _gelu_exact(gv)
    mu = jnp.mean(g2, axis=-1, keepdims=True)
    xc = g2 - mu
    var = jnp.mean(xc * xc, axis=-1, keepdims=True)
    vn = xc * lax.rsqrt(var + NORM_EPS) * lng_ref[...] + lnb_ref[...]
    vn_scr[...] = vn.astype(BF16)

    qt = _dot(hn, win_ref[:, :D_ATTN]).T

    tok = _dot(hn, win_ref[:, D_ATTN:2 * D_ATTN])
    cos_k = cosk_ref[...]
    sa_k = sak_ref[...]
    sb_k = sbk_ref[...]
    kparts = []
    for j in range(D_ATTN // LANES):
        kb = tok[:, j * LANES:(j + 1) * LANES]
        up = pltpu.roll(kb, LANES - ROPE_HALF, axis=1)
        dn = pltpu.roll(kb, ROPE_HALF, axis=1)
        kparts.append(kb * cos_k + up * sa_k + dn * sb_k)
    k = jnp.concatenate(kparts, axis=1)
    k_scr[i] = k.astype(BF16)
    km_scr[pl.ds(i, 1), :] = jnp.mean(k, axis=0, keepdims=True)

    u = _dot(hn, win_ref[:, 3 * D_ATTN:3 * D_ATTN + D_GMLP])
    ug_scr[...] = _gelu_exact(u).astype(BF16)

    vt = _dot(hn, win_ref[:, 2 * D_ATTN:3 * D_ATTN]).T
    cos_t = cost_ref[...]
    sin_t = sint_ref[...]
    ones = jnp.ones((V_ROWS - HEAD_DIM, tq), F32)
    vparts = []
    for h in range(N_HEADS):
        vparts.append(vt[h * HEAD_DIM:(h + 1) * HEAD_DIM])
        vparts.append(ones)
    vt_scr[i] = jnp.concatenate(vparts, axis=0).astype(BF16)

    blk_row = lax.broadcasted_iota(jnp.int32, (nb, tq), 0)
    past = blk_row < i

    km = km_scr[...]
    km_hi = km.astype(BF16)
    r1 = km - km_hi.astype(F32)
    km_mid = r1.astype(BF16)
    km_lo = (r1 - km_mid.astype(F32)).astype(BF16)
    km3 = jnp.concatenate([km_hi, km_mid, km_lo], axis=0)

    def scores(blk, h):
        hp = h // HEADS_PER_LANE_TILE
        return _dot(k_scr[blk, :, hp * LANES:(hp + 1) * LANES], qz_scr[h][...])

    def v_rows(blk, h):
        return vt_scr[blk, h * V_ROWS:(h + 1) * V_ROWS, :]

    zeros_head = jnp.zeros((HEAD_DIM, tq), F32)
    for h in range(N_HEADS):
        hh = h % HEADS_PER_LANE_TILE
        r0 = h * HEAD_DIM
        x1 = qt[r0:r0 + ROPE_HALF]
        x2 = qt[r0 + ROPE_HALF:r0 + ROPE_DIM]
        q_h = [x1 * cos_t - x2 * sin_t, x2 * cos_t + x1 * sin_t, qt[r0 + ROPE_DIM:r0 + HEAD_DIM]]
        padded = [zeros_head] * hh + q_h + [zeros_head] * (HEADS_PER_LANE_TILE - 1 - hh)
        qz_scr[h][...] = jnp.concatenate(padded, axis=0).astype(BF16)

    @pl.when(i >= 0)
    def _():
        for h in range(N_HEADS):
            hp = h // HEADS_PER_LANE_TILE
            s_scr[h][...] = scores(0, h)

            g3 = _dot(km3[:, hp * LANES:(hp + 1) * LANES], qz_scr[h][...])
            gate = g3[0:nb] + g3[nb:2 * nb] + g3[2 * nb:3 * nb]
            gate = jnp.where(past, gate, NEG_INF)
            rank = jnp.zeros((nb, tq), jnp.int32)
            for m in range(nb):
                gm = gate[m:m + 1, :]
                beats = (gm > gate) | ((gm == gate) & (m < blk_row))
                rank = rank + beats.astype(jnp.int32)
            sel = past & (rank < MOBA_TOPK)
            sel_scr[h][...] = sel.astype(F32)
            m_scr[h][...] = jnp.full((1, tq), NEG_INF, F32)
            acc_scr[h][...] = jnp.zeros((V_ROWS, tq), F32)

    def past_blocks(kb0, n_blocks):
        units = [(kb0 + j, h) for j in range(n_blocks) for h in range(N_HEADS)]
        ahead = {}
        for u in range(-SCORE_LEAD, len(units)):
            if 0 <= u + SCORE_LEAD < len(units):
                blk, head = units[u + SCORE_LEAD]
                ahead[u + SCORE_LEAD] = scores(blk + 1, head)
            if u < 0:
                continue
            kb, h = units[u]
            s = s_scr[h][...]
            keep = sel_scr[h][pl.ds(kb, 1), :] > 0.5
            m_run = m_scr[h][...]
            m_new = jnp.where(keep, jnp.maximum(m_run, jnp.max(s, axis=0, keepdims=True)), m_run)
            m_sub = jnp.where(keep, m_new, BIG)
            alpha = jnp.exp2(m_run - m_new)
            pb = jnp.exp2((s - m_sub).astype(BF16))
            m_scr[h][...] = m_new
            acc_scr[h][...] = alpha * acc_scr[h][...] + _dot(v_rows(kb, h), pb)
            s_scr[h][...] = ahead.pop(u)

    def two_blocks(t, carry):
        past_blocks(2 * t, 2)
        return carry

    lax.fori_loop(0, i // 2, two_blocks, 0)

    @pl.when(i % 2 == 1)
    def _():
        past_blocks(i - 1, 1)

    kpos = lax.broadcasted_iota(jnp.int32, (MOBA_BLOCK, tq), 0)
    qpos = lax.broadcasted_iota(jnp.int32, (MOBA_BLOCK, tq), 1)
    causal = kpos <= qpos
    t_idx = lax.broadcasted_iota(jnp.int32, (GMLP_CHUNK, GMLP_CHUNK), 0)
    s_idx = lax.broadcasted_iota(jnp.int32, (GMLP_CHUNK, GMLP_CHUNK), 1)
    tril = s_idx <= t_idx
    lane = lax.broadcasted_iota(jnp.int32, (GMLP_CHUNK, LANES), 1)
    first_group = lane < GROUP_DIM
    n_lane_tiles = D_GMLP // LANES
    n_chunks = tq // GMLP_CHUNK
    assert n_lane_tiles * n_chunks == N_HEADS
    heads_out = []
    cols = [[None] * n_lane_tiles for _ in range(n_chunks)]
    for h in range(N_HEADS):
        c, gp = divmod(h, n_lane_tiles)
        vp = vn_scr[c * GMLP_CHUNK:(c + 1) * GMLP_CHUNK, gp * LANES:(gp + 1) * LANES]
        w_a = jnp.where(tril, ws_ref[2 * gp], 0.0).astype(BF16)
        w_b = jnp.where(tril, ws_ref[2 * gp + 1], 0.0).astype(BF16)
        cols[c][gp] = jnp.where(first_group, _dot(w_a, vp), _dot(w_b, vp))

        st = jnp.where(causal, s_scr[h][...], NEG_INF)
        m_run = m_scr[h][...]
        m_new = jnp.maximum(m_run, jnp.max(st, axis=0, keepdims=True))
        alpha = jnp.exp2(m_run - m_new)
        p = jnp.exp2((st - m_new).astype(BF16))
        acc = alpha * acc_scr[h][...] + _dot(v_rows(i, h), p)
        heads_out.append(acc[0:HEAD_DIM] / acc[HEAD_DIM:HEAD_DIM + 1])

    mixed = jnp.concatenate(
        [jnp.concatenate(cols[c], axis=1) + bs_ref[...] for c in range(n_chunks)], axis=0)
    gm_out = ug_scr[...].astype(F32) * mixed
    gm_n = _rms_rows(gm_out, gg_ref[...]).astype(BF16)
    y = _dot(gm_n, wo_ref[D_ATTN:, :])

    attn_t = jnp.concatenate(heads_out, axis=0)
    ms_a = jnp.mean(attn_t * attn_t, axis=0, keepdims=True)
    attn_n = (attn_t * lax.rsqrt(ms_a + NORM_EPS) * ag_ref[...]).astype(BF16)
    y = y + _dot_tn(attn_n, wo_ref[:D_ATTN, :])
    o_ref[0] = x_ref[0] + _rms_rows(y, pg_ref[...])


def _load_weight_bf16(src_hbm, dst_ref, stage_ref, sem_ref, col_scale=None):
    chunk = stage_ref.shape[1]
    n_rows = src_hbm.shape[0]
    chunk = min(chunk, n_rows)
    n_chunks = n_rows // chunk
    assert n_chunks * chunk == n_rows

    def copy(c, slot):
        return pltpu.make_async_copy(src_hbm.at[pl.ds(c * chunk, chunk), :],
                                     stage_ref.at[slot, pl.ds(0, chunk), :],
                                     sem_ref.at[slot])

    copy(0, 0).start()

    def body(c, carry):
        slot = lax.rem(c, 2)

        @pl.when(c + 1 < n_chunks)
        def _():
            copy(c + 1, 1 - slot).start()

        copy(c, slot).wait()
        rows = pl.ds(pl.multiple_of(c * chunk, chunk), chunk)
        w = stage_ref[slot, pl.ds(0, chunk), :]
        if col_scale is not None:
            w = w * col_scale
        dst_ref[rows, :] = w.astype(BF16)
        return carry

    lax.fori_loop(0, n_chunks, body, 0)


def _mlp_kernel(h_ref, p_ref, g1_ref, wup_hbm, wdn_hbm, g2_ref, wgate_hbm,
                bgate_ref, wple_hbm, g3_ref, o_ref,
                wup_ref, wdn_ref, wgate_ref, wple_ref, wide_stage, narrow_stage, sem):
    @pl.when(pl.program_id(0) == 0)
    def _():
        _load_weight_bf16(wup_hbm, wup_ref, wide_stage, sem)
        _load_weight_bf16(wdn_hbm, wdn_ref, narrow_stage, sem)
        _load_weight_bf16(wgate_hbm, wgate_ref, narrow_stage, sem)
        _load_weight_bf16(wple_hbm, wple_ref, narrow_stage, sem)

    h = h_ref[...]
    hn = _rms_rows(h, g1_ref[...]).astype(BF16)
    acc = jnp.zeros(h.shape, F32)
    for c in range(D_FF // FF_CHUNK):
        cs = slice(c * FF_CHUNK, (c + 1) * FF_CHUNK)
        up = jnp.maximum(_dot(hn, wup_ref[:, cs]), 0.0)
        acc = acc + _dot((up * up).astype(BF16), wdn_ref[cs, :])
    h = h + _rms_rows(acc, g2_ref[...])
    z = _dot(h.astype(BF16), wgate_ref[...]) + bgate_ref[...]
    gate = 1.0 / (1.0 + jnp.exp(-z))
    ple = _dot(p_ref[...].astype(BF16), wple_ref[...]) * gate
    o_ref[...] = h + _rms_rows(ple, g3_ref[...])


def _const_spec(shape):
    zeros = (0,) * len(shape)
    return pl.BlockSpec(shape, lambda *_: zeros, pipeline_mode=pl.Buffered(1))


def _rope_tables(seq):
    inv_freq = ROPE_THETA ** (-np.arange(ROPE_HALF, dtype=np.float64) / ROPE_HALF)
    ang = np.arange(seq, dtype=np.float64)[:, None] * inv_freq[None, :]
    cos, sin = np.cos(ang), np.sin(ang)
    cos_t, sin_t = cos.T, sin.T
    ones = np.ones((seq, HEAD_DIM - ROPE_DIM))
    zeros = np.zeros((seq, HEAD_DIM - ROPE_DIM))
    zh = np.zeros((seq, ROPE_HALF))
    cos_h = np.concatenate([cos, cos, ones], axis=1)
    sa_h = np.concatenate([-sin, zh, zeros], axis=1)
    sb_h = np.concatenate([zh, sin, zeros], axis=1)
    tile = lambda t: np.concatenate([t] * HEADS_PER_LANE_TILE, axis=1)
    tables = (cos_t, sin_t, tile(cos_h), tile(sa_h), tile(sb_h))
    return tuple(jnp.asarray(np.ascontiguousarray(t), dtype=F32) for t in tables)


def _layer(h, p, mix_pre_g, w_in, gmlp_ln_g, gmlp_ln_b, w_s, b_s, attn_out_g,
           gmlp_out_g, w_o, mix_post_g, mlp_pre_g, w_up, w_down, mlp_post_g,
           w_ple, w_ple_gate, b_ple_gate, ple_post_g):
    B, S, D = h.shape
    assert D == D_MODEL and S % MOBA_BLOCK == 0
    nb = S // MOBA_BLOCK
    T = MOBA_BLOCK
    row = lambda v: v.reshape(1, -1).astype(F32)

    d_in_proj = 3 * D_ATTN + 2 * D_GMLP
    qscale = np.ones((1, d_in_proj), np.float32)
    qscale[:, :D_ATTN] = HEAD_DIM ** -0.5 * math.log2(math.e)
    cos_t, sin_t, cos_k, sa_k, sb_k = _rope_tables(S)

    cparams = functools.partial(pltpu.CompilerParams, vmem_limit_bytes=VMEM_LIMIT_BYTES)

    bs_exp = jnp.repeat(b_s.T, GROUP_DIM, axis=1).astype(F32)

    h1 = pl.pallas_call(
        _mixer_kernel,
        grid=(B, nb),
        in_specs=[
            pl.BlockSpec((1, T, D), lambda b, i: (b, i, 0)),
            _const_spec((1, D)),
            pl.BlockSpec(memory_space=pl.ANY),
            _const_spec((1, d_in_proj)),
            pl.BlockSpec((ROPE_HALF, T), lambda b, i: (0, i)),
            pl.BlockSpec((ROPE_HALF, T), lambda b, i: (0, i)),
            pl.BlockSpec((T, LANES), lambda b, i: (i, 0)),
            pl.BlockSpec((T, LANES), lambda b, i: (i, 0)),
            pl.BlockSpec((T, LANES), lambda b, i: (i, 0)),
            _const_spec((1, D_GMLP)),
            _const_spec((1, D_GMLP)),
            _const_spec((N_GROUPS, GMLP_CHUNK, GMLP_CHUNK)),
            _const_spec((GMLP_CHUNK, D_GMLP)),
            _const_spec((D_ATTN, 1)),
            _const_spec((1, D_GMLP)),
            pl.BlockSpec(memory_space=pl.ANY),
            _const_spec((1, D)),
        ],
        out_specs=pl.BlockSpec((1, T, D), lambda b, i: (b, i, 0)),
        out_shape=jax.ShapeDtypeStruct((B, S, D), F32),
        scratch_shapes=([pltpu.VMEM((nb, T, D_ATTN), BF16),
                         pltpu.VMEM((nb, N_HEADS * V_ROWS, T), BF16),
                         pltpu.VMEM((nb, D_ATTN), F32),
                         pltpu.VMEM((T, D_GMLP), BF16),
                         pltpu.VMEM((T, D_GMLP), BF16),
                         pltpu.VMEM((D, d_in_proj), BF16),
                         pltpu.VMEM((D_ATTN + D_GMLP, D), BF16),
                         pltpu.VMEM((2, LANES, d_in_proj), F32),
                         pltpu.VMEM((2, WEIGHT_STAGE_BYTES // (4 * D), D), F32),
                         pltpu.SemaphoreType.DMA((2,))]
                        + [pltpu.VMEM((nb, T), F32)] * N_HEADS
                        + [pltpu.VMEM((LANES, T), BF16)] * N_HEADS
                        + [pltpu.VMEM((1, T), F32)] * N_HEADS
                        + [pltpu.VMEM((V_ROWS, T), F32)] * N_HEADS
                        + [pltpu.VMEM((MOBA_BLOCK, T), F32)] * N_HEADS),
        compiler_params=cparams(dimension_semantics=("arbitrary", "arbitrary")),
        name="mixer",
    )(h, row(mix_pre_g), w_in.astype(F32), jnp.asarray(qscale), cos_t, sin_t, cos_k, sa_k, sb_k,
      row(gmlp_ln_g), row(gmlp_ln_b), w_s.astype(F32), bs_exp,
      attn_out_g.reshape(-1, 1).astype(F32), row(gmlp_out_g), w_o.astype(F32),
      row(mix_post_g))

    n_tok = B * S
    tm = MLP_TOKENS
    out = pl.pallas_call(
        _mlp_kernel,
        grid=(n_tok // tm,),
        in_specs=[
            pl.BlockSpec((tm, D), lambda t: (t, 0)),
            pl.BlockSpec((tm, D_PLE), lambda t: (t, 0)),
            _const_spec((1, D)),
            pl.BlockSpec(memory_space=pl.ANY),
            pl.BlockSpec(memory_space=pl.ANY),
            _const_spec((1, D)),
            pl.BlockSpec(memory_space=pl.ANY),
            _const_spec((1, D)),
            pl.BlockSpec(memory_space=pl.ANY),
            _const_spec((1, D)),
        ],
        out_specs=pl.BlockSpec((tm, D), lambda t: (t, 0)),
        out_shape=jax.ShapeDtypeStruct((n_tok, D), F32),
        scratch_shapes=[
            pltpu.VMEM((D, D_FF), BF16),
            pltpu.VMEM((D_FF, D), BF16),
            pltpu.VMEM((D, D), BF16),
            pltpu.VMEM((D_PLE, D), BF16),
            pltpu.VMEM((2, WEIGHT_STAGE_BYTES // (4 * D_FF), D_FF), F32),
            pltpu.VMEM((2, WEIGHT_STAGE_BYTES // (4 * D), D), F32),
            pltpu.SemaphoreType.DMA((2,)),
        ],
        compiler_params=cparams(dimension_semantics=("arbitrary",)),
        name="mlp",
    )(h1.reshape(n_tok, D), p.reshape(n_tok, D_PLE), row(mlp_pre_g),
      w_up.astype(F32), w_down.astype(F32), row(mlp_post_g),
      w_ple_gate.astype(F32), row(b_ple_gate), w_ple.astype(F32), row(ple_post_g))
    return out.reshape(B, S, D)


def kernel(x, p, mix_pre_g, w_in, gmlp_ln_g, gmlp_ln_b, w_s, b_s, attn_out_g,
           gmlp_out_g, w_o, mix_post_g, mlp_pre_g, w_up, w_down, mlp_post_g,
           w_ple, w_ple_gate, b_ple_gate, ple_post_g):
    h = x
    for i in range(w_in.shape[0]):
        h = _layer(h, p[i], mix_pre_g[i], w_in[i], gmlp_ln_g[i], gmlp_ln_b[i],
                   w_s[i], b_s[i], attn_out_g[i], gmlp_out_g[i], w_o[i],
                   mix_post_g[i], mlp_pre_g[i], w_up[i], w_down[i], mlp_post_g[i],
                   w_ple[i], w_ple_gate[i], b_ple_gate[i], ple_post_g[i])
    return h
```

```python
import functools
import math

import jax
import jax.numpy as jnp
import numpy as np
from jax import lax
from jax.experimental import pallas as pl
from jax.experimental.pallas import tpu as pltpu

D_MODEL = 1024
D_PLE = 256
N_HEADS = 8
HEAD_DIM = 64
D_ATTN = N_HEADS * HEAD_DIM
N_GROUPS = 8
GROUP_DIM = 64
D_GMLP = N_GROUPS * GROUP_DIM
D_IN_PROJ = 3 * D_ATTN + 2 * D_GMLP
D_FF = 4 * D_MODEL
MOBA_BLOCK = 256
MOBA_TOPK = 3
GMLP_CHUNK = 128
ROPE_THETA = 500000.0
ROPE_DIM = HEAD_DIM // 4
ROPE_HALF = ROPE_DIM // 2
NORM_EPS = 1e-6
NEG_INF = -1e30

LANES = 128
HEADS_PER_LANE_TILE = LANES // HEAD_DIM
BF16_SUBLANES = 16
V_ROWS = HEAD_DIM + BF16_SUBLANES
BIG = 1e30
SCORE_LEAD = 2
VMEM_LIMIT_BYTES = 58 * 1024 * 1024
FF_CHUNK = 1024
STAGE_ROWS = 128

F32 = jnp.float32
BF16 = jnp.bfloat16


def _dot(a, b):
    return jnp.dot(a, b, preferred_element_type=F32)


def _dot_tn(a, b):
    return lax.dot_general(a, b, (((0,), (0,)), ((), ())), preferred_element_type=F32)


def _rms_rows(x, g):
    ms = jnp.mean(x * x, axis=-1, keepdims=True)
    return x * lax.rsqrt(ms + NORM_EPS) * g


def _gelu_exact(x):
    return 0.5 * x * (1.0 + lax.erf(x * math.sqrt(0.5)))


def _load_weight_bf16(src_hbm, dst_ref, stage_ref, sem_ref, col_scale=None):
    n_rows, width = src_hbm.shape
    chunk = min(stage_ref.shape[1], n_rows)
    n_chunks = n_rows // chunk
    assert n_chunks * chunk == n_rows and width <= stage_ref.shape[2]

    def copy(c, slot):
        return pltpu.make_async_copy(src_hbm.at[pl.ds(c * chunk, chunk), :],
                                     stage_ref.at[slot, pl.ds(0, chunk), pl.ds(0, width)],
                                     sem_ref.at[slot])

    copy(0, 0).start()

    def body(c, carry):
        slot = lax.rem(c, 2)

        @pl.when(c + 1 < n_chunks)
        def _():
            copy(c + 1, 1 - slot).start()

        copy(c, slot).wait()
        w = stage_ref[slot, pl.ds(0, chunk), pl.ds(0, width)]
        if col_scale is not None:
            w = w * col_scale
        rows = pl.ds(pl.multiple_of(c * chunk, chunk), chunk)
        dst_ref[rows, :] = w.astype(BF16)
        return carry

    lax.fori_loop(0, n_chunks, body, 0)


def _layer_kernel(x_ref, p_ref, g_ref, win_hbm, qscale_ref, cost_ref, sint_ref,
                  cosk_ref, sak_ref, sbk_ref, lng_ref, lnb_ref,
                  ws_ref, bs_ref, ag_ref, gg_ref, wo_hbm, pg_ref,
                  g1_ref, wup_hbm, wdn_hbm, g2_ref, wgate_hbm, bgate_ref, wple_hbm, g3_ref,
                  o_ref,
                  k_scr, vt_scr, km_scr, ug_scr, vn_scr, h1_scr,
                  win_ref, wo_ref, wup_ref, wdn_ref, wgate_ref, wple_ref, stage, sem,
                  *scratch):
    sel_scr, qz_scr, m_scr, acc_scr, s_scr = (
        scratch[j * N_HEADS:(j + 1) * N_HEADS] for j in range(5))
    nb = k_scr.shape[0]
    tq = MOBA_BLOCK
    n_tiles = pl.num_programs(0) - 1
    step = pl.program_id(0)
    live = step < n_tiles
    i = lax.rem(jnp.minimum(step, n_tiles - 1), nb)
    slot = lax.rem(step, 2)

    @pl.when(step == 0)
    def _():
        km_scr[...] = jnp.zeros(km_scr.shape, F32)
        h1_scr[...] = jnp.zeros(h1_scr.shape, F32)
        _load_weight_bf16(win_hbm, win_ref, stage, sem, col_scale=qscale_ref[...])
        _load_weight_bf16(wo_hbm, wo_ref, stage, sem)
        _load_weight_bf16(wup_hbm, wup_ref, stage, sem)
        _load_weight_bf16(wdn_hbm, wdn_ref, stage, sem)
        _load_weight_bf16(wgate_hbm, wgate_ref, stage, sem)
        _load_weight_bf16(wple_hbm, wple_ref, stage, sem)

    def scores(blk, h):
        hp = h // HEADS_PER_LANE_TILE
        return _dot(k_scr[blk, :, hp * LANES:(hp + 1) * LANES], qz_scr[h][...])

    def v_rows(blk, h):
        return vt_scr[blk, h * V_ROWS:(h + 1) * V_ROWS, :]

    @pl.when(live)
    def _():
        x = x_ref[0]
        hn = _rms_rows(x, g_ref[...]).astype(BF16)

        gv = _dot(hn, win_ref[:, 3 * D_ATTN + D_GMLP:])
        g2 = _gelu_exact(gv)
        mu = jnp.mean(g2, axis=-1, keepdims=True)
        xc = g2 - mu
        var = jnp.mean(xc * xc, axis=-1, keepdims=True)
        vn = xc * lax.rsqrt(var + NORM_EPS) * lng_ref[...] + lnb_ref[...]
        vn_scr[...] = vn.astype(BF16)

        qt = _dot(hn, win_ref[:, :D_ATTN]).T

        tok = _dot(hn, win_ref[:, D_ATTN:2 * D_ATTN])
        cos_k = cosk_ref[...]
        sa_k = sak_ref[...]
        sb_k = sbk_ref[...]
        kparts = []
        for j in range(D_ATTN // LANES):
            kb = tok[:, j * LANES:(j + 1) * LANES]
            up = pltpu.roll(kb, LANES - ROPE_HALF, axis=1)
            dn = pltpu.roll(kb, ROPE_HALF, axis=1)
            kparts.append(kb * cos_k + up * sa_k + dn * sb_k)
        k = jnp.concatenate(kparts, axis=1)
        k_scr[i] = k.astype(BF16)
        km_scr[pl.ds(i, 1), :] = jnp.mean(k, axis=0, keepdims=True)

        u = _dot(hn, win_ref[:, 3 * D_ATTN:3 * D_ATTN + D_GMLP])
        ug_scr[...] = _gelu_exact(u).astype(BF16)

        vt = _dot(hn, win_ref[:, 2 * D_ATTN:3 * D_ATTN]).T
        ones = jnp.ones((V_ROWS - HEAD_DIM, tq), F32)
        vparts = []
        for h in range(N_HEADS):
            vparts.append(vt[h * HEAD_DIM:(h + 1) * HEAD_DIM])
            vparts.append(ones)
        vt_scr[i] = jnp.concatenate(vparts, axis=0).astype(BF16)

        cos_t = cost_ref[...]
        sin_t = sint_ref[...]
        zeros_head = jnp.zeros((HEAD_DIM, tq), F32)
        for h in range(N_HEADS):
            hh = h % HEADS_PER_LANE_TILE
            r0 = h * HEAD_DIM
            x1 = qt[r0:r0 + ROPE_HALF]
            x2 = qt[r0 + ROPE_HALF:r0 + ROPE_DIM]
            q_h = [x1 * cos_t - x2 * sin_t, x2 * cos_t + x1 * sin_t, qt[r0 + ROPE_DIM:r0 + HEAD_DIM]]
            padded = [zeros_head] * hh + q_h + [zeros_head] * (HEADS_PER_LANE_TILE - 1 - hh)
            qz_scr[h][...] = jnp.concatenate(padded, axis=0).astype(BF16)

    @pl.when(live)
    def _():
        blk_row = lax.broadcasted_iota(jnp.int32, (nb, tq), 0)
        past = blk_row < i
        km = km_scr[...]
        km_hi = km.astype(BF16)
        r1 = km - km_hi.astype(F32)
        km_mid = r1.astype(BF16)
        km_lo = (r1 - km_mid.astype(F32)).astype(BF16)
        km3 = jnp.concatenate([km_hi, km_mid, km_lo], axis=0)
        for h in range(N_HEADS):
            hp = h // HEADS_PER_LANE_TILE
            s_scr[h][...] = scores(0, h)

            g3 = _dot(km3[:, hp * LANES:(hp + 1) * LANES], qz_scr[h][...])
            gate = g3[0:nb] + g3[nb:2 * nb] + g3[2 * nb:3 * nb]
            gate = jnp.where(past, gate, NEG_INF)
            rank = jnp.zeros((nb, tq), jnp.int32)
            for m in range(nb):
                gm = gate[m:m + 1, :]
                beats = (gm > gate) | ((gm == gate) & (m < blk_row))
                rank = rank + beats.astype(jnp.int32)
            sel = past & (rank < MOBA_TOPK)
            sel_scr[h][...] = sel.astype(F32)
            m_scr[h][...] = jnp.full((1, tq), NEG_INF, F32)
            acc_scr[h][...] = jnp.zeros((V_ROWS, tq), F32)

    def past_blocks(kb0, n_blocks):
        units = [(kb0 + j, h) for j in range(n_blocks) for h in range(N_HEADS)]
        ahead = {}
        for u in range(-SCORE_LEAD, len(units)):
            if 0 <= u + SCORE_LEAD < len(units):
                blk, head = units[u + SCORE_LEAD]
                ahead[u + SCORE_LEAD] = scores(blk + 1, head)
            if u < 0:
                continue
            kb, h = units[u]
            s = s_scr[h][...]
            keep = sel_scr[h][pl.ds(kb, 1), :] > 0.5
            m_run = m_scr[h][...]
            m_new = jnp.where(keep, jnp.maximum(m_run, jnp.max(s, axis=0, keepdims=True)), m_run)
            m_sub = jnp.where(keep, m_new, BIG)
            alpha = jnp.exp2(m_run - m_new)
            pb = jnp.exp2(s - m_sub).astype(BF16)
            m_scr[h][...] = m_new
            acc_scr[h][...] = alpha * acc_scr[h][...] + _dot(v_rows(kb, h), pb)
            s_scr[h][...] = ahead.pop(u)

    def two_blocks(t, carry):
        past_blocks(2 * t, 2)
        return carry

    n_past = jnp.where(live, i, 0)
    lax.fori_loop(0, n_past // 2, two_blocks, 0)

    @pl.when(lax.rem(n_past, 2) == 1)
    def _():
        past_blocks(n_past - 1, 1)

    mlp = {}

    def mlp_begin():
        mlp['h'] = h1_scr[1 - slot]
        mlp['hn'] = _rms_rows(mlp['h'], g1_ref[...]).astype(BF16)
        mlp['acc'] = None

    def mlp_up(c):
        up = jnp.maximum(_dot(mlp['hn'], wup_ref[:, c * FF_CHUNK:(c + 1) * FF_CHUNK]), 0.0)
        mlp['up'] = (up * up).astype(BF16)

    def mlp_down(c):
        d = _dot(mlp['up'], wdn_ref[c * FF_CHUNK:(c + 1) * FF_CHUNK, :])
        mlp['acc'] = d if mlp['acc'] is None else mlp['acc'] + d

    def mlp_gate():
        mlp['h2'] = mlp['h'] + _rms_rows(mlp['acc'], g2_ref[...])
        mlp['z'] = _dot(mlp['h2'].astype(BF16), wgate_ref[...]) + bgate_ref[...]

    def mlp_ple():
        mlp['ple'] = _dot(p_ref[0].astype(BF16), wple_ref[...])

    def mlp_end():
        gate = 1.0 / (1.0 + jnp.exp(-mlp['z']))
        o_ref[0] = mlp['h2'] + _rms_rows(mlp['ple'] * gate, g3_ref[...])

    n_ff = D_FF // FF_CHUNK
    assert 2 * n_ff == N_HEADS
    pieces_before_head = [[mlp_begin]] + [[] for _ in range(N_HEADS - 1)]
    for c in range(n_ff):
        pieces_before_head[2 * c].append(functools.partial(mlp_up, c))
        pieces_before_head[2 * c + 1].append(functools.partial(mlp_down, c))

    kpos = lax.broadcasted_iota(jnp.int32, (MOBA_BLOCK, tq), 0)
    qpos = lax.broadcasted_iota(jnp.int32, (MOBA_BLOCK, tq), 1)
    causal = kpos <= qpos
    t_idx = lax.broadcasted_iota(jnp.int32, (GMLP_CHUNK, GMLP_CHUNK), 0)
    s_idx = lax.broadcasted_iota(jnp.int32, (GMLP_CHUNK, GMLP_CHUNK), 1)
    tril = s_idx <= t_idx
    lane = lax.broadcasted_iota(jnp.int32, (GMLP_CHUNK, LANES), 1)
    first_group = lane < GROUP_DIM
    n_lane_tiles = D_GMLP // LANES
    n_chunks = tq // GMLP_CHUNK
    assert n_lane_tiles * n_chunks == N_HEADS
    heads_out = []
    cols = [[None] * n_lane_tiles for _ in range(n_chunks)]
    for h in range(N_HEADS):
        for piece in pieces_before_head[h]:
            piece()
        c, gp = divmod(h, n_lane_tiles)
        vp = vn_scr[c * GMLP_CHUNK:(c + 1) * GMLP_CHUNK, gp * LANES:(gp + 1) * LANES]
        w_a = jnp.where(tril, ws_ref[2 * gp], 0.0).astype(BF16)
        w_b = jnp.where(tril, ws_ref[2 * gp + 1], 0.0).astype(BF16)
        cols[c][gp] = jnp.where(first_group, _dot(w_a, vp), _dot(w_b, vp))

        st = jnp.where(causal, s_scr[h][...], NEG_INF)
        m_run = m_scr[h][...]
        m_new = jnp.maximum(m_run, jnp.max(st, axis=0, keepdims=True))
        alpha = jnp.exp2(m_run - m_new)
        p = jnp.exp2(st - m_new).astype(BF16)
        acc = alpha * acc_scr[h][...] + _dot(v_rows(i, h), p)
        heads_out.append(acc[0:HEAD_DIM] / acc[HEAD_DIM:HEAD_DIM + 1])

    mlp_ple()
    mixed = jnp.concatenate(
        [jnp.concatenate(cols[c], axis=1) + bs_ref[...] for c in range(n_chunks)], axis=0)
    gm_out = ug_scr[...].astype(F32) * mixed
    gm_n = _rms_rows(gm_out, gg_ref[...]).astype(BF16)
    y = _dot(gm_n, wo_ref[D_ATTN:, :])
    mlp_gate()

    attn_t = jnp.concatenate(heads_out, axis=0)
    ms_a = jnp.mean(attn_t * attn_t, axis=0, keepdims=True)
    attn_n = (attn_t * lax.rsqrt(ms_a + NORM_EPS) * ag_ref[...]).astype(BF16)
    y = y + _dot_tn(attn_n, wo_ref[:D_ATTN, :])
    h1_scr[slot] = x_ref[0] + _rms_rows(y, pg_ref[...])
    mlp_end()


def _const_spec(shape):
    zeros = (0,) * len(shape)
    return pl.BlockSpec(shape, lambda *_: zeros, pipeline_mode=pl.Buffered(1))


def _rope_tables(seq):
    inv_freq = ROPE_THETA ** (-np.arange(ROPE_HALF, dtype=np.float64) / ROPE_HALF)
    ang = np.arange(seq, dtype=np.float64)[:, None] * inv_freq[None, :]
    cos, sin = np.cos(ang), np.sin(ang)
    cos_t, sin_t = cos.T, sin.T
    ones = np.ones((seq, HEAD_DIM - ROPE_DIM))
    zeros = np.zeros((seq, HEAD_DIM - ROPE_DIM))
    zh = np.zeros((seq, ROPE_HALF))
    cos_h = np.concatenate([cos, cos, ones], axis=1)
    sa_h = np.concatenate([-sin, zh, zeros], axis=1)
    sb_h = np.concatenate([zh, sin, zeros], axis=1)
    tile = lambda t: np.concatenate([t] * HEADS_PER_LANE_TILE, axis=1)
    tables = (cos_t, sin_t, tile(cos_h), tile(sa_h), tile(sb_h))
    return tuple(jnp.asarray(np.ascontiguousarray(t), dtype=F32) for t in tables)


def _layer(h, p, mix_pre_g, w_in, gmlp_ln_g, gmlp_ln_b, w_s, b_s, attn_out_g,
           gmlp_out_g, w_o, mix_post_g, mlp_pre_g, w_up, w_down, mlp_post_g,
           w_ple, w_ple_gate, b_ple_gate, ple_post_g):
    B, S, D = h.shape
    assert D == D_MODEL and S % MOBA_BLOCK == 0
    nb = S // MOBA_BLOCK
    T = MOBA_BLOCK
    n_tiles = B * nb
    row = lambda v: v.reshape(1, -1).astype(F32)

    qscale = np.ones((1, D_IN_PROJ), np.float32)
    qscale[:, :D_ATTN] = HEAD_DIM ** -0.5 * math.log2(math.e)
    cos_t, sin_t, cos_k, sa_k, sb_k = _rope_tables(S)
    bs_exp = jnp.repeat(b_s.T, GROUP_DIM, axis=1).astype(F32)

    def cur(s):
        t = jnp.minimum(s, n_tiles - 1)
        return t // nb, t % nb

    def prev(s):
        t = jnp.maximum(s - 1, 0)
        return t // nb, t % nb

    hbm = pl.BlockSpec(memory_space=pl.ANY)
    out = pl.pallas_call(
        _layer_kernel,
        grid=(n_tiles + 1,),
        in_specs=[
            pl.BlockSpec((1, T, D), lambda s: (*cur(s), 0)),
            pl.BlockSpec((1, T, D_PLE), lambda s: (*prev(s), 0)),
            _const_spec((1, D)),
            hbm,
            _const_spec((1, D_IN_PROJ)),
            pl.BlockSpec((ROPE_HALF, T), lambda s: (0, cur(s)[1])),
            pl.BlockSpec((ROPE_HALF, T), lambda s: (0, cur(s)[1])),
            pl.BlockSpec((T, LANES), lambda s: (cur(s)[1], 0)),
            pl.BlockSpec((T, LANES), lambda s: (cur(s)[1], 0)),
            pl.BlockSpec((T, LANES), lambda s: (cur(s)[1], 0)),
            _const_spec((1, D_GMLP)),
            _const_spec((1, D_GMLP)),
            _const_spec((N_GROUPS, GMLP_CHUNK, GMLP_CHUNK)),
            _const_spec((GMLP_CHUNK, D_GMLP)),
            _const_spec((D_ATTN, 1)),
            _const_spec((1, D_GMLP)),
            hbm,
            _const_spec((1, D)),
            _const_spec((1, D)),
            hbm,
            hbm,
            _const_spec((1, D)),
            hbm,
            _const_spec((1, D)),
            hbm,
            _const_spec((1, D)),
        ],
        out_specs=pl.BlockSpec((1, T, D), lambda s: (*prev(s), 0)),
        out_shape=jax.ShapeDtypeStruct((B, S, D), F32),
        scratch_shapes=([pltpu.VMEM((nb, T, D_ATTN), BF16),
                         pltpu.VMEM((nb, N_HEADS * V_ROWS, T), BF16),
                         pltpu.VMEM((nb, D_ATTN), F32),
                         pltpu.VMEM((T, D_GMLP), BF16),
                         pltpu.VMEM((T, D_GMLP), BF16),
                         pltpu.VMEM((2, T, D), F32),
                         pltpu.VMEM((D, D_IN_PROJ), BF16),
                         pltpu.VMEM((D_ATTN + D_GMLP, D), BF16),
                         pltpu.VMEM((D, D_FF), BF16),
                         pltpu.VMEM((D_FF, D), BF16),
                         pltpu.VMEM((D, D), BF16),
                         pltpu.VMEM((D_PLE, D), BF16),
                         pltpu.VMEM((2, STAGE_ROWS, D_FF), F32),
                         pltpu.SemaphoreType.DMA((2,))]
                        + [pltpu.VMEM((nb, T), F32)] * N_HEADS
                        + [pltpu.VMEM((LANES, T), BF16)] * N_HEADS
                        + [pltpu.VMEM((1, T), F32)] * N_HEADS
                        + [pltpu.VMEM((V_ROWS, T), F32)] * N_HEADS
                        + [pltpu.VMEM((MOBA_BLOCK, T), F32)] * N_HEADS),
        compiler_params=pltpu.CompilerParams(dimension_semantics=("arbitrary",),
                                             vmem_limit_bytes=VMEM_LIMIT_BYTES),
        name="layer",
    )(h, p, row(mix_pre_g), w_in.astype(F32), jnp.asarray(qscale), cos_t, sin_t, cos_k, sa_k, sb_k,
      row(gmlp_ln_g), row(gmlp_ln_b), w_s.astype(F32), bs_exp,
      attn_out_g.reshape(-1, 1).astype(F32), row(gmlp_out_g), w_o.astype(F32), row(mix_post_g),
      row(mlp_pre_g), w_up.astype(F32), w_down.astype(F32), row(mlp_post_g),
      w_ple_gate.astype(F32), row(b_ple_gate), w_ple.astype(F32), row(ple_post_g))
    return out


def kernel(x, p, mix_pre_g, w_in, gmlp_ln_g, gmlp_ln_b, w_s, b_s, attn_out_g,
           gmlp_out_g, w_o, mix_post_g, mlp_pre_g, w_up, w_down, mlp_post_g,
           w_ple, w_ple_gate, b_ple_gate, ple_post_g):
    h = x
    for i in range(w_in.shape[0]):
        h = _layer(h, p[i], mix_pre_g[i], w_in[i], gmlp_ln_g[i], gmlp_ln_b[i],
                   w_s[i], b_s[i], attn_out_g[i], gmlp_out_g[i], w_o[i],
                   mix_post_g[i], mlp_pre_g[i], w_up[i], w_down[i], mlp_post_g[i],
                   w_ple[i], w_ple_gate[i], b_ple_gate[i], ple_post_g[i])
    return h
```

```python
import functools
import math

import jax
import jax.numpy as jnp
import numpy as np
from jax import lax
from jax.experimental import pallas as pl
from jax.experimental.pallas import tpu as pltpu

D_MODEL = 1024
D_PLE = 256
N_HEADS = 8
HEAD_DIM = 64
D_ATTN = N_HEADS * HEAD_DIM
N_GROUPS = 8
GROUP_DIM = 64
D_GMLP = N_GROUPS * GROUP_DIM
D_IN_PROJ = 3 * D_ATTN + 2 * D_GMLP
D_FF = 4 * D_MODEL
MOBA_BLOCK = 256
MOBA_TOPK = 3
GMLP_CHUNK = 128
ROPE_THETA = 500000.0
ROPE_DIM = HEAD_DIM // 4
ROPE_HALF = ROPE_DIM // 2
NORM_EPS = 1e-6
NEG_INF = -1e30

LANES = 128
HEADS_PER_LANE_TILE = LANES // HEAD_DIM
BF16_SUBLANES = 16
V_ROWS = HEAD_DIM + BF16_SUBLANES
BIG = 1e30
SCORE_LEAD = 2
VMEM_LIMIT_BYTES = 58 * 1024 * 1024
FF_CHUNK = 2048
STAGE_ROWS = 128

F32 = jnp.float32
BF16 = jnp.bfloat16


def _dot(a, b):
    return jnp.dot(a, b, preferred_element_type=F32)


def _dot_tn(a, b):
    return lax.dot_general(a, b, (((0,), (0,)), ((), ())), preferred_element_type=F32)


def _rms_rows(x, g):
    ms = jnp.mean(x * x, axis=-1, keepdims=True)
    return x * lax.rsqrt(ms + NORM_EPS) * g


def _gelu_exact(x):
    return 0.5 * x * (1.0 + lax.erf(x * math.sqrt(0.5)))


def _load_weight_bf16(src_hbm, dst_ref, stage_ref, sem_ref, col_scale=None):
    n_rows, width = src_hbm.shape
    chunk = min(stage_ref.shape[1], n_rows)
    n_chunks = n_rows // chunk
    assert n_chunks * chunk == n_rows and width <= stage_ref.shape[2]

    def copy(c, slot):
        return pltpu.make_async_copy(src_hbm.at[pl.ds(c * chunk, chunk), :],
                                     stage_ref.at[slot, pl.ds(0, chunk), pl.ds(0, width)],
                                     sem_ref.at[slot])

    copy(0, 0).start()

    def body(c, carry):
        slot = lax.rem(c, 2)

        @pl.when(c + 1 < n_chunks)
        def _():
            copy(c + 1, 1 - slot).start()

        copy(c, slot).wait()
        w = stage_ref[slot, pl.ds(0, chunk), pl.ds(0, width)]
        if col_scale is not None:
            w = w * col_scale
        rows = pl.ds(pl.multiple_of(c * chunk, chunk), chunk)
        dst_ref[rows, :] = w.astype(BF16)
        return carry

    lax.fori_loop(0, n_chunks, body, 0)


def _layer_kernel(x_ref, p_ref, g_ref, win_hbm, qscale_ref, cost_ref, sint_ref,
                  cosk_ref, sak_ref, sbk_ref, lng_ref, lnb_ref,
                  ws_ref, bs_ref, ag_ref, gg_ref, wo_hbm, pg_ref,
                  g1_ref, wup_hbm, wdn_hbm, g2_ref, wgate_hbm, bgate_ref, wple_hbm, g3_ref,
                  o_ref,
                  k_scr, vt_scr, km_scr, ug_scr, vn_scr, h1_scr,
                  win_ref, wo_ref, wup_ref, wdn_ref, wgate_ref, wple_ref, stage, sem,
                  *scratch):
    sel_scr, qz_scr, m_scr, acc_scr, s_scr = (
        scratch[j * N_HEADS:(j + 1) * N_HEADS] for j in range(5))
    nb = k_scr.shape[0]
    tq = MOBA_BLOCK
    n_tiles = pl.num_programs(0) - 1
    step = pl.program_id(0)
    live = step < n_tiles
    i = lax.rem(jnp.minimum(step, n_tiles - 1), nb)
    slot = lax.rem(step, 2)

    @pl.when(step == 0)
    def _():
        km_scr[...] = jnp.zeros(km_scr.shape, F32)
        h1_scr[...] = jnp.zeros(h1_scr.shape, F32)
        _load_weight_bf16(win_hbm, win_ref, stage, sem, col_scale=qscale_ref[...])
        _load_weight_bf16(wo_hbm, wo_ref, stage, sem)
        _load_weight_bf16(wup_hbm, wup_ref, stage, sem)
        _load_weight_bf16(wdn_hbm, wdn_ref, stage, sem)
        _load_weight_bf16(wgate_hbm, wgate_ref, stage, sem)
        _load_weight_bf16(wple_hbm, wple_ref, stage, sem)

    def scores(blk, h):
        hp = h // HEADS_PER_LANE_TILE
        return _dot(k_scr[blk, :, hp * LANES:(hp + 1) * LANES], qz_scr[h][...])

    def v_rows(blk, h):
        return vt_scr[blk, h * V_ROWS:(h + 1) * V_ROWS, :]

    @pl.when(live)
    def _():
        x = x_ref[0]
        hn = _rms_rows(x, g_ref[...]).astype(BF16)

        gv = _dot(hn, win_ref[:, 3 * D_ATTN + D_GMLP:])
        g2 = _gelu_exact(gv)
        mu = jnp.mean(g2, axis=-1, keepdims=True)
        xc = g2 - mu
        var = jnp.mean(xc * xc, axis=-1, keepdims=True)
        vn = xc * lax.rsqrt(var + NORM_EPS) * lng_ref[...] + lnb_ref[...]
        vn_scr[...] = vn.astype(BF16)

        qt = _dot(hn, win_ref[:, :D_ATTN]).T

        tok = _dot(hn, win_ref[:, D_ATTN:2 * D_ATTN])
        cos_k = cosk_ref[...]
        sa_k = sak_ref[...]
        sb_k = sbk_ref[...]
        kparts = []
        for j in range(D_ATTN // LANES):
            kb = tok[:, j * LANES:(j + 1) * LANES]
            up = pltpu.roll(kb, LANES - ROPE_HALF, axis=1)
            dn = pltpu.roll(kb, ROPE_HALF, axis=1)
            kparts.append(kb * cos_k + up * sa_k + dn * sb_k)
        k = jnp.concatenate(kparts, axis=1)
        k_scr[i] = k.astype(BF16)
        km_scr[pl.ds(i, 1), :] = jnp.mean(k, axis=0, keepdims=True)

        u = _dot(hn, win_ref[:, 3 * D_ATTN:3 * D_ATTN + D_GMLP])
        ug_scr[...] = _gelu_exact(u).astype(BF16)

        vt = _dot(hn, win_ref[:, 2 * D_ATTN:3 * D_ATTN]).T
        ones = jnp.ones((V_ROWS - HEAD_DIM, tq), F32)
        vparts = []
        for h in range(N_HEADS):
            vparts.append(vt[h * HEAD_DIM:(h + 1) * HEAD_DIM])
            vparts.append(ones)
        vt_scr[i] = jnp.concatenate(vparts, axis=0).astype(BF16)

        cos_t = cost_ref[...]
        sin_t = sint_ref[...]
        zeros_head = jnp.zeros((HEAD_DIM, tq), F32)
        for h in range(N_HEADS):
            hh = h % HEADS_PER_LANE_TILE
            r0 = h * HEAD_DIM
            x1 = qt[r0:r0 + ROPE_HALF]
            x2 = qt[r0 + ROPE_HALF:r0 + ROPE_DIM]
            q_h = [x1 * cos_t - x2 * sin_t, x2 * cos_t + x1 * sin_t, qt[r0 + ROPE_DIM:r0 + HEAD_DIM]]
            padded = [zeros_head] * hh + q_h + [zeros_head] * (HEADS_PER_LANE_TILE - 1 - hh)
            qz_scr[h][...] = jnp.concatenate(padded, axis=0).astype(BF16)

    @pl.when(live)
    def _():
        blk_row = lax.broadcasted_iota(jnp.int32, (nb, tq), 0)
        past = blk_row < i
        km = km_scr[...]
        km_hi = km.astype(BF16)
        r1 = km - km_hi.astype(F32)
        km_mid = r1.astype(BF16)
        km_lo = (r1 - km_mid.astype(F32)).astype(BF16)
        km3 = jnp.concatenate([km_hi, km_mid, km_lo], axis=0)
        for h in range(N_HEADS):
            hp = h // HEADS_PER_LANE_TILE
            s_scr[h][...] = scores(0, h)

            g3 = _dot(km3[:, hp * LANES:(hp + 1) * LANES], qz_scr[h][...])
            gate = g3[0:nb] + g3[nb:2 * nb] + g3[2 * nb:3 * nb]
            gate = jnp.where(past, gate, NEG_INF)
            rank = jnp.zeros((nb, tq), jnp.int32)
            for m in range(nb):
                gm = gate[m:m + 1, :]
                beats = (gm > gate) | ((gm == gate) & (m < blk_row))
                rank = rank + beats.astype(jnp.int32)
            sel = past & (rank < MOBA_TOPK)
            sel_scr[h][...] = sel.astype(F32)
            m_scr[h][...] = jnp.full((1, tq), NEG_INF, F32)
            acc_scr[h][...] = jnp.zeros((V_ROWS, tq), F32)

    def past_blocks(kb0, n_blocks):
        units = [(kb0 + j, h) for j in range(n_blocks) for h in range(N_HEADS)]
        ahead = {}
        for u in range(-SCORE_LEAD, len(units)):
            if 0 <= u + SCORE_LEAD < len(units):
                blk, head = units[u + SCORE_LEAD]
                ahead[u + SCORE_LEAD] = scores(blk + 1, head)
            if u < 0:
                continue
            kb, h = units[u]
            s = s_scr[h][...]
            keep = sel_scr[h][pl.ds(kb, 1), :] > 0.5
            m_run = m_scr[h][...]
            m_new = jnp.where(keep, jnp.maximum(m_run, jnp.max(s, axis=0, keepdims=True)), m_run)
            m_sub = jnp.where(keep, m_new, BIG)
            alpha = jnp.exp2(m_run - m_new)
            pb = jnp.exp2(s - m_sub).astype(BF16)
            m_scr[h][...] = m_new
            acc_scr[h][...] = alpha * acc_scr[h][...] + _dot(v_rows(kb, h), pb)
            s_scr[h][...] = ahead.pop(u)

    def two_blocks(t, carry):
        past_blocks(2 * t, 2)
        return carry

    n_past = jnp.where(live, i, 0)
    lax.fori_loop(0, n_past // 2, two_blocks, 0)

    @pl.when(lax.rem(n_past, 2) == 1)
    def _():
        past_blocks(n_past - 1, 1)

    mlp = {}

    def mlp_begin():
        mlp['h'] = h1_scr[1 - slot]
        mlp['hn'] = _rms_rows(mlp['h'], g1_ref[...]).astype(BF16)
        mlp['acc'] = None

    def mlp_up(c):
        up = jnp.maximum(_dot(mlp['hn'], wup_ref[:, c * FF_CHUNK:(c + 1) * FF_CHUNK]), 0.0)
        mlp['up'] = (up * up).astype(BF16)

    def mlp_down(c):
        d = _dot(mlp['up'], wdn_ref[c * FF_CHUNK:(c + 1) * FF_CHUNK, :])
        mlp['acc'] = d if mlp['acc'] is None else mlp['acc'] + d

    def mlp_gate():
        mlp['h2'] = mlp['h'] + _rms_rows(mlp['acc'], g2_ref[...])
        mlp['z'] = _dot(mlp['h2'].astype(BF16), wgate_ref[...]) + bgate_ref[...]

    def mlp_ple():
        mlp['ple'] = _dot(p_ref[0].astype(BF16), wple_ref[...])

    def mlp_end():
        gate = 1.0 / (1.0 + jnp.exp(-mlp['z']))
        o_ref[0] = mlp['h2'] + _rms_rows(mlp['ple'] * gate, g3_ref[...])

    n_ff = D_FF // FF_CHUNK
    heads_per_piece = N_HEADS // (2 * n_ff)
    assert heads_per_piece * 2 * n_ff == N_HEADS
    pieces_before_head = [[mlp_begin]] + [[] for _ in range(N_HEADS - 1)]
    for c in range(n_ff):
        pieces_before_head[2 * c * heads_per_piece].append(functools.partial(mlp_up, c))
        pieces_before_head[(2 * c + 1) * heads_per_piece].append(functools.partial(mlp_down, c))

    kpos = lax.broadcasted_iota(jnp.int32, (MOBA_BLOCK, tq), 0)
    qpos = lax.broadcasted_iota(jnp.int32, (MOBA_BLOCK, tq), 1)
    causal = kpos <= qpos
    t_idx = lax.broadcasted_iota(jnp.int32, (GMLP_CHUNK, GMLP_CHUNK), 0)
    s_idx = lax.broadcasted_iota(jnp.int32, (GMLP_CHUNK, GMLP_CHUNK), 1)
    tril = s_idx <= t_idx
    lane = lax.broadcasted_iota(jnp.int32, (GMLP_CHUNK, LANES), 1)
    first_group = lane < GROUP_DIM
    n_lane_tiles = D_GMLP // LANES
    n_chunks = tq // GMLP_CHUNK
    assert n_lane_tiles * n_chunks == N_HEADS
    heads_out = []
    cols = [[None] * n_lane_tiles for _ in range(n_chunks)]
    for h in range(N_HEADS):
        for piece in pieces_before_head[h]:
            piece()
        c, gp = divmod(h, n_lane_tiles)
        vp = vn_scr[c * GMLP_CHUNK:(c + 1) * GMLP_CHUNK, gp * LANES:(gp + 1) * LANES]
        w_a = jnp.where(tril, ws_ref[2 * gp], 0.0).astype(BF16)
        w_b = jnp.where(tril, ws_ref[2 * gp + 1], 0.0).astype(BF16)
        cols[c][gp] = jnp.where(first_group, _dot(w_a, vp), _dot(w_b, vp))

        st = jnp.where(causal, s_scr[h][...], NEG_INF)
        m_run = m_scr[h][...]
        m_new = jnp.maximum(m_run, jnp.max(st, axis=0, keepdims=True))
        alpha = jnp.exp2(m_run - m_new)
        p = jnp.exp2(st - m_new).astype(BF16)
        acc = alpha * acc_scr[h][...] + _dot(v_rows(i, h), p)
        heads_out.append(acc[0:HEAD_DIM] / acc[HEAD_DIM:HEAD_DIM + 1])

    mlp_ple()
    mixed = jnp.concatenate(
        [jnp.concatenate(cols[c], axis=1) + bs_ref[...] for c in range(n_chunks)], axis=0)
    gm_out = ug_scr[...].astype(F32) * mixed
    gm_n = _rms_rows(gm_out, gg_ref[...]).astype(BF16)
    y = _dot(gm_n, wo_ref[D_ATTN:, :])
    mlp_gate()

    attn_t = jnp.concatenate(heads_out, axis=0)
    ms_a = jnp.mean(attn_t * attn_t, axis=0, keepdims=True)
    attn_n = (attn_t * lax.rsqrt(ms_a + NORM_EPS) * ag_ref[...]).astype(BF16)
    y = y + _dot_tn(attn_n, wo_ref[:D_ATTN, :])
    h1_scr[slot] = x_ref[0] + _rms_rows(y, pg_ref[...])
    mlp_end()


def _const_spec(shape):
    zeros = (0,) * len(shape)
    return pl.BlockSpec(shape, lambda *_: zeros, pipeline_mode=pl.Buffered(1))


def _rope_tables(seq):
    inv_freq = ROPE_THETA ** (-np.arange(ROPE_HALF, dtype=np.float64) / ROPE_HALF)
    ang = np.arange(seq, dtype=np.float64)[:, None] * inv_freq[None, :]
    cos, sin = np.cos(ang), np.sin(ang)
    cos_t, sin_t = cos.T, sin.T
    ones = np.ones((seq, HEAD_DIM - ROPE_DIM))
    zeros = np.zeros((seq, HEAD_DIM - ROPE_DIM))
    zh = np.zeros((seq, ROPE_HALF))
    cos_h = np.concatenate([cos, cos, ones], axis=1)
    sa_h = np.concatenate([-sin, zh, zeros], axis=1)
    sb_h = np.concatenate([zh, sin, zeros], axis=1)
    tile = lambda t: np.concatenate([t] * HEADS_PER_LANE_TILE, axis=1)
    tables = (cos_t, sin_t, tile(cos_h), tile(sa_h), tile(sb_h))
    return tuple(jnp.asarray(np.ascontiguousarray(t), dtype=F32) for t in tables)


def _layer(h, p, mix_pre_g, w_in, gmlp_ln_g, gmlp_ln_b, w_s, b_s, attn_out_g,
           gmlp_out_g, w_o, mix_post_g, mlp_pre_g, w_up, w_down, mlp_post_g,
           w_ple, w_ple_gate, b_ple_gate, ple_post_g):
    B, S, D = h.shape
    assert D == D_MODEL and S % MOBA_BLOCK == 0
    nb = S // MOBA_BLOCK
    T = MOBA_BLOCK
    n_tiles = B * nb
    row = lambda v: v.reshape(1, -1).astype(F32)

    qscale = np.ones((1, D_IN_PROJ), np.float32)
    qscale[:, :D_ATTN] = HEAD_DIM ** -0.5 * math.log2(math.e)
    cos_t, sin_t, cos_k, sa_k, sb_k = _rope_tables(S)
    bs_exp = jnp.repeat(b_s.T, GROUP_DIM, axis=1).astype(F32)

    def cur(s):
        t = jnp.minimum(s, n_tiles - 1)
        return t // nb, t % nb

    def prev(s):
        t = jnp.maximum(s - 1, 0)
        return t // nb, t % nb

    hbm = pl.BlockSpec(memory_space=pl.ANY)
    out = pl.pallas_call(
        _layer_kernel,
        grid=(n_tiles + 1,),
        in_specs=[
            pl.BlockSpec((1, T, D), lambda s: (*cur(s), 0)),
            pl.BlockSpec((1, T, D_PLE), lambda s: (*prev(s), 0)),
            _const_spec((1, D)),
            hbm,
            _const_spec((1, D_IN_PROJ)),
            pl.BlockSpec((ROPE_HALF, T), lambda s: (0, cur(s)[1])),
            pl.BlockSpec((ROPE_HALF, T), lambda s: (0, cur(s)[1])),
            pl.BlockSpec((T, LANES), lambda s: (cur(s)[1], 0)),
            pl.BlockSpec((T, LANES), lambda s: (cur(s)[1], 0)),
            pl.BlockSpec((T, LANES), lambda s: (cur(s)[1], 0)),
            _const_spec((1, D_GMLP)),
            _const_spec((1, D_GMLP)),
            _const_spec((N_GROUPS, GMLP_CHUNK, GMLP_CHUNK)),
            _const_spec((GMLP_CHUNK, D_GMLP)),
            _const_spec((D_ATTN, 1)),
            _const_spec((1, D_GMLP)),
            hbm,
            _const_spec((1, D)),
            _const_spec((1, D)),
            hbm,
            hbm,
            _const_spec((1, D)),
            hbm,
            _const_spec((1, D)),
            hbm,
            _const_spec((1, D)),
        ],
        out_specs=pl.BlockSpec((1, T, D), lambda s: (*prev(s), 0)),
        out_shape=jax.ShapeDtypeStruct((B, S, D), F32),
        scratch_shapes=([pltpu.VMEM((nb, T, D_ATTN), BF16),
                         pltpu.VMEM((nb, N_HEADS * V_ROWS, T), BF16),
                         pltpu.VMEM((nb, D_ATTN), F32),
                         pltpu.VMEM((T, D_GMLP), BF16),
                         pltpu.VMEM((T, D_GMLP), BF16),
                         pltpu.VMEM((2, T, D), F32),
                         pltpu.VMEM((D, D_IN_PROJ), BF16),
                         pltpu.VMEM((D_ATTN + D_GMLP, D), BF16),
                         pltpu.VMEM((D, D_FF), BF16),
                         pltpu.VMEM((D_FF, D), BF16),
                         pltpu.VMEM((D, D), BF16),
                         pltpu.VMEM((D_PLE, D), BF16),
                         pltpu.VMEM((2, STAGE_ROWS, D_FF), F32),
                         pltpu.SemaphoreType.DMA((2,))]
                        + [pltpu.VMEM((nb, T), F32)] * N_HEADS
                        + [pltpu.VMEM((LANES, T), BF16)] * N_HEADS
                        + [pltpu.VMEM((1, T), F32)] * N_HEADS
                        + [pltpu.VMEM((V_ROWS, T), F32)] * N_HEADS
                        + [pltpu.VMEM((MOBA_BLOCK, T), F32)] * N_HEADS),
        compiler_params=pltpu.CompilerParams(dimension_semantics=("arbitrary",),
                                             vmem_limit_bytes=VMEM_LIMIT_BYTES),
        name="layer",
    )(h, p, row(mix_pre_g), w_in.astype(F32), jnp.asarray(qscale), cos_t, sin_t, cos_k, sa_k, sb_k,
      row(gmlp_ln_g), row(gmlp_ln_b), w_s.astype(F32), bs_exp,
      attn_out_g.reshape(-1, 1).astype(F32), row(gmlp_out_g), w_o.astype(F32), row(mix_post_g),
      row(mlp_pre_g), w_up.astype(F32), w_down.astype(F32), row(mlp_post_g),
      w_ple_gate.astype(F32), row(b_ple_gate), w_ple.astype(F32), row(ple_post_g))
    return out


def kernel(x, p, mix_pre_g, w_in, gmlp_ln_g, gmlp_ln_b, w_s, b_s, attn_out_g,
           gmlp_out_g, w_o, mix_post_g, mlp_pre_g, w_up, w_down, mlp_post_g,
           w_ple, w_ple_gate, b_ple_gate, ple_post_g):
    h = x
    for i in range(w_in.shape[0]):
        h = _layer(h, p[i], mix_pre_g[i], w_in[i], gmlp_ln_g[i], gmlp_ln_b[i],
                   w_s[i], b_s[i], attn_out_g[i], gmlp_out_g[i], w_o[i],
                   mix_post_g[i], mlp_pre_g[i], w_up[i], w_down[i], mlp_post_g[i],
                   w_ple[i], w_ple_gate[i], b_ple_gate[i], ple_post_g[i])
    return h
```

```python
import functools
import math

import jax
import jax.numpy as jnp
import numpy as np
from jax import lax
from jax.experimental import pallas as pl
from jax.experimental.pallas import tpu as pltpu

D_MODEL = 1024
D_PLE = 256
N_HEADS = 8
HEAD_DIM = 64
D_ATTN = N_HEADS * HEAD_DIM
N_GROUPS = 8
GROUP_DIM = 64
D_GMLP = N_GROUPS * GROUP_DIM
D_IN_PROJ = 3 * D_ATTN + 2 * D_GMLP
D_FF = 4 * D_MODEL
MOBA_BLOCK = 256
MOBA_TOPK = 3
GMLP_CHUNK = 128
ROPE_THETA = 500000.0
ROPE_DIM = HEAD_DIM // 4
ROPE_HALF = ROPE_DIM // 2
NORM_EPS = 1e-6
NEG_INF = -1e30

LANES = 128
HEADS_PER_LANE_TILE = LANES // HEAD_DIM
BF16_SUBLANES = 16
V_ROWS = HEAD_DIM + BF16_SUBLANES
BIG = 1e30
SCORE_LEAD = 2
VMEM_LIMIT_BYTES = 58 * 1024 * 1024
FF_CHUNK = 1024
STAGE_ROWS = 128

F32 = jnp.float32
BF16 = jnp.bfloat16


def _dot(a, b):
    return jnp.dot(a, b, preferred_element_type=F32)


def _dot_tn(a, b):
    return lax.dot_general(a, b, (((0,), (0,)), ((), ())), preferred_element_type=F32)


def _rms_rows(x, g):
    ms = jnp.mean(x * x, axis=-1, keepdims=True)
    return x * lax.rsqrt(ms + NORM_EPS) * g


def _gelu_exact(x):
    return 0.5 * x * (1.0 + lax.erf(x * math.sqrt(0.5)))


def _load_weight_bf16(src_hbm, dst_ref, stage_ref, sem_ref, col_scale=None):
    n_rows, width = src_hbm.shape
    chunk = min(stage_ref.shape[1], n_rows)
    n_chunks = n_rows // chunk
    assert n_chunks * chunk == n_rows and width <= stage_ref.shape[2]

    def copy(c, slot):
        return pltpu.make_async_copy(src_hbm.at[pl.ds(c * chunk, chunk), :],
                                     stage_ref.at[slot, pl.ds(0, chunk), pl.ds(0, width)],
                                     sem_ref.at[slot])

    copy(0, 0).start()

    def body(c, carry):
        slot = lax.rem(c, 2)

        @pl.when(c + 1 < n_chunks)
        def _():
            copy(c + 1, 1 - slot).start()

        copy(c, slot).wait()
        w = stage_ref[slot, pl.ds(0, chunk), pl.ds(0, width)]
        if col_scale is not None:
            w = w * col_scale
        rows = pl.ds(pl.multiple_of(c * chunk, chunk), chunk)
        dst_ref[rows, :] = w.astype(BF16)
        return carry

    lax.fori_loop(0, n_chunks, body, 0)


def _layer_kernel(x_ref, p_ref, g_ref, win_hbm, qscale_ref, cost_ref, sint_ref,
                  cosk_ref, sak_ref, sbk_ref, lng_ref, lnb_ref,
                  ws_ref, bs_ref, ag_ref, gg_ref, wo_hbm, pg_ref,
                  g1_ref, wup_hbm, wdn_hbm, g2_ref, wgate_hbm, bgate_ref, wple_hbm, g3_ref,
                  o_ref,
                  k_scr, vt_scr, km_scr, ug_scr, vn_scr, h1_scr,
                  win_ref, wo_ref, wup_ref, wdn_ref, wgate_ref, wple_ref, wsm_ref, stage, sem,
                  *scratch):
    sel_scr, qz_scr, m_scr, acc_scr, s_scr = (
        scratch[j * N_HEADS:(j + 1) * N_HEADS] for j in range(5))
    nb = k_scr.shape[0]
    tq = MOBA_BLOCK
    n_tiles = pl.num_programs(0) - 1
    step = pl.program_id(0)
    live = step < n_tiles
    i = lax.rem(jnp.minimum(step, n_tiles - 1), nb)
    slot = lax.rem(step, 2)

    @pl.when(step == 0)
    def _():
        km_scr[...] = jnp.zeros(km_scr.shape, F32)
        h1_scr[...] = jnp.zeros(h1_scr.shape, F32)
        _load_weight_bf16(win_hbm, win_ref, stage, sem, col_scale=qscale_ref[...])
        _load_weight_bf16(wo_hbm, wo_ref, stage, sem)
        _load_weight_bf16(wup_hbm, wup_ref, stage, sem)
        _load_weight_bf16(wdn_hbm, wdn_ref, stage, sem)
        _load_weight_bf16(wgate_hbm, wgate_ref, stage, sem)
        _load_weight_bf16(wple_hbm, wple_ref, stage, sem)
        t_idx = lax.broadcasted_iota(jnp.int32, (GMLP_CHUNK, GMLP_CHUNK), 0)
        s_idx = lax.broadcasted_iota(jnp.int32, (GMLP_CHUNK, GMLP_CHUNK), 1)
        for g in range(N_GROUPS):
            wsm_ref[g] = jnp.where(s_idx <= t_idx, ws_ref[g], 0.0).astype(BF16)

    def scores(blk, h):
        hp = h // HEADS_PER_LANE_TILE
        return _dot(k_scr[blk, :, hp * LANES:(hp + 1) * LANES], qz_scr[h][...])

    def v_rows(blk, h):
        return vt_scr[blk, h * V_ROWS:(h + 1) * V_ROWS, :]

    @pl.when(live)
    def _():
        x = x_ref[0]
        hn = _rms_rows(x, g_ref[...]).astype(BF16)

        gv = _dot(hn, win_ref[:, 3 * D_ATTN + D_GMLP:])
        g2 = _gelu_exact(gv)
        mu = jnp.mean(g2, axis=-1, keepdims=True)
        xc = g2 - mu
        var = jnp.mean(xc * xc, axis=-1, keepdims=True)
        vn = xc * lax.rsqrt(var + NORM_EPS) * lng_ref[...] + lnb_ref[...]
        vn_scr[...] = vn.astype(BF16)

        qt = _dot(hn, win_ref[:, :D_ATTN]).T

        tok = _dot(hn, win_ref[:, D_ATTN:2 * D_ATTN])
        cos_k = cosk_ref[i]
        sa_k = sak_ref[i]
        sb_k = sbk_ref[i]
        kparts = []
        for j in range(D_ATTN // LANES):
            kb = tok[:, j * LANES:(j + 1) * LANES]
            up = pltpu.roll(kb, LANES - ROPE_HALF, axis=1)
            dn = pltpu.roll(kb, ROPE_HALF, axis=1)
            kparts.append(kb * cos_k + up * sa_k + dn * sb_k)
        k = jnp.concatenate(kparts, axis=1)
        k_scr[i] = k.astype(BF16)
        km_scr[pl.ds(i, 1), :] = jnp.mean(k, axis=0, keepdims=True)

        u = _dot(hn, win_ref[:, 3 * D_ATTN:3 * D_ATTN + D_GMLP])
        ug_scr[...] = _gelu_exact(u).astype(BF16)

        vt = _dot(hn, win_ref[:, 2 * D_ATTN:3 * D_ATTN]).T
        ones = jnp.ones((V_ROWS - HEAD_DIM, tq), F32)
        vparts = []
        for h in range(N_HEADS):
            vparts.append(vt[h * HEAD_DIM:(h + 1) * HEAD_DIM])
            vparts.append(ones)
        vt_scr[i] = jnp.concatenate(vparts, axis=0).astype(BF16)

        cos_t = cost_ref[i]
        sin_t = sint_ref[i]
        zeros_head = jnp.zeros((HEAD_DIM, tq), F32)
        for h in range(N_HEADS):
            hh = h % HEADS_PER_LANE_TILE
            r0 = h * HEAD_DIM
            x1 = qt[r0:r0 + ROPE_HALF]
            x2 = qt[r0 + ROPE_HALF:r0 + ROPE_DIM]
            q_h = [x1 * cos_t - x2 * sin_t, x2 * cos_t + x1 * sin_t, qt[r0 + ROPE_DIM:r0 + HEAD_DIM]]
            padded = [zeros_head] * hh + q_h + [zeros_head] * (HEADS_PER_LANE_TILE - 1 - hh)
            qz_scr[h][...] = jnp.concatenate(padded, axis=0).astype(BF16)

    @pl.when(live)
    def _():
        blk_row = lax.broadcasted_iota(jnp.int32, (nb, tq), 0)
        past = blk_row < i
        km = km_scr[...]
        km_hi = km.astype(BF16)
        r1 = km - km_hi.astype(F32)
        km_mid = r1.astype(BF16)
        km_lo = (r1 - km_mid.astype(F32)).astype(BF16)
        km3 = jnp.concatenate([km_hi, km_mid, km_lo], axis=0)
        for h in range(N_HEADS):
            hp = h // HEADS_PER_LANE_TILE
            s_scr[h][...] = scores(0, h)

            g3 = _dot(km3[:, hp * LANES:(hp + 1) * LANES], qz_scr[h][...])
            gate = g3[0:nb] + g3[nb:2 * nb] + g3[2 * nb:3 * nb]
            gate = jnp.where(past, gate, NEG_INF)
            rank = jnp.zeros((nb, tq), jnp.int32)
            for m in range(nb):
                gm = gate[m:m + 1, :]
                beats = (gm > gate) | ((gm == gate) & (m < blk_row))
                rank = rank + beats.astype(jnp.int32)
            sel = past & (rank < MOBA_TOPK)
            sel_scr[h][...] = sel.astype(F32)
            m_scr[h][...] = jnp.full((1, tq), NEG_INF, F32)
            acc_scr[h][...] = jnp.zeros((V_ROWS, tq), F32)

    def past_blocks(kb0, n_blocks):
        units = [(kb0 + j, h) for j in range(n_blocks) for h in range(N_HEADS)]
        ahead = {}
        for u in range(-SCORE_LEAD, len(units)):
            if 0 <= u + SCORE_LEAD < len(units):
                blk, head = units[u + SCORE_LEAD]
                ahead[u + SCORE_LEAD] = scores(blk + 1, head)
            if u < 0:
                continue
            kb, h = units[u]
            s = s_scr[h][...]
            keep = sel_scr[h][pl.ds(kb, 1), :] > 0.5
            m_run = m_scr[h][...]
            m_new = jnp.where(keep, jnp.maximum(m_run, jnp.max(s, axis=0, keepdims=True)), m_run)
            m_sub = jnp.where(keep, m_new, BIG)
            alpha = jnp.exp2(m_run - m_new)
            pb = jnp.exp2((s - m_sub).astype(BF16))
            m_scr[h][...] = m_new
            acc_scr[h][...] = alpha * acc_scr[h][...] + _dot(v_rows(kb, h), pb)
            s_scr[h][...] = ahead.pop(u)

    def two_blocks(t, carry):
        past_blocks(2 * t, 2)
        return carry

    n_past = jnp.where(live, i, 0)
    lax.fori_loop(0, n_past // 2, two_blocks, 0)

    @pl.when(lax.rem(n_past, 2) == 1)
    def _():
        past_blocks(n_past - 1, 1)

    mlp = {}

    def mlp_begin():
        mlp['h'] = h1_scr[1 - slot]
        mlp['hn'] = _rms_rows(mlp['h'], g1_ref[...]).astype(BF16)
        mlp['acc'] = None

    def mlp_up(c):
        up = jnp.maximum(_dot(mlp['hn'], wup_ref[:, c * FF_CHUNK:(c + 1) * FF_CHUNK]), 0.0)
        mlp['up'] = (up * up).astype(BF16)

    def mlp_down(c):
        d = _dot(mlp['up'], wdn_ref[c * FF_CHUNK:(c + 1) * FF_CHUNK, :])
        mlp['acc'] = d if mlp['acc'] is None else mlp['acc'] + d

    def mlp_gate():
        mlp['h2'] = mlp['h'] + _rms_rows(mlp['acc'], g2_ref[...])
        mlp['z'] = _dot(mlp['h2'].astype(BF16), wgate_ref[...]) + bgate_ref[...]

    def mlp_ple():
        mlp['ple'] = _dot(p_ref[0].astype(BF16), wple_ref[...])

    def mlp_end():
        gate = 1.0 / (1.0 + jnp.exp(-mlp['z']))
        o_ref[0] = mlp['h2'] + _rms_rows(mlp['ple'] * gate, g3_ref[...])

    n_ff = D_FF // FF_CHUNK
    heads_per_piece = N_HEADS // (2 * n_ff)
    assert heads_per_piece * 2 * n_ff == N_HEADS
    pieces_before_head = [[mlp_begin]] + [[] for _ in range(N_HEADS - 1)]
    for c in range(n_ff):
        pieces_before_head[2 * c * heads_per_piece].append(functools.partial(mlp_up, c))
        pieces_before_head[(2 * c + 1) * heads_per_piece].append(functools.partial(mlp_down, c))

    kpos = lax.broadcasted_iota(jnp.int32, (MOBA_BLOCK, tq), 0)
    qpos = lax.broadcasted_iota(jnp.int32, (MOBA_BLOCK, tq), 1)
    causal = kpos <= qpos
    lane =lax.broadcasted_iota(jnp.int32, (GMLP_CHUNK, LANES), 1)
    first_group = lane < GROUP_DIM
    n_lane_tiles = D_GMLP // LANES
    n_chunks = tq // GMLP_CHUNK
    assert n_lane_tiles * n_chunks == N_HEADS
    heads_out = []
    cols = [[None] * n_lane_tiles for _ in range(n_chunks)]
    for h in range(N_HEADS):
        for piece in pieces_before_head[h]:
            piece()
        c, gp = divmod(h, n_lane_tiles)
        vp = vn_scr[c * GMLP_CHUNK:(c + 1) * GMLP_CHUNK, gp * LANES:(gp + 1) * LANES]
        cols[c][gp] = jnp.where(first_group, _dot(wsm_ref[2 * gp], vp), _dot(wsm_ref[2 * gp + 1], vp))

        st = jnp.where(causal, s_scr[h][...], NEG_INF)
        m_run = m_scr[h][...]
        m_new = jnp.maximum(m_run, jnp.max(st, axis=0, keepdims=True))
        alpha = jnp.exp2(m_run - m_new)
        p = jnp.exp2((st - m_new).astype(BF16))
        acc = alpha * acc_scr[h][...] + _dot(v_rows(i, h), p)
        heads_out.append(acc[0:HEAD_DIM] / acc[HEAD_DIM:HEAD_DIM + 1])

    mlp_ple()
    mixed = jnp.concatenate(
        [jnp.concatenate(cols[c], axis=1) + bs_ref[...] for c in range(n_chunks)], axis=0)
    gm_out = ug_scr[...].astype(F32) * mixed
    gm_n = _rms_rows(gm_out, gg_ref[...]).astype(BF16)
    y = _dot(gm_n, wo_ref[D_ATTN:, :])
    mlp_gate()

    attn_t = jnp.concatenate(heads_out, axis=0)
    ms_a = jnp.mean(attn_t * attn_t, axis=0, keepdims=True)
    attn_n = (attn_t * lax.rsqrt(ms_a + NORM_EPS) * ag_ref[...]).astype(BF16)
    y = y + _dot_tn(attn_n, wo_ref[:D_ATTN, :])
    h1_scr[slot] = x_ref[0] + _rms_rows(y, pg_ref[...])
    mlp_end()


def _const_spec(shape):
    zeros = (0,) * len(shape)
    return pl.BlockSpec(shape, lambda *_: zeros, pipeline_mode=pl.Buffered(1))


def _rope_tables(seq, block):
    inv_freq = ROPE_THETA ** (-np.arange(ROPE_HALF, dtype=np.float64) / ROPE_HALF)
    ang = np.arange(seq, dtype=np.float64)[:, None] * inv_freq[None, :]
    cos, sin = np.cos(ang), np.sin(ang)
    n_blocks = seq // block
    cos_t, sin_t = (t.T.reshape(ROPE_HALF, n_blocks, block).transpose(1, 0, 2) for t in (cos, sin))
    ones = np.ones((seq, HEAD_DIM - ROPE_DIM))
    zeros = np.zeros((seq, HEAD_DIM - ROPE_DIM))
    zh = np.zeros((seq, ROPE_HALF))
    cos_h = np.concatenate([cos, cos, ones], axis=1)
    sa_h = np.concatenate([-sin, zh, zeros], axis=1)
    sb_h = np.concatenate([zh, sin, zeros], axis=1)
    tile = lambda t: np.concatenate([t] * HEADS_PER_LANE_TILE, axis=1).reshape(n_blocks, block, LANES)
    tables = (cos_t, sin_t, tile(cos_h), tile(sa_h), tile(sb_h))
    return tuple(jnp.asarray(np.ascontiguousarray(t), dtype=F32) for t in tables)


def _layer(h, p, mix_pre_g, w_in, gmlp_ln_g, gmlp_ln_b, w_s, b_s, attn_out_g,
           gmlp_out_g, w_o, mix_post_g, mlp_pre_g, w_up, w_down, mlp_post_g,
           w_ple, w_ple_gate, b_ple_gate, ple_post_g):
    B, S, D = h.shape
    assert D == D_MODEL and S % MOBA_BLOCK == 0
    nb = S // MOBA_BLOCK
    T = MOBA_BLOCK
    n_tiles = B * nb
    row = lambda v: v.reshape(1, -1).astype(F32)

    qscale = np.ones((1, D_IN_PROJ), np.float32)
    qscale[:, :D_ATTN] = HEAD_DIM ** -0.5 * math.log2(math.e)
    cos_t, sin_t, cos_k, sa_k, sb_k = _rope_tables(S, T)
    bs_exp = jnp.repeat(b_s.T, GROUP_DIM, axis=1).astype(F32)

    def cur(s):
        t = jnp.minimum(s, n_tiles - 1)
        return t // nb, t % nb

    def prev(s):
        t = jnp.maximum(s - 1, 0)
        return t // nb, t % nb

    hbm = pl.BlockSpec(memory_space=pl.ANY)
    out = pl.pallas_call(
        _layer_kernel,
        grid=(n_tiles + 1,),
        in_specs=[
            pl.BlockSpec((1, T, D), lambda s: (*cur(s), 0)),
            pl.BlockSpec((1, T, D_PLE), lambda s: (*prev(s), 0)),
            _const_spec((1, D)),
            hbm,
            _const_spec((1, D_IN_PROJ)),
            _const_spec((nb, ROPE_HALF, T)),
            _const_spec((nb, ROPE_HALF, T)),
            _const_spec((nb, T, LANES)),
            _const_spec((nb, T, LANES)),
            _const_spec((nb, T, LANES)),
            _const_spec((1, D_GMLP)),
            _const_spec((1, D_GMLP)),
            _const_spec((N_GROUPS, GMLP_CHUNK, GMLP_CHUNK)),
            _const_spec((GMLP_CHUNK, D_GMLP)),
            _const_spec((D_ATTN, 1)),
            _const_spec((1, D_GMLP)),
            hbm,
            _const_spec((1, D)),
            _const_spec((1, D)),
            hbm,
            hbm,
            _const_spec((1, D)),
            hbm,
            _const_spec((1, D)),
            hbm,
            _const_spec((1, D)),
        ],
        out_specs=pl.BlockSpec((1, T, D), lambda s: (*prev(s), 0)),
        out_shape=jax.ShapeDtypeStruct((B, S, D), F32),
        scratch_shapes=([pltpu.VMEM((nb, T, D_ATTN), BF16),
                         pltpu.VMEM((nb, N_HEADS * V_ROWS, T), BF16),
                         pltpu.VMEM((nb, D_ATTN), F32),
                         pltpu.VMEM((T, D_GMLP), BF16),
                         pltpu.VMEM((T, D_GMLP), BF16),
                         pltpu.VMEM((2, T, D), F32),
                         pltpu.VMEM((D, D_IN_PROJ), BF16),
                         pltpu.VMEM((D_ATTN + D_GMLP, D), BF16),
                         pltpu.VMEM((D, D_FF), BF16),
                         pltpu.VMEM((D_FF, D), BF16),
                         pltpu.VMEM((D, D), BF16),
                         pltpu.VMEM((D_PLE, D), BF16),
                         pltpu.VMEM((N_GROUPS, GMLP_CHUNK, GMLP_CHUNK), BF16),
                         pltpu.VMEM((2, STAGE_ROWS, D_FF), F32),
                         pltpu.SemaphoreType.DMA((2,))]
                        + [pltpu.VMEM((nb, T), F32)] * N_HEADS
                        + [pltpu.VMEM((LANES, T), BF16)] * N_HEADS
                        + [pltpu.VMEM((1, T), F32)] * N_HEADS
                        + [pltpu.VMEM((V_ROWS, T), F32)] * N_HEADS
                        + [pltpu.VMEM((MOBA_BLOCK, T), F32)] * N_HEADS),
        compiler_params=pltpu.CompilerParams(dimension_semantics=("arbitrary",),
                                             vmem_limit_bytes=VMEM_LIMIT_BYTES),
        name="layer",
    )(h, p, row(mix_pre_g), w_in.astype(F32), jnp.asarray(qscale), cos_t, sin_t, cos_k, sa_k, sb_k,
      row(gmlp_ln_g), row(gmlp_ln_b), w_s.astype(F32), bs_exp,
      attn_out_g.reshape(-1, 1).astype(F32), row(gmlp_out_g), w_o.astype(F32), row(mix_post_g),
      row(mlp_pre_g), w_up.astype(F32), w_down.astype(F32), row(mlp_post_g),
      w_ple_gate.astype(F32), row(b_ple_gate), w_ple.astype(F32), row(ple_post_g))
    return out


def kernel(x, p, mix_pre_g, w_in, gmlp_ln_g, gmlp_ln_b, w_s, b_s, attn_out_g,
           gmlp_out_g, w_o, mix_post_g, mlp_pre_g, w_up, w_down, mlp_post_g,
           w_ple, w_ple_gate, b_ple_gate, ple_post_g):
    h = x
    for i in range(w_in.shape[0]):
        h = _layer(h, p[i], mix_pre_g[i], w_in[i], gmlp_ln_g[i], gmlp_ln_b[i],
                   w_s[i], b_s[i], attn_out_g[i], gmlp_out_g[i], w_o[i],
                   mix_post_g[i], mlp_pre_g[i], w_up[i], w_down[i], mlp_post_g[i],
                   w_ple[i], w_ple_gate[i], b_ple_gate[i], ple_post_g[i])
    return h
```

```python
import functools
import math

import jax
import jax.numpy as jnp
import numpy as np
from jax import lax
from jax.experimental import pallas as pl
from jax.experimental.pallas import tpu as pltpu

D_MODEL = 1024
D_PLE = 256
N_HEADS = 8
HEAD_DIM = 64
D_ATTN = N_HEADS * HEAD_DIM
N_GROUPS = 8
GROUP_DIM = 64
D_GMLP = N_GROUPS * GROUP_DIM
D_IN_PROJ = 3 * D_ATTN + 2 * D_GMLP
D_FF = 4 * D_MODEL
MOBA_BLOCK = 256
MOBA_TOPK = 3
GMLP_CHUNK = 128
ROPE_THETA = 500000.0
ROPE_DIM = HEAD_DIM // 4
ROPE_HALF = ROPE_DIM // 2
NORM_EPS = 1e-6
NEG_INF = -1e30

LANES = 128
HEADS_PER_LANE_TILE = LANES // HEAD_DIM
GROUPS_PER_LANE_TILE = LANES // GROUP_DIM
BF16_SUBLANES = 16
V_ROWS = HEAD_DIM + BF16_SUBLANES
BIG = 1e30
SCORE_LEAD = 2
VMEM_LIMIT_BYTES = 58 * 1024 * 1024
FF_CHUNK = 1024
STAGE_ROWS = 128

F32 = jnp.float32
BF16 = jnp.bfloat16


def _dot(a, b):
    return jnp.dot(a, b, preferred_element_type=F32)


def _dot_tn(a, b):
    return lax.dot_general(a, b, (((0,), (0,)), ((), ())), preferred_element_type=F32)


def _rms_rows(x, g):
    ms = jnp.mean(x * x, axis=-1, keepdims=True)
    return x * lax.rsqrt(ms + NORM_EPS) * g


def _gelu_exact(x):
    return 0.5 * x * (1.0 + lax.erf(x * math.sqrt(0.5)))


def _load_weight_bf16(src_hbm, dst_ref, stage_ref, sem_ref, col_scale=None):
    n_rows, width = src_hbm.shape
    chunk = min(stage_ref.shape[1], n_rows)
    n_chunks = n_rows // chunk
    assert n_chunks * chunk == n_rows and width <= stage_ref.shape[2]

    def copy(c, slot):
        return pltpu.make_async_copy(src_hbm.at[pl.ds(c * chunk, chunk), :],
                                     stage_ref.at[slot, pl.ds(0, chunk), pl.ds(0, width)],
                                     sem_ref.at[slot])

    copy(0, 0).start()

    def body(c, carry):
        slot = lax.rem(c, 2)

        @pl.when(c + 1 < n_chunks)
        def _():
            copy(c + 1, 1 - slot).start()

        copy(c, slot).wait()
        w = stage_ref[slot, pl.ds(0, chunk), pl.ds(0, width)]
        if col_scale is not None:
            w = w * col_scale
        rows = pl.ds(pl.multiple_of(c * chunk, chunk), chunk)
        dst_ref[rows, :] = w.astype(BF16)
        return carry

    lax.fori_loop(0, n_chunks, body, 0)


def _layer_kernel(x_ref, p_ref, g_ref, win_hbm, qscale_ref, cost_ref, sint_ref,
                  cosk_ref, sak_ref, sbk_ref, lng_ref, lnb_ref,
                  ws_ref, bs_ref, ag_ref, gg_ref, wo_hbm, pg_ref,
                  g1_ref, wup_hbm, wdn_hbm, g2_ref, wgate_hbm, bgate_ref, wple_hbm, g3_ref,
                  o_ref,
                  k_scr, vt_scr, km_scr, ug_scr, vn_scr, h1_scr,
                  win_ref, wo_ref, wup_ref, wdn_ref, wgate_ref, wple_ref, wsm_ref, stage, sem,
                  *scratch):
    sel_scr, qz_scr, m_scr, acc_scr, s_scr = (
        scratch[j * N_HEADS:(j + 1) * N_HEADS] for j in range(5))
    nb = k_scr.shape[0]
    tq = MOBA_BLOCK
    n_tiles = pl.num_programs(0) - 1
    step = pl.program_id(0)
    live = step < n_tiles
    i = lax.rem(jnp.minimum(step, n_tiles - 1), nb)
    slot = lax.rem(step, 2)

    @pl.when(step == 0)
    def _():
        km_scr[...] = jnp.zeros(km_scr.shape, F32)
        h1_scr[...] = jnp.zeros(h1_scr.shape, F32)
        _load_weight_bf16(win_hbm, win_ref, stage, sem, col_scale=qscale_ref[...])
        _load_weight_bf16(wo_hbm, wo_ref, stage, sem)
        _load_weight_bf16(wup_hbm, wup_ref, stage, sem)
        _load_weight_bf16(wdn_hbm, wdn_ref, stage, sem)
        _load_weight_bf16(wgate_hbm, wgate_ref, stage, sem)
        _load_weight_bf16(wple_hbm, wple_ref, stage, sem)
        t_idx = lax.broadcasted_iota(jnp.int32, (GMLP_CHUNK, GMLP_CHUNK), 0)
        s_idx = lax.broadcasted_iota(jnp.int32, (GMLP_CHUNK, GMLP_CHUNK), 1)
        for g in range(N_GROUPS):
            tile, part = divmod(g, GROUPS_PER_LANE_TILE)
            wsm_ref[tile, :, part * GMLP_CHUNK:(part + 1) * GMLP_CHUNK] = (
                jnp.where(s_idx <= t_idx, ws_ref[g], 0.0).astype(BF16))

    def scores(blk, h):
        hp = h // HEADS_PER_LANE_TILE
        return _dot(k_scr[blk, :, hp * LANES:(hp + 1) * LANES], qz_scr[h][...])

    def v_rows(blk, h):
        return vt_scr[blk, h * V_ROWS:(h + 1) * V_ROWS, :]

    @pl.when(live)
    def _():
        x = x_ref[0]
        hn = _rms_rows(x, g_ref[...]).astype(BF16)

        gv = _dot(hn, win_ref[:, 3 * D_ATTN + D_GMLP:])
        g2 = _gelu_exact(gv)
        mu = jnp.mean(g2, axis=-1, keepdims=True)
        xc = g2 - mu
        var = jnp.mean(xc * xc, axis=-1, keepdims=True)
        vn = xc * lax.rsqrt(var + NORM_EPS) * lng_ref[...] + lnb_ref[...]
        vn_scr[...] = vn.astype(BF16)

        qt = _dot(hn, win_ref[:, :D_ATTN]).T

        tok = _dot(hn, win_ref[:, D_ATTN:2 * D_ATTN])
        cos_k = cosk_ref[i]
        sa_k = sak_ref[i]
        sb_k = sbk_ref[i]
        kparts = []
        for j in range(D_ATTN // LANES):
            kb = tok[:, j * LANES:(j + 1) * LANES]
            up = pltpu.roll(kb, LANES - ROPE_HALF, axis=1)
            dn = pltpu.roll(kb, ROPE_HALF, axis=1)
            kparts.append(kb * cos_k + up * sa_k + dn * sb_k)
        k = jnp.concatenate(kparts, axis=1)
        k_scr[i] = k.astype(BF16)
        km_scr[pl.ds(i, 1), :] = jnp.mean(k, axis=0, keepdims=True)

        u = _dot(hn, win_ref[:, 3 * D_ATTN:3 * D_ATTN + D_GMLP])
        ug_scr[...] = _gelu_exact(u).astype(BF16)

        vt = _dot(hn, win_ref[:, 2 * D_ATTN:3 * D_ATTN]).T
        ones = jnp.ones((V_ROWS - HEAD_DIM, tq), F32)
        vparts = []
        for h in range(N_HEADS):
            vparts.append(vt[h * HEAD_DIM:(h + 1) * HEAD_DIM])
            vparts.append(ones)
        vt_scr[i] = jnp.concatenate(vparts, axis=0).astype(BF16)

        cos_t = cost_ref[i]
        sin_t = sint_ref[i]
        zeros_head = jnp.zeros((HEAD_DIM, tq), F32)
        for h in range(N_HEADS):
            hh = h % HEADS_PER_LANE_TILE
            r0 = h * HEAD_DIM
            x1 = qt[r0:r0 + ROPE_HALF]
            x2 = qt[r0 + ROPE_HALF:r0 + ROPE_DIM]
            q_h = [x1 * cos_t - x2 * sin_t, x2 * cos_t + x1 * sin_t, qt[r0 + ROPE_DIM:r0 + HEAD_DIM]]
            padded = [zeros_head] * hh + q_h + [zeros_head] * (HEADS_PER_LANE_TILE - 1 - hh)
            qz_scr[h][...] = jnp.concatenate(padded, axis=0).astype(BF16)

    @pl.when(live)
    def _():
        blk_row = lax.broadcasted_iota(jnp.int32, (nb, tq), 0)
        past = blk_row < i
        km = km_scr[...]
        km_hi = km.astype(BF16)
        r1 = km - km_hi.astype(F32)
        km_mid = r1.astype(BF16)
        km_lo = (r1 - km_mid.astype(F32)).astype(BF16)
        km3 = jnp.concatenate([km_hi, km_mid, km_lo], axis=0)
        for h in range(N_HEADS):
            hp = h // HEADS_PER_LANE_TILE
            s_scr[h][...] = scores(0, h)

            g3 = _dot(km3[:, hp * LANES:(hp + 1) * LANES], qz_scr[h][...])
            gate = g3[0:nb] + g3[nb:2 * nb] + g3[2 * nb:3 * nb]
            gate = jnp.where(past, gate, NEG_INF)
            rank = jnp.zeros((nb, tq), jnp.int32)
            for m in range(nb):
                gm = gate[m:m + 1, :]
                beats = (gm > gate) | ((gm == gate) & (m < blk_row))
                rank = rank + beats.astype(jnp.int32)
            sel = past & (rank < MOBA_TOPK)
            sel_scr[h][...] = sel.astype(F32)
            m_scr[h][...] = jnp.full((1, tq), NEG_INF, F32)
            acc_scr[h][...] = jnp.zeros((V_ROWS, tq), F32)

    def past_blocks(kb0, n_blocks):
        units = [(kb0 + j, h) for j in range(n_blocks) for h in range(N_HEADS)]
        ahead = {}
        for u in range(-SCORE_LEAD, len(units)):
            if 0 <= u + SCORE_LEAD < len(units):
                blk, head = units[u + SCORE_LEAD]
                ahead[u + SCORE_LEAD] = scores(blk + 1, head)
            if u < 0:
                continue
            kb, h = units[u]
            s = s_scr[h][...]
            keep = sel_scr[h][pl.ds(kb, 1), :] > 0.5
            m_run = m_scr[h][...]
            m_new = jnp.where(keep, jnp.maximum(m_run, jnp.max(s, axis=0, keepdims=True)), m_run)
            m_sub = jnp.where(keep, m_new, BIG)
            alpha = jnp.exp2(m_run - m_new)
            pb = jnp.exp2((s - m_sub).astype(BF16))
            m_scr[h][...] = m_new
            acc_scr[h][...] = alpha * acc_scr[h][...] + _dot(v_rows(kb, h), pb)
            s_scr[h][...] = ahead.pop(u)

    def two_blocks(t, carry):
        past_blocks(2 * t, 2)
        return carry

    n_past = jnp.where(live, i, 0)
    lax.fori_loop(0, n_past // 2, two_blocks, 0)

    @pl.when(lax.rem(n_past, 2) == 1)
    def _():
        past_blocks(n_past - 1, 1)

    mlp = {}

    def mlp_begin():
        mlp['h'] = h1_scr[1 - slot]
        mlp['hn'] = _rms_rows(mlp['h'], g1_ref[...]).astype(BF16)
        mlp['acc'] = None

    def mlp_up(c):
        up = jnp.maximum(_dot(mlp['hn'], wup_ref[:, c * FF_CHUNK:(c + 1) * FF_CHUNK]), 0.0)
        mlp['up'] = (up * up).astype(BF16)

    def mlp_down(c):
        d = _dot(mlp['up'], wdn_ref[c * FF_CHUNK:(c + 1) * FF_CHUNK, :])
        mlp['acc'] = d if mlp['acc'] is None else mlp['acc'] + d

    def mlp_gate():
        mlp['h2'] = mlp['h'] + _rms_rows(mlp['acc'], g2_ref[...])
        mlp['z'] = _dot(mlp['h2'].astype(BF16), wgate_ref[...]) + bgate_ref[...]

    def mlp_ple():
        mlp['ple'] = _dot(p_ref[0].astype(BF16), wple_ref[...])

    def mlp_end():
        gate = 1.0 / (1.0 + jnp.exp(-mlp['z']))
        o_ref[0] = mlp['h2'] + _rms_rows(mlp['ple'] * gate, g3_ref[...])

    n_ff = D_FF // FF_CHUNK
    heads_per_piece = N_HEADS // (2 * n_ff)
    assert heads_per_piece * 2 * n_ff == N_HEADS
    pieces_before_head = [[mlp_begin]] + [[] for _ in range(N_HEADS - 1)]
    for c in range(n_ff):
        pieces_before_head[2 * c * heads_per_piece].append(functools.partial(mlp_up, c))
        pieces_before_head[(2 * c + 1) * heads_per_piece].append(functools.partial(mlp_down, c))

    kpos = lax.broadcasted_iota(jnp.int32, (MOBA_BLOCK, tq), 0)
    qpos = lax.broadcasted_iota(jnp.int32, (MOBA_BLOCK, tq), 1)
    causal = kpos <= qpos
    lane = lax.broadcasted_iota(jnp.int32, (GMLP_CHUNK, LANES), 1)
    n_lane_tiles = D_GMLP // LANES
    n_chunks = tq // GMLP_CHUNK
    assert n_lane_tiles * n_chunks == N_HEADS
    heads_out = []
    cols = [[None] * n_lane_tiles for _ in range(n_chunks)]
    for h in range(N_HEADS):
        for piece in pieces_before_head[h]:
            piece()
        c, gp = divmod(h, n_lane_tiles)
        vp = vn_scr[c * GMLP_CHUNK:(c + 1) * GMLP_CHUNK, gp * LANES:(gp + 1) * LANES]
        stacked = jnp.concatenate(
            [jnp.where(lane // GROUP_DIM == part, vp, jnp.zeros_like(vp))
             for part in range(GROUPS_PER_LANE_TILE)], axis=0)
        cols[c][gp] = _dot(wsm_ref[gp], stacked)

        st = jnp.where(causal, s_scr[h][...], NEG_INF)
        m_run = m_scr[h][...]
        m_new = jnp.maximum(m_run, jnp.max(st, axis=0, keepdims=True))
        alpha = jnp.exp2(m_run - m_new)
        p = jnp.exp2((st - m_new).astype(BF16))
        acc = alpha * acc_scr[h][...] + _dot(v_rows(i, h), p)
        heads_out.append(acc[0:HEAD_DIM] / acc[HEAD_DIM:HEAD_DIM + 1])

    mlp_ple()
    mixed = jnp.concatenate(
        [jnp.concatenate(cols[c], axis=1) + bs_ref[...] for c in range(n_chunks)], axis=0)
    gm_out = ug_scr[...].astype(F32) * mixed
    gm_n = _rms_rows(gm_out, gg_ref[...]).astype(BF16)
    y = _dot(gm_n, wo_ref[D_ATTN:, :])
    mlp_gate()

    attn_t = jnp.concatenate(heads_out, axis=0)
    ms_a = jnp.mean(attn_t * attn_t, axis=0, keepdims=True)
    attn_n = (attn_t * lax.rsqrt(ms_a + NORM_EPS) * ag_ref[...]).astype(BF16)
    y = y + _dot_tn(attn_n, wo_ref[:D_ATTN, :])
    h1_scr[slot] = x_ref[0] + _rms_rows(y, pg_ref[...])
    mlp_end()


def _const_spec(shape):
    zeros = (0,) * len(shape)
    return pl.BlockSpec(shape, lambda *_: zeros, pipeline_mode=pl.Buffered(1))


def _rope_tables(seq, block):
    inv_freq = ROPE_THETA ** (-np.arange(ROPE_HALF, dtype=np.float64) / ROPE_HALF)
    ang = np.arange(seq, dtype=np.float64)[:, None] * inv_freq[None, :]
    cos, sin = np.cos(ang), np.sin(ang)
    n_blocks = seq // block
    cos_t, sin_t = (t.T.reshape(ROPE_HALF, n_blocks, block).transpose(1, 0, 2) for t in (cos, sin))
    ones = np.ones((seq, HEAD_DIM - ROPE_DIM))
    zeros = np.zeros((seq, HEAD_DIM - ROPE_DIM))
    zh = np.zeros((seq, ROPE_HALF))
    cos_h = np.concatenate([cos, cos, ones], axis=1)
    sa_h = np.concatenate([-sin, zh, zeros], axis=1)
    sb_h = np.concatenate([zh, sin, zeros], axis=1)
    tile = lambda t: np.concatenate([t] * HEADS_PER_LANE_TILE, axis=1).reshape(n_blocks, block, LANES)
    tables = (cos_t, sin_t, tile(cos_h), tile(sa_h), tile(sb_h))
    return tuple(jnp.asarray(np.ascontiguousarray(t), dtype=F32) for t in tables)


def _layer(h, p, mix_pre_g, w_in, gmlp_ln_g, gmlp_ln_b, w_s, b_s, attn_out_g,
           gmlp_out_g, w_o, mix_post_g, mlp_pre_g, w_up, w_down, mlp_post_g,
           w_ple, w_ple_gate, b_ple_gate, ple_post_g):
    B, S, D = h.shape
    assert D == D_MODEL and S % MOBA_BLOCK == 0
    nb = S // MOBA_BLOCK
    T = MOBA_BLOCK
    n_tiles = B * nb
    row = lambda v: v.reshape(1, -1).astype(F32)

    qscale = np.ones((1, D_IN_PROJ), np.float32)
    qscale[:, :D_ATTN] = HEAD_DIM ** -0.5 * math.log2(math.e)
    cos_t, sin_t, cos_k, sa_k, sb_k = _rope_tables(S, T)
    bs_exp = jnp.repeat(b_s.T, GROUP_DIM, axis=1).astype(F32)

    def cur(s):
        t = jnp.minimum(s, n_tiles - 1)
        return t // nb, t % nb

    def prev(s):
        t = jnp.maximum(s - 1, 0)
        return t // nb, t % nb

    hbm = pl.BlockSpec(memory_space=pl.ANY)
    out = pl.pallas_call(
        _layer_kernel,
        grid=(n_tiles + 1,),
        in_specs=[
            pl.BlockSpec((1, T, D), lambda s: (*cur(s), 0)),
            pl.BlockSpec((1, T, D_PLE), lambda s: (*prev(s), 0)),
            _const_spec((1, D)),
            hbm,
            _const_spec((1, D_IN_PROJ)),
            _const_spec((nb, ROPE_HALF, T)),
            _const_spec((nb, ROPE_HALF, T)),
            _const_spec((nb, T, LANES)),
            _const_spec((nb, T, LANES)),
            _const_spec((nb, T, LANES)),
            _const_spec((1, D_GMLP)),
            _const_spec((1, D_GMLP)),
            _const_spec((N_GROUPS, GMLP_CHUNK, GMLP_CHUNK)),
            _const_spec((GMLP_CHUNK, D_GMLP)),
            _const_spec((D_ATTN, 1)),
            _const_spec((1, D_GMLP)),
            hbm,
            _const_spec((1, D)),
            _const_spec((1, D)),
            hbm,
            hbm,
            _const_spec((1, D)),
            hbm,
            _const_spec((1, D)),
            hbm,
            _const_spec((1, D)),
        ],
        out_specs=pl.BlockSpec((1, T, D), lambda s: (*prev(s), 0)),
        out_shape=jax.ShapeDtypeStruct((B, S, D), F32),
        scratch_shapes=([pltpu.VMEM((nb, T, D_ATTN), BF16),
                         pltpu.VMEM((nb, N_HEADS * V_ROWS, T), BF16),
                         pltpu.VMEM((nb, D_ATTN), F32),
                         pltpu.VMEM((T, D_GMLP), BF16),
                         pltpu.VMEM((T, D_GMLP), BF16),
                         pltpu.VMEM((2, T, D), F32),
                         pltpu.VMEM((D, D_IN_PROJ), BF16),
                         pltpu.VMEM((D_ATTN + D_GMLP, D), BF16),
                         pltpu.VMEM((D, D_FF), BF16),
                         pltpu.VMEM((D_FF, D), BF16),
                         pltpu.VMEM((D, D), BF16),
                         pltpu.VMEM((D_PLE, D), BF16),
                         pltpu.VMEM((D_GMLP // LANES, GMLP_CHUNK, GROUPS_PER_LANE_TILE * GMLP_CHUNK), BF16),
                         pltpu.VMEM((2, STAGE_ROWS, D_FF), F32),
                         pltpu.SemaphoreType.DMA((2,))]
                        + [pltpu.VMEM((nb, T), F32)] * N_HEADS
                        + [pltpu.VMEM((LANES, T), BF16)] * N_HEADS
                        + [pltpu.VMEM((1, T), F32)] * N_HEADS
                        + [pltpu.VMEM((V_ROWS, T), F32)] * N_HEADS
                        + [pltpu.VMEM((MOBA_BLOCK, T), F32)] * N_HEADS),
        compiler_params=pltpu.CompilerParams(dimension_semantics=("arbitrary",),
                                             vmem_limit_bytes=VMEM_LIMIT_BYTES),
        name="layer",
    )(h, p, row(mix_pre_g), w_in.astype(F32), jnp.asarray(qscale), cos_t, sin_t, cos_k, sa_k, sb_k,
      row(gmlp_ln_g), row(gmlp_ln_b), w_s.astype(F32), bs_exp,
      attn_out_g.reshape(-1, 1).astype(F32), row(gmlp_out_g), w_o.astype(F32), row(mix_post_g),
      row(mlp_pre_g), w_up.astype(F32), w_down.astype(F32), row(mlp_post_g),
      w_ple_gate.astype(F32), row(b_ple_gate), w_ple.astype(F32), row(ple_post_g))
    return out


def kernel(x, p, mix_pre_g, w_in, gmlp_ln_g, gmlp_ln_b, w_s, b_s, attn_out_g,
           gmlp_out_g, w_o, mix_post_g, mlp_pre_g, w_up, w_down, mlp_post_g,
           w_ple, w_ple_gate, b_ple_gate, ple_post_g):
    h = x
    for i in range(w_in.shape[0]):
        h = _layer(h, p[i], mix_pre_g[i], w_in[i], gmlp_ln_g[i], gmlp_ln_b[i],
                   w_s[i], b_s[i], attn_out_g[i], gmlp_out_g[i], w_o[i],
                   mix_post_g[i], mlp_pre_g[i], w_up[i], w_down[i], mlp_post_g[i],
                   w_ple[i], w_ple_gate[i], b_ple_gate[i], ple_post_g[i])
    return h
```

```python
import functools
import math

import jax
import jax.numpy as jnp
import numpy as np
from jax import lax
from jax.experimental import pallas as pl
from jax.experimental.pallas import tpu as pltpu

D_MODEL = 1024
D_PLE = 256
N_HEADS = 8
HEAD_DIM = 64
D_ATTN = N_HEADS * HEAD_DIM
N_GROUPS = 8
GROUP_DIM = 64
D_GMLP = N_GROUPS * GROUP_DIM
D_IN_PROJ = 3 * D_ATTN + 2 * D_GMLP
D_FF = 4 * D_MODEL
MOBA_BLOCK = 256
MOBA_TOPK = 3
GMLP_CHUNK = 128
ROPE_THETA = 500000.0
ROPE_DIM = HEAD_DIM // 4
ROPE_HALF = ROPE_DIM // 2
NORM_EPS = 1e-6
NEG_INF = -1e30

LANES = 128
HEADS_PER_LANE_TILE = LANES // HEAD_DIM
MXU_COLS = 256
GROUPS_PER_GATE_TILE = MXU_COLS // GROUP_DIM
BF16_SUBLANES = 16
V_ROWS = HEAD_DIM + BF16_SUBLANES
BIG = 1e30
SCORE_LEAD = 2
VMEM_LIMIT_BYTES = 58 * 1024 * 1024
FF_CHUNK = 1024
STAGE_ROWS = 128

F32 = jnp.float32
BF16 = jnp.bfloat16


def _dot(a, b):
    return jnp.dot(a, b, preferred_element_type=F32)


def _dot_tn(a, b):
    return lax.dot_general(a, b, (((0,), (0,)), ((), ())), preferred_element_type=F32)


def _rms_rows(x, g):
    ms = jnp.mean(x * x, axis=-1, keepdims=True)
    return x * lax.rsqrt(ms + NORM_EPS) * g


def _gelu_exact(x):
    return 0.5 * x * (1.0 + lax.erf(x * math.sqrt(0.5)))


def _load_weight_bf16(src_hbm, dst_ref, stage_ref, sem_ref, col_scale=None):
    n_rows, width = src_hbm.shape
    chunk = min(stage_ref.shape[1], n_rows)
    n_chunks = n_rows // chunk
    assert n_chunks * chunk == n_rows and width <= stage_ref.shape[2]

    def copy(c, slot):
        return pltpu.make_async_copy(src_hbm.at[pl.ds(c * chunk, chunk), :],
                                     stage_ref.at[slot, pl.ds(0, chunk), pl.ds(0, width)],
                                     sem_ref.at[slot])

    copy(0, 0).start()

    def body(c, carry):
        slot = lax.rem(c, 2)

        @pl.when(c + 1 < n_chunks)
        def _():
            copy(c + 1, 1 - slot).start()

        copy(c, slot).wait()
        w = stage_ref[slot, pl.ds(0, chunk), pl.ds(0, width)]
        if col_scale is not None:
            w = w * col_scale
        rows = pl.ds(pl.multiple_of(c * chunk, chunk), chunk)
        dst_ref[rows, :] = w.astype(BF16)
        return carry

    lax.fori_loop(0, n_chunks, body, 0)


def _layer_kernel(x_ref, p_ref, g_ref, win_hbm, qscale_ref, cost_ref, sint_ref,
                  cosk_ref, sak_ref, sbk_ref, lng_ref, lnb_ref,
                  ws_ref, bs_ref, ag_ref, gg_ref, wo_hbm, pg_ref,
                  g1_ref, wup_hbm, wdn_hbm, g2_ref, wgate_hbm, bgate_ref, wple_hbm, g3_ref,
                  o_ref,
                  k_scr, vt_scr, km_scr, ug_scr, vn_scr, h1_scr,
                  win_ref, wo_ref, wup_ref, wdn_ref, wgate_ref, wple_ref, wsm_ref, stage, sem,
                  *scratch):
    sel_scr, qz_scr, m_scr, acc_scr, s_scr = (
        scratch[j * N_HEADS:(j + 1) * N_HEADS] for j in range(5))
    nb = k_scr.shape[0]
    tq = MOBA_BLOCK
    n_tiles = pl.num_programs(0) - 1
    step = pl.program_id(0)
    live = step < n_tiles
    i = lax.rem(jnp.minimum(step, n_tiles - 1), nb)
    slot = lax.rem(step, 2)

    @pl.when(step == 0)
    def _():
        km_scr[...] = jnp.zeros(km_scr.shape, F32)
        h1_scr[...] = jnp.zeros(h1_scr.shape, F32)
        _load_weight_bf16(win_hbm, win_ref, stage, sem, col_scale=qscale_ref[...])
        _load_weight_bf16(wo_hbm, wo_ref, stage, sem)
        _load_weight_bf16(wup_hbm, wup_ref, stage, sem)
        _load_weight_bf16(wdn_hbm, wdn_ref, stage, sem)
        _load_weight_bf16(wgate_hbm, wgate_ref, stage, sem)
        _load_weight_bf16(wple_hbm, wple_ref, stage, sem)
        t_idx = lax.broadcasted_iota(jnp.int32, (GMLP_CHUNK, GMLP_CHUNK), 0)
        s_idx = lax.broadcasted_iota(jnp.int32, (GMLP_CHUNK, GMLP_CHUNK), 1)
        for g in range(N_GROUPS):
            tile, part = divmod(g, GROUPS_PER_GATE_TILE)
            wsm_ref[tile, :, part * GMLP_CHUNK:(part + 1) * GMLP_CHUNK] = (
                jnp.where(s_idx <= t_idx, ws_ref[g], 0.0).astype(BF16))

    def scores(blk, h):
        hp = h // HEADS_PER_LANE_TILE
        return _dot(k_scr[blk, :, hp * LANES:(hp + 1) * LANES], qz_scr[h][...])

    def v_rows(blk, h):
        return vt_scr[blk, h * V_ROWS:(h + 1) * V_ROWS, :]

    @pl.when(live)
    def _():
        x = x_ref[0]
        hn = _rms_rows(x, g_ref[...]).astype(BF16)

        gv = _dot(hn, win_ref[:, 3 * D_ATTN + D_GMLP:])
        g2 = _gelu_exact(gv)
        mu = jnp.mean(g2, axis=-1, keepdims=True)
        xc = g2 - mu
        var = jnp.mean(xc * xc, axis=-1, keepdims=True)
        vn = xc * lax.rsqrt(var + NORM_EPS) * lng_ref[...] + lnb_ref[...]
        vn_scr[...] = vn.astype(BF16)

        qt = _dot(hn, win_ref[:, :D_ATTN]).T

        tok = _dot(hn, win_ref[:, D_ATTN:2 * D_ATTN])
        cos_k = cosk_ref[i]
        sa_k = sak_ref[i]
        sb_k = sbk_ref[i]
        kparts = []
        for j in range(D_ATTN // LANES):
            kb = tok[:, j * LANES:(j + 1) * LANES]
            up = pltpu.roll(kb, LANES - ROPE_HALF, axis=1)
            dn = pltpu.roll(kb, ROPE_HALF, axis=1)
            kparts.append(kb * cos_k + up * sa_k + dn * sb_k)
        k = jnp.concatenate(kparts, axis=1)
        k_scr[i] = k.astype(BF16)
        km_scr[pl.ds(i, 1), :] = jnp.mean(k, axis=0, keepdims=True)

        u = _dot(hn, win_ref[:, 3 * D_ATTN:3 * D_ATTN + D_GMLP])
        ug_scr[...] = _gelu_exact(u).astype(BF16)

        vt = _dot(hn, win_ref[:, 2 * D_ATTN:3 * D_ATTN]).T
        ones = jnp.ones((V_ROWS - HEAD_DIM, tq), F32)
        vparts = []
        for h in range(N_HEADS):
            vparts.append(vt[h * HEAD_DIM:(h + 1) * HEAD_DIM])
            vparts.append(ones)
        vt_scr[i] = jnp.concatenate(vparts, axis=0).astype(BF16)

        cos_t = cost_ref[i]
        sin_t = sint_ref[i]
        zeros_head = jnp.zeros((HEAD_DIM, tq), F32)
        for h in range(N_HEADS):
            hh = h % HEADS_PER_LANE_TILE
            r0 = h * HEAD_DIM
            x1 = qt[r0:r0 + ROPE_HALF]
            x2 = qt[r0 + ROPE_HALF:r0 + ROPE_DIM]
            q_h = [x1 * cos_t - x2 * sin_t, x2 * cos_t + x1 * sin_t, qt[r0 + ROPE_DIM:r0 + HEAD_DIM]]
            padded = [zeros_head] * hh + q_h + [zeros_head] * (HEADS_PER_LANE_TILE - 1 - hh)
            qz_scr[h][...] = jnp.concatenate(padded, axis=0).astype(BF16)

    @pl.when(live)
    def _():
        blk_row = lax.broadcasted_iota(jnp.int32, (nb, tq), 0)
        past = blk_row < i
        km = km_scr[...]
        km_hi = km.astype(BF16)
        r1 = km - km_hi.astype(F32)
        km_mid = r1.astype(BF16)
        km_lo = (r1 - km_mid.astype(F32)).astype(BF16)
        km3 = jnp.concatenate([km_hi, km_mid, km_lo], axis=0)
        for h in range(N_HEADS):
            hp = h // HEADS_PER_LANE_TILE
            s_scr[h][...] = scores(0, h)

            g3 = _dot(km3[:, hp * LANES:(hp + 1) * LANES], qz_scr[h][...])
            gate = g3[0:nb] + g3[nb:2 * nb] + g3[2 * nb:3 * nb]
            gate = jnp.where(past, gate, NEG_INF)
            rank = jnp.zeros((nb, tq), jnp.int32)
            for m in range(nb):
                gm = gate[m:m + 1, :]
                beats = (gm > gate) | ((gm == gate) & (m < blk_row))
                rank = rank + beats.astype(jnp.int32)
            sel = past & (rank < MOBA_TOPK)
            sel_scr[h][...] = sel.astype(F32)
            m_scr[h][...] = jnp.full((1, tq), NEG_INF, F32)
            acc_scr[h][...] = jnp.zeros((V_ROWS, tq), F32)

    def past_blocks(kb0, n_blocks):
        units = [(kb0 + j, h) for j in range(n_blocks) for h in range(N_HEADS)]
        ahead = {}
        for u in range(-SCORE_LEAD, len(units)):
            if 0 <= u + SCORE_LEAD < len(units):
                blk, head = units[u + SCORE_LEAD]
                ahead[u + SCORE_LEAD] = scores(blk + 1, head)
            if u < 0:
                continue
            kb, h = units[u]
            s = s_scr[h][...]
            keep = sel_scr[h][pl.ds(kb, 1), :] > 0.5
            m_run = m_scr[h][...]
            m_new = jnp.where(keep, jnp.maximum(m_run, jnp.max(s, axis=0, keepdims=True)), m_run)
            m_sub = jnp.where(keep, m_new, BIG)
            alpha = jnp.exp2(m_run - m_new)
            pb = jnp.exp2((s - m_sub).astype(BF16))
            m_scr[h][...] = m_new
            acc_scr[h][...] = alpha * acc_scr[h][...] + _dot(v_rows(kb, h), pb)
            s_scr[h][...] = ahead.pop(u)

    def two_blocks(t, carry):
        past_blocks(2 * t, 2)
        return carry

    n_past = jnp.where(live, i, 0)
    lax.fori_loop(0, n_past // 2, two_blocks, 0)

    @pl.when(lax.rem(n_past, 2) == 1)
    def _():
        past_blocks(n_past - 1, 1)

    mlp = {}

    def mlp_begin():
        mlp['h'] = h1_scr[1 - slot]
        mlp['hn'] = _rms_rows(mlp['h'], g1_ref[...]).astype(BF16)
        mlp['acc'] = None

    def mlp_up(c):
        up = jnp.maximum(_dot(mlp['hn'], wup_ref[:, c * FF_CHUNK:(c + 1) * FF_CHUNK]), 0.0)
        mlp['up'] = (up * up).astype(BF16)

    def mlp_down(c):
        d = _dot(mlp['up'], wdn_ref[c * FF_CHUNK:(c + 1) * FF_CHUNK, :])
        mlp['acc'] = d if mlp['acc'] is None else mlp['acc'] + d

    def mlp_gate():
        mlp['h2'] = mlp['h'] + _rms_rows(mlp['acc'], g2_ref[...])
        mlp['z'] = _dot(mlp['h2'].astype(BF16), wgate_ref[...]) + bgate_ref[...]

    def mlp_ple():
        mlp['ple'] = _dot(p_ref[0].astype(BF16), wple_ref[...])

    def mlp_end():
        gate = 1.0 / (1.0 + jnp.exp(-mlp['z']))
        o_ref[0] = mlp['h2'] + _rms_rows(mlp['ple'] * gate, g3_ref[...])

    n_ff = D_FF // FF_CHUNK
    heads_per_piece = N_HEADS // (2 * n_ff)
    assert heads_per_piece * 2 * n_ff == N_HEADS
    pieces_before_head = [[mlp_begin]] + [[] for _ in range(N_HEADS - 1)]
    for c in range(n_ff):
        pieces_before_head[2 * c * heads_per_piece].append(functools.partial(mlp_up, c))
        pieces_before_head[(2 * c + 1) * heads_per_piece].append(functools.partial(mlp_down, c))

    kpos = lax.broadcasted_iota(jnp.int32, (MOBA_BLOCK, tq), 0)
    qpos = lax.broadcasted_iota(jnp.int32, (MOBA_BLOCK, tq), 1)
    causal = kpos <= qpos
    lane = lax.broadcasted_iota(jnp.int32, (GMLP_CHUNK, MXU_COLS), 1)
    gate_jobs = [(c, t) for c in range(tq // GMLP_CHUNK) for t in range(D_GMLP // MXU_COLS)]
    heads_per_job = N_HEADS // len(gate_jobs)
    assert heads_per_job * len(gate_jobs) == N_HEADS
    heads_out = []
    cols = [[None] * (D_GMLP // MXU_COLS) for _ in range(tq // GMLP_CHUNK)]
    for h in range(N_HEADS):
        for piece in pieces_before_head[h]:
            piece()
        if h % heads_per_job == 0:
            c, t = gate_jobs[h // heads_per_job]
            vp = vn_scr[c * GMLP_CHUNK:(c + 1) * GMLP_CHUNK, t * MXU_COLS:(t + 1) * MXU_COLS]
            stacked = jnp.concatenate(
                [jnp.where(lane // GROUP_DIM == part, vp, jnp.zeros_like(vp))
                 for part in range(GROUPS_PER_GATE_TILE)], axis=0)
            cols[c][t] = _dot(wsm_ref[t], stacked)

        st = jnp.where(causal, s_scr[h][...], NEG_INF)
        m_run = m_scr[h][...]
        m_new = jnp.maximum(m_run, jnp.max(st, axis=0, keepdims=True))
        alpha = jnp.exp2(m_run - m_new)
        p = jnp.exp2((st - m_new).astype(BF16))
        acc = alpha * acc_scr[h][...] + _dot(v_rows(i, h), p)
        heads_out.append(acc[0:HEAD_DIM] / acc[HEAD_DIM:HEAD_DIM + 1])

    mlp_ple()
    mixed = jnp.concatenate(
        [jnp.concatenate(cols[c], axis=1) + bs_ref[...] for c in range(len(cols))], axis=0)
    gm_out = ug_scr[...].astype(F32) * mixed
    gm_n = _rms_rows(gm_out, gg_ref[...]).astype(BF16)
    y = _dot(gm_n, wo_ref[D_ATTN:, :])
    mlp_gate()

    attn_t = jnp.concatenate(heads_out, axis=0)
    ms_a = jnp.mean(attn_t * attn_t, axis=0, keepdims=True)
    attn_n = (attn_t * lax.rsqrt(ms_a + NORM_EPS) * ag_ref[...]).astype(BF16)
    y = y + _dot_tn(attn_n, wo_ref[:D_ATTN, :])
    h1_scr[slot] = x_ref[0] + _rms_rows(y, pg_ref[...])
    mlp_end()


def _const_spec(shape):
    zeros = (0,) * len(shape)
    return pl.BlockSpec(shape, lambda *_: zeros, pipeline_mode=pl.Buffered(1))


def _rope_tables(seq, block):
    inv_freq = ROPE_THETA ** (-np.arange(ROPE_HALF, dtype=np.float64) / ROPE_HALF)
    ang = np.arange(seq, dtype=np.float64)[:, None] * inv_freq[None, :]
    cos, sin = np.cos(ang), np.sin(ang)
    n_blocks = seq // block
    cos_t, sin_t = (t.T.reshape(ROPE_HALF, n_blocks, block).transpose(1, 0, 2) for t in (cos, sin))
    ones = np.ones((seq, HEAD_DIM - ROPE_DIM))
    zeros = np.zeros((seq, HEAD_DIM - ROPE_DIM))
    zh = np.zeros((seq, ROPE_HALF))
    cos_h = np.concatenate([cos, cos, ones], axis=1)
    sa_h = np.concatenate([-sin, zh, zeros], axis=1)
    sb_h = np.concatenate([zh, sin, zeros], axis=1)
    tile = lambda t: np.concatenate([t] * HEADS_PER_LANE_TILE, axis=1).reshape(n_blocks, block, LANES)
    tables = (cos_t, sin_t, tile(cos_h), tile(sa_h), tile(sb_h))
    return tuple(jnp.asarray(np.ascontiguousarray(t), dtype=F32) for t in tables)


def _layer(h, p, mix_pre_g, w_in, gmlp_ln_g, gmlp_ln_b, w_s, b_s, attn_out_g,
           gmlp_out_g, w_o, mix_post_g, mlp_pre_g, w_up, w_down, mlp_post_g,
           w_ple, w_ple_gate, b_ple_gate, ple_post_g):
    B, S, D = h.shape
    assert D == D_MODEL and S % MOBA_BLOCK == 0
    nb = S // MOBA_BLOCK
    T = MOBA_BLOCK
    n_tiles = B * nb
    row = lambda v: v.reshape(1, -1).astype(F32)

    qscale = np.ones((1, D_IN_PROJ), np.float32)
    qscale[:, :D_ATTN] = HEAD_DIM ** -0.5 * math.log2(math.e)
    cos_t, sin_t, cos_k, sa_k, sb_k = _rope_tables(S, T)
    bs_exp = jnp.repeat(b_s.T, GROUP_DIM, axis=1).astype(F32)

    def cur(s):
        t = jnp.minimum(s, n_tiles - 1)
        return t // nb, t % nb

    def prev(s):
        t = jnp.maximum(s - 1, 0)
        return t // nb, t % nb

    hbm = pl.BlockSpec(memory_space=pl.ANY)
    out = pl.pallas_call(
        _layer_kernel,
        grid=(n_tiles + 1,),
        in_specs=[
            pl.BlockSpec((1, T, D), lambda s: (*cur(s), 0)),
            pl.BlockSpec((1, T, D_PLE), lambda s: (*prev(s), 0)),
            _const_spec((1, D)),
            hbm,
            _const_spec((1, D_IN_PROJ)),
            _const_spec((nb, ROPE_HALF, T)),
            _const_spec((nb, ROPE_HALF, T)),
            _const_spec((nb, T, LANES)),
            _const_spec((nb, T, LANES)),
            _const_spec((nb, T, LANES)),
            _const_spec((1, D_GMLP)),
            _const_spec((1, D_GMLP)),
            _const_spec((N_GROUPS, GMLP_CHUNK, GMLP_CHUNK)),
            _const_spec((GMLP_CHUNK, D_GMLP)),
            _const_spec((D_ATTN, 1)),
            _const_spec((1, D_GMLP)),
            hbm,
            _const_spec((1, D)),
            _const_spec((1, D)),
            hbm,
            hbm,
            _const_spec((1, D)),
            hbm,
            _const_spec((1, D)),
            hbm,
            _const_spec((1, D)),
        ],
        out_specs=pl.BlockSpec((1, T, D), lambda s: (*prev(s), 0)),
        out_shape=jax.ShapeDtypeStruct((B, S, D), F32),
        scratch_shapes=([pltpu.VMEM((nb, T, D_ATTN), BF16),
                         pltpu.VMEM((nb, N_HEADS * V_ROWS, T), BF16),
                         pltpu.VMEM((nb, D_ATTN), F32),
                         pltpu.VMEM((T, D_GMLP), BF16),
                         pltpu.VMEM((T, D_GMLP), BF16),
                         pltpu.VMEM((2, T, D), F32),
                         pltpu.VMEM((D, D_IN_PROJ), BF16),
                         pltpu.VMEM((D_ATTN + D_GMLP, D), BF16),
                         pltpu.VMEM((D, D_FF), BF16),
                         pltpu.VMEM((D_FF, D), BF16),
                         pltpu.VMEM((D, D), BF16),
                         pltpu.VMEM((D_PLE, D), BF16),
                         pltpu.VMEM((D_GMLP // MXU_COLS, GMLP_CHUNK, GROUPS_PER_GATE_TILE * GMLP_CHUNK), BF16),
                         pltpu.VMEM((2, STAGE_ROWS, D_FF), F32),
                         pltpu.SemaphoreType.DMA((2,))]
                        + [pltpu.VMEM((nb, T), F32)] * N_HEADS
                        + [pltpu.VMEM((LANES, T), BF16)] * N_HEADS
                        + [pltpu.VMEM((1, T), F32)] * N_HEADS
                        + [pltpu.VMEM((V_ROWS, T), F32)] * N_HEADS
                        + [pltpu.VMEM((MOBA_BLOCK, T), F32)] * N_HEADS),
        compiler_params=pltpu.CompilerParams(dimension_semantics=("arbitrary",),
                                             vmem_limit_bytes=VMEM_LIMIT_BYTES),
        name="layer",
    )(h, p, row(mix_pre_g), w_in.astype(F32), jnp.asarray(qscale), cos_t, sin_t, cos_k, sa_k, sb_k,
      row(gmlp_ln_g), row(gmlp_ln_b), w_s.astype(F32), bs_exp,
      attn_out_g.reshape(-1, 1).astype(F32), row(gmlp_out_g), w_o.astype(F32), row(mix_post_g),
      row(mlp_pre_g), w_up.astype(F32), w_down.astype(F32), row(mlp_post_g),
      w_ple_gate.astype(F32), row(b_ple_gate), w_ple.astype(F32), row(ple_post_g))
    return out


def kernel(x, p, mix_pre_g, w_in, gmlp_ln_g, gmlp_ln_b, w_s, b_s, attn_out_g,
           gmlp_out_g, w_o, mix_post_g, mlp_pre_g, w_up, w_down, mlp_post_g,
           w_ple, w_ple_gate, b_ple_gate, ple_post_g):
    h = x
    for i in range(w_in.shape[0]):
        h = _layer(h, p[i], mix_pre_g[i], w_in[i], gmlp_ln_g[i], gmlp_ln_b[i],
                   w_s[i], b_s[i], attn_out_g[i], gmlp_out_g[i], w_o[i],
                   mix_post_g[i], mlp_pre_g[i], w_up[i], w_down[i], mlp_post_g[i],
                   w_ple[i], w_ple_gate[i], b_ple_gate[i], ple_post_g[i])
    return h
```

```python
import functools
import math

import jax
import jax.numpy as jnp
import numpy as np
from jax import lax
from jax.experimental import pallas as pl
from jax.experimental.pallas import tpu as pltpu

D_MODEL = 1024
D_PLE = 256
N_HEADS = 8
HEAD_DIM = 64
D_ATTN = N_HEADS * HEAD_DIM
N_GROUPS = 8
GROUP_DIM = 64
D_GMLP = N_GROUPS * GROUP_DIM
D_IN_PROJ = 3 * D_ATTN + 2 * D_GMLP
D_FF = 4 * D_MODEL
MOBA_BLOCK = 256
MOBA_TOPK = 3
GMLP_CHUNK = 128
ROPE_THETA = 500000.0
ROPE_DIM = HEAD_DIM // 4
ROPE_HALF = ROPE_DIM // 2
NORM_EPS = 1e-6
NEG_INF = -1e30

LANES = 128
HEADS_PER_LANE_TILE = LANES // HEAD_DIM
MXU_COLS = 256
GROUPS_PER_GATE_TILE = MXU_COLS // GROUP_DIM
BF16_SUBLANES = 16
V_ROWS = HEAD_DIM + BF16_SUBLANES
BIG = 1e30
SCORE_LEAD = 2
VMEM_LIMIT_BYTES = 58 * 1024 * 1024
FF_CHUNK = 1024
STAGE_ROWS = 128

F32 = jnp.float32
BF16 = jnp.bfloat16


def _dot(a, b):
    return jnp.dot(a, b, preferred_element_type=F32)


def _dot_tn(a, b):
    return lax.dot_general(a, b, (((0,), (0,)), ((), ())), preferred_element_type=F32)


def _rms_rows(x, g):
    ms = jnp.mean(x * x, axis=-1, keepdims=True)
    return x * lax.rsqrt(ms + NORM_EPS) * g


def _gelu_exact(x):
    return 0.5 * x * (1.0 + lax.erf(x * math.sqrt(0.5)))


def _load_weight_bf16(src_hbm, dst_ref, stage_ref, sem_ref, col_scale=None):
    n_rows, width = src_hbm.shape
    chunk = min(stage_ref.shape[1], n_rows)
    n_chunks = n_rows // chunk
    assert n_chunks * chunk == n_rows and width <= stage_ref.shape[2]

    def copy(c, slot):
        return pltpu.make_async_copy(src_hbm.at[pl.ds(c * chunk, chunk), :],
                                     stage_ref.at[slot, pl.ds(0, chunk), pl.ds(0, width)],
                                     sem_ref.at[slot])

    copy(0, 0).start()

    def body(c, carry):
        slot = lax.rem(c, 2)

        @pl.when(c + 1 < n_chunks)
        def _():
            copy(c + 1, 1 - slot).start()

        copy(c, slot).wait()
        w = stage_ref[slot, pl.ds(0, chunk), pl.ds(0, width)]
        if col_scale is not None:
            w = w * col_scale
        rows = pl.ds(pl.multiple_of(c * chunk, chunk), chunk)
        dst_ref[rows, :] = w.astype(BF16)
        return carry

    lax.fori_loop(0, n_chunks, body, 0)


def _layer_kernel(x_ref, p_ref, g_ref, win_hbm, qscale_ref, cost_ref, sint_ref,
                  cosk_ref, sak_ref, sbk_ref, lng_ref, lnb_ref,
                  ws_ref, bs_ref, ag_ref, gg_ref, wo_hbm, pg_ref,
                  g1_ref, wup_hbm, wdn_hbm, g2_ref, wgate_hbm, bgate_ref, wple_hbm, g3_ref,
                  o_ref,
                  k_scr, vt_scr, km_scr, ug_scr, vn_scr, h1_scr,
                  win_ref, wo_ref, wup_ref, wdn_ref, wgate_ref, wple_ref, wsm_ref, stage, sem,
                  *scratch):
    sel_scr, qz_scr, m_scr, acc_scr, s_scr = (
        scratch[j * N_HEADS:(j + 1) * N_HEADS] for j in range(5))
    nb = k_scr.shape[0]
    tq = MOBA_BLOCK
    n_tiles = pl.num_programs(0) - 1
    step = pl.program_id(0)
    live = step < n_tiles
    i = lax.rem(jnp.minimum(step, n_tiles - 1), nb)
    slot = lax.rem(step, 2)

    @pl.when(step == 0)
    def _():
        km_scr[...] = jnp.zeros(km_scr.shape, F32)
        h1_scr[...] = jnp.zeros(h1_scr.shape, F32)
        _load_weight_bf16(win_hbm, win_ref, stage, sem, col_scale=qscale_ref[...])
        _load_weight_bf16(wo_hbm, wo_ref, stage, sem)
        _load_weight_bf16(wup_hbm, wup_ref, stage, sem)
        _load_weight_bf16(wdn_hbm, wdn_ref, stage, sem)
        _load_weight_bf16(wgate_hbm, wgate_ref, stage, sem)
        _load_weight_bf16(wple_hbm, wple_ref, stage, sem)
        t_idx = lax.broadcasted_iota(jnp.int32, (GMLP_CHUNK, GMLP_CHUNK), 0)
        s_idx = lax.broadcasted_iota(jnp.int32, (GMLP_CHUNK, GMLP_CHUNK), 1)
        for g in range(N_GROUPS):
            tile, part = divmod(g, GROUPS_PER_GATE_TILE)
            wsm_ref[tile, :, part * GMLP_CHUNK:(part + 1) * GMLP_CHUNK] = (
                jnp.where(s_idx <= t_idx, ws_ref[g], 0.0).astype(BF16))

    def scores(blk, h):
        hp = h // HEADS_PER_LANE_TILE
        return _dot(k_scr[blk, :, hp * LANES:(hp + 1) * LANES], qz_scr[h][...])

    def v_rows(blk, h):
        return vt_scr[blk, h * V_ROWS:(h + 1) * V_ROWS, :]

    @pl.when(live)
    def _():
        x = x_ref[0]
        hn = _rms_rows(x, g_ref[...]).astype(BF16)

        gv = _dot(hn, win_ref[:, 3 * D_ATTN + D_GMLP:])
        g2 = _gelu_exact(gv)
        mu = jnp.mean(g2, axis=-1, keepdims=True)
        xc = g2 - mu
        var = jnp.mean(xc * xc, axis=-1, keepdims=True)
        vn = xc * lax.rsqrt(var + NORM_EPS) * lng_ref[...] + lnb_ref[...]
        vn_scr[...] = vn.astype(BF16)

        qt = _dot(hn, win_ref[:, :D_ATTN]).T

        tok = _dot(hn, win_ref[:, D_ATTN:2 * D_ATTN])
        cos_k = cosk_ref[i]
        sa_k = sak_ref[i]
        sb_k = sbk_ref[i]
        kparts = []
        for j in range(D_ATTN // LANES):
            kb = tok[:, j * LANES:(j + 1) * LANES]
            up = pltpu.roll(kb, LANES - ROPE_HALF, axis=1)
            dn = pltpu.roll(kb, ROPE_HALF, axis=1)
            kparts.append(kb * cos_k + up * sa_k + dn * sb_k)
        k = jnp.concatenate(kparts, axis=1)
        k_scr[i] = k.astype(BF16)
        km_scr[pl.ds(i, 1), :] = jnp.mean(k, axis=0, keepdims=True)

        u = _dot(hn, win_ref[:, 3 * D_ATTN:3 * D_ATTN + D_GMLP])
        ug_scr[...] = _gelu_exact(u).astype(BF16)

        vt = _dot(hn, win_ref[:, 2 * D_ATTN:3 * D_ATTN]).T
        ones = jnp.ones((V_ROWS - HEAD_DIM, tq), F32)
        vparts = []
        for h in range(N_HEADS):
            vparts.append(vt[h * HEAD_DIM:(h + 1) * HEAD_DIM])
            vparts.append(ones)
        vt_scr[i] = jnp.concatenate(vparts, axis=0).astype(BF16)

        cos_t = cost_ref[i]
        sin_t = sint_ref[i]
        zeros_head = jnp.zeros((HEAD_DIM, tq), F32)
        for h in range(N_HEADS):
            hh = h % HEADS_PER_LANE_TILE
            r0 = h * HEAD_DIM
            x1 = qt[r0:r0 + ROPE_HALF]
            x2 = qt[r0 + ROPE_HALF:r0 + ROPE_DIM]
            q_h = [x1 * cos_t - x2 * sin_t, x2 * cos_t + x1 * sin_t, qt[r0 + ROPE_DIM:r0 + HEAD_DIM]]
            padded = [zeros_head] * hh + q_h + [zeros_head] * (HEADS_PER_LANE_TILE - 1 - hh)
            qz_scr[h][...] = jnp.concatenate(padded, axis=0).astype(BF16)

    @pl.when(live)
    def _():
        blk_row = lax.broadcasted_iota(jnp.int32, (nb, tq), 0)
        past = blk_row < i
        km = km_scr[...]
        km_hi = km.astype(BF16)
        r1 = km - km_hi.astype(F32)
        km_mid = r1.astype(BF16)
        km_lo = (r1 - km_mid.astype(F32)).astype(BF16)
        pad = jnp.zeros((-3 * nb % BF16_SUBLANES, D_ATTN), BF16)
        km3 = jnp.concatenate([km_hi, km_mid, km_lo, pad], axis=0)
        for h in range(N_HEADS):
            lanes = slice(h // HEADS_PER_LANE_TILE * LANES, (h // HEADS_PER_LANE_TILE + 1) * LANES)
            both = _dot(jnp.concatenate([k_scr[0, :, lanes], km3[:, lanes]], axis=0), qz_scr[h][...])
            s_scr[h][...] = both[:MOBA_BLOCK]

            g3 = both[MOBA_BLOCK:]
            gate = g3[0:nb] + g3[nb:2 * nb] + g3[2 * nb:3 * nb]
            gate = jnp.where(past, gate, NEG_INF)
            rank = jnp.zeros((nb, tq), jnp.int32)
            for m in range(nb):
                gm = gate[m:m + 1, :]
                beats = (gm > gate) | ((gm == gate) & (m < blk_row))
                rank = rank + beats.astype(jnp.int32)
            sel = past & (rank < MOBA_TOPK)
            sel_scr[h][...] = sel.astype(F32)
            m_scr[h][...] = jnp.full((1, tq), NEG_INF, F32)
            acc_scr[h][...] = jnp.zeros((V_ROWS, tq), F32)

    def past_blocks(kb0, n_blocks):
        units = [(kb0 + j, h) for j in range(n_blocks) for h in range(N_HEADS)]
        ahead = {}
        for u in range(-SCORE_LEAD, len(units)):
            if 0 <= u + SCORE_LEAD < len(units):
                blk, head = units[u + SCORE_LEAD]
                ahead[u + SCORE_LEAD] = scores(blk + 1, head)
            if u < 0:
                continue
            kb, h = units[u]
            s = s_scr[h][...]
            keep = sel_scr[h][pl.ds(kb, 1), :] > 0.5
            m_run = m_scr[h][...]
            m_new = jnp.where(keep, jnp.maximum(m_run, jnp.max(s, axis=0, keepdims=True)), m_run)
            m_sub = jnp.where(keep, m_new, BIG)
            alpha = jnp.exp2(m_run - m_new)
            pb = jnp.exp2((s - m_sub).astype(BF16))
            m_scr[h][...] = m_new
            acc_scr[h][...] = alpha * acc_scr[h][...] + _dot(v_rows(kb, h), pb)
            s_scr[h][...] = ahead.pop(u)

    def two_blocks(t, carry):
        past_blocks(2 * t, 2)
        return carry

    n_past = jnp.where(live, i, 0)
    lax.fori_loop(0, n_past // 2, two_blocks, 0)

    @pl.when(lax.rem(n_past, 2) == 1)
    def _():
        past_blocks(n_past - 1, 1)

    mlp = {}

    def mlp_begin():
        mlp['h'] = h1_scr[1 - slot]
        mlp['hn'] = _rms_rows(mlp['h'], g1_ref[...]).astype(BF16)
        mlp['acc'] = None

    def mlp_up(c):
        up = jnp.maximum(_dot(mlp['hn'], wup_ref[:, c * FF_CHUNK:(c + 1) * FF_CHUNK]), 0.0)
        mlp['up'] = (up * up).astype(BF16)

    def mlp_down(c):
        d = _dot(mlp['up'], wdn_ref[c * FF_CHUNK:(c + 1) * FF_CHUNK, :])
        mlp['acc'] = d if mlp['acc'] is None else mlp['acc'] + d

    def mlp_gate():
        mlp['h2'] = mlp['h'] + _rms_rows(mlp['acc'], g2_ref[...])
        mlp['z'] = _dot(mlp['h2'].astype(BF16), wgate_ref[...]) + bgate_ref[...]

    def mlp_ple():
        mlp['ple'] = _dot(p_ref[0].astype(BF16), wple_ref[...])

    def mlp_end():
        gate = 1.0 / (1.0 + jnp.exp(-mlp['z']))
        o_ref[0] = mlp['h2'] + _rms_rows(mlp['ple'] * gate, g3_ref[...])

    n_ff = D_FF // FF_CHUNK
    heads_per_piece = N_HEADS // (2 * n_ff)
    assert heads_per_piece * 2 * n_ff == N_HEADS
    pieces_before_head = [[mlp_begin]] + [[] for _ in range(N_HEADS - 1)]
    for c in range(n_ff):
        pieces_before_head[2 * c * heads_per_piece].append(functools.partial(mlp_up, c))
        pieces_before_head[(2 * c + 1) * heads_per_piece].append(functools.partial(mlp_down, c))

    kpos = lax.broadcasted_iota(jnp.int32, (MOBA_BLOCK, tq), 0)
    qpos = lax.broadcasted_iota(jnp.int32, (MOBA_BLOCK, tq), 1)
    causal = kpos <= qpos
    lane = lax.broadcasted_iota(jnp.int32, (GMLP_CHUNK, MXU_COLS), 1)
    gate_jobs = [(c, t) for c in range(tq // GMLP_CHUNK) for t in range(D_GMLP // MXU_COLS)]
    heads_per_job = N_HEADS // len(gate_jobs)
    assert heads_per_job * len(gate_jobs) == N_HEADS
    heads_out = []
    cols = [[None] * (D_GMLP // MXU_COLS) for _ in range(tq // GMLP_CHUNK)]
    for h in range(N_HEADS):
        for piece in pieces_before_head[h]:
            piece()
        if h % heads_per_job == 0:
            c, t = gate_jobs[h // heads_per_job]
            vp = vn_scr[c * GMLP_CHUNK:(c + 1) * GMLP_CHUNK, t * MXU_COLS:(t + 1) * MXU_COLS]
            stacked = jnp.concatenate(
                [jnp.where(lane // GROUP_DIM == part, vp, jnp.zeros_like(vp))
                 for part in range(GROUPS_PER_GATE_TILE)], axis=0)
            cols[c][t] = _dot(wsm_ref[t], stacked)

        st = jnp.where(causal, s_scr[h][...], NEG_INF)
        m_run = m_scr[h][...]
        m_new = jnp.maximum(m_run, jnp.max(st, axis=0, keepdims=True))
        alpha = jnp.exp2(m_run - m_new)
        p = jnp.exp2((st - m_new).astype(BF16))
        acc = alpha * acc_scr[h][...] + _dot(v_rows(i, h), p)
        heads_out.append(acc[0:HEAD_DIM] / acc[HEAD_DIM:HEAD_DIM + 1])

    mlp_ple()
    mixed = jnp.concatenate(
        [jnp.concatenate(cols[c], axis=1) + bs_ref[...] for c in range(len(cols))], axis=0)
    gm_out = ug_scr[...].astype(F32) * mixed
    gm_n = _rms_rows(gm_out, gg_ref[...]).astype(BF16)
    y = _dot(gm_n, wo_ref[D_ATTN:, :])
    mlp_gate()

    attn_t = jnp.concatenate(heads_out, axis=0)
    ms_a = jnp.mean(attn_t * attn_t, axis=0, keepdims=True)
    attn_n = (attn_t * lax.rsqrt(ms_a + NORM_EPS) * ag_ref[...]).astype(BF16)
    y = y + _dot_tn(attn_n, wo_ref[:D_ATTN, :])
    h1_scr[slot] = x_ref[0] + _rms_rows(y, pg_ref[...])
    mlp_end()


def _const_spec(shape):
    zeros = (0,) * len(shape)
    return pl.BlockSpec(shape, lambda *_: zeros, pipeline_mode=pl.Buffered(1))


def _rope_tables(seq, block):
    inv_freq = ROPE_THETA ** (-np.arange(ROPE_HALF, dtype=np.float64) / ROPE_HALF)
    ang = np.arange(seq, dtype=np.float64)[:, None] * inv_freq[None, :]
    cos, sin = np.cos(ang), np.sin(ang)
    n_blocks = seq // block
    cos_t, sin_t = (t.T.reshape(ROPE_HALF, n_blocks, block).transpose(1, 0, 2) for t in (cos, sin))
    ones = np.ones((seq, HEAD_DIM - ROPE_DIM))
    zeros = np.zeros((seq, HEAD_DIM - ROPE_DIM))
    zh = np.zeros((seq, ROPE_HALF))
    cos_h = np.concatenate([cos, cos, ones], axis=1)
    sa_h = np.concatenate([-sin, zh, zeros], axis=1)
    sb_h = np.concatenate([zh, sin, zeros], axis=1)
    tile = lambda t: np.concatenate([t] * HEADS_PER_LANE_TILE, axis=1).reshape(n_blocks, block, LANES)
    tables = (cos_t, sin_t, tile(cos_h), tile(sa_h), tile(sb_h))
    return tuple(jnp.asarray(np.ascontiguousarray(t), dtype=F32) for t in tables)


def _layer(h, p, mix_pre_g, w_in, gmlp_ln_g, gmlp_ln_b, w_s, b_s, attn_out_g,
           gmlp_out_g, w_o, mix_post_g, mlp_pre_g, w_up, w_down, mlp_post_g,
           w_ple, w_ple_gate, b_ple_gate, ple_post_g):
    B, S, D = h.shape
    assert D == D_MODEL and S % MOBA_BLOCK == 0
    nb = S // MOBA_BLOCK
    T = MOBA_BLOCK
    n_tiles = B * nb
    row = lambda v: v.reshape(1, -1).astype(F32)

    qscale = np.ones((1, D_IN_PROJ), np.float32)
    qscale[:, :D_ATTN] = HEAD_DIM ** -0.5 * math.log2(math.e)
    cos_t, sin_t, cos_k, sa_k, sb_k = _rope_tables(S, T)
    bs_exp = jnp.repeat(b_s.T, GROUP_DIM, axis=1).astype(F32)

    def cur(s):
        t = jnp.minimum(s, n_tiles - 1)
        return t // nb, t % nb

    def prev(s):
        t = jnp.maximum(s - 1, 0)
        return t // nb, t % nb

    hbm = pl.BlockSpec(memory_space=pl.ANY)
    out = pl.pallas_call(
        _layer_kernel,
        grid=(n_tiles + 1,),
        in_specs=[
            pl.BlockSpec((1, T, D), lambda s: (*cur(s), 0)),
            pl.BlockSpec((1, T, D_PLE), lambda s: (*prev(s), 0)),
            _const_spec((1, D)),
            hbm,
            _const_spec((1, D_IN_PROJ)),
            _const_spec((nb, ROPE_HALF, T)),
            _const_spec((nb, ROPE_HALF, T)),
            _const_spec((nb, T, LANES)),
            _const_spec((nb, T, LANES)),
            _const_spec((nb, T, LANES)),
            _const_spec((1, D_GMLP)),
            _const_spec((1, D_GMLP)),
            _const_spec((N_GROUPS, GMLP_CHUNK, GMLP_CHUNK)),
            _const_spec((GMLP_CHUNK, D_GMLP)),
            _const_spec((D_ATTN, 1)),
            _const_spec((1, D_GMLP)),
            hbm,
            _const_spec((1, D)),
            _const_spec((1, D)),
            hbm,
            hbm,
            _const_spec((1, D)),
            hbm,
            _const_spec((1, D)),
            hbm,
            _const_spec((1, D)),
        ],
        out_specs=pl.BlockSpec((1, T, D), lambda s: (*prev(s), 0)),
        out_shape=jax.ShapeDtypeStruct((B, S, D), F32),
        scratch_shapes=([pltpu.VMEM((nb, T, D_ATTN), BF16),
                         pltpu.VMEM((nb, N_HEADS * V_ROWS, T), BF16),
                         pltpu.VMEM((nb, D_ATTN), F32),
                         pltpu.VMEM((T, D_GMLP), BF16),
                         pltpu.VMEM((T, D_GMLP), BF16),
                         pltpu.VMEM((2, T, D), F32),
                         pltpu.VMEM((D, D_IN_PROJ), BF16),
                         pltpu.VMEM((D_ATTN + D_GMLP, D), BF16),
                         pltpu.VMEM((D, D_FF), BF16),
                         pltpu.VMEM((D_FF, D), BF16),
                         pltpu.VMEM((D, D), BF16),
                         pltpu.VMEM((D_PLE, D), BF16),
                         pltpu.VMEM((D_GMLP // MXU_COLS, GMLP_CHUNK, GROUPS_PER_GATE_TILE * GMLP_CHUNK), BF16),
                         pltpu.VMEM((2, STAGE_ROWS, D_FF), F32),
                         pltpu.SemaphoreType.DMA((2,))]
                        + [pltpu.VMEM((nb, T), F32)] * N_HEADS
                        + [pltpu.VMEM((LANES, T), BF16)] * N_HEADS
                        + [pltpu.VMEM((1, T), F32)] * N_HEADS
                        + [pltpu.VMEM((V_ROWS, T), F32)] * N_HEADS
                        + [pltpu.VMEM((MOBA_BLOCK, T), F32)] * N_HEADS),
        compiler_params=pltpu.CompilerParams(dimension_semantics=("arbitrary",),
                                             vmem_limit_bytes=VMEM_LIMIT_BYTES),
        name="layer",
    )(h, p, row(mix_pre_g), w_in.astype(F32), jnp.asarray(qscale), cos_t, sin_t, cos_k, sa_k, sb_k,
      row(gmlp_ln_g), row(gmlp_ln_b), w_s.astype(F32), bs_exp,
      attn_out_g.reshape(-1, 1).astype(F32), row(gmlp_out_g), w_o.astype(F32), row(mix_post_g),
      row(mlp_pre_g), w_up.astype(F32), w_down.astype(F32), row(mlp_post_g),
      w_ple_gate.astype(F32), row(b_ple_gate), w_ple.astype(F32), row(ple_post_g))
    return out


def kernel(x, p, mix_pre_g, w_in, gmlp_ln_g, gmlp_ln_b, w_s, b_s, attn_out_g,
           gmlp_out_g, w_o, mix_post_g, mlp_pre_g, w_up, w_down, mlp_post_g,
           w_ple, w_ple_gate, b_ple_gate, ple_post_g):
    h = x
    for i in range(w_in.shape[0]):
        h = _layer(h, p[i], mix_pre_g[i], w_in[i], gmlp_ln_g[i], gmlp_ln_b[i],
                   w_s[i], b_s[i], attn_out_g[i], gmlp_out_g[i], w_o[i],
                   mix_post_g[i], mlp_pre_g[i], w_up[i], w_down[i], mlp_post_g[i],
                   w_ple[i], w_ple_gate[i], b_ple_gate[i], ple_post_g[i])
    return h
```

```python
import functools
import math

import jax
import jax.numpy as jnp
import numpy as np
from jax import lax
from jax.experimental import pallas as pl
from jax.experimental.pallas import tpu as pltpu

D_MODEL = 1024
D_PLE = 256
N_HEADS = 8
HEAD_DIM = 64
D_ATTN = N_HEADS * HEAD_DIM
N_GROUPS = 8
GROUP_DIM = 64
D_GMLP = N_GROUPS * GROUP_DIM
D_IN_PROJ = 3 * D_ATTN + 2 * D_GMLP
D_FF = 4 * D_MODEL
MOBA_BLOCK = 256
MOBA_TOPK = 3
GMLP_CHUNK = 128
ROPE_THETA = 500000.0
ROPE_DIM = HEAD_DIM // 4
ROPE_HALF = ROPE_DIM // 2
NORM_EPS = 1e-6
NEG_INF = -1e30

LANES = 128
HEADS_PER_LANE_TILE = LANES // HEAD_DIM
MXU_COLS = 256
GROUPS_PER_GATE_TILE = MXU_COLS // GROUP_DIM
BF16_SUBLANES = 16
V_ROWS = HEAD_DIM + BF16_SUBLANES
BIG = 1e30
SCORE_LEAD = 2
VMEM_LIMIT_BYTES = 58 * 1024 * 1024
FF_CHUNK = 1024
STAGE_ROWS = 128

F32 = jnp.float32
BF16 = jnp.bfloat16


def _dot(a, b):
    return jnp.dot(a, b, preferred_element_type=F32)


def _dot_tn(a, b):
    return lax.dot_general(a, b, (((0,), (0,)), ((), ())), preferred_element_type=F32)


def _rms_rows(x, g):
    ms = jnp.mean(x * x, axis=-1, keepdims=True)
    return x * lax.rsqrt(ms + NORM_EPS) * g


def _gelu_exact(x):
    return 0.5 * x * (1.0 + lax.erf(x * math.sqrt(0.5)))


def _load_weight_bf16(src_hbm, dst_ref, stage_ref, sem_ref, col_scale=None):
    n_rows, width = src_hbm.shape
    chunk = min(stage_ref.shape[1], n_rows)
    n_chunks = n_rows // chunk
    assert n_chunks * chunk == n_rows and width <= stage_ref.shape[2]

    def copy(c, slot):
        return pltpu.make_async_copy(src_hbm.at[pl.ds(c * chunk, chunk), :],
                                     stage_ref.at[slot, pl.ds(0, chunk), pl.ds(0, width)],
                                     sem_ref.at[slot])

    copy(0, 0).start()

    def body(c, carry):
        slot = lax.rem(c, 2)

        @pl.when(c + 1 < n_chunks)
        def _():
            copy(c + 1, 1 - slot).start()

        copy(c, slot).wait()
        w = stage_ref[slot, pl.ds(0, chunk), pl.ds(0, width)]
        if col_scale is not None:
            w = w * col_scale
        rows = pl.ds(pl.multiple_of(c * chunk, chunk), chunk)
        dst_ref[rows, :] = w.astype(BF16)
        return carry

    lax.fori_loop(0, n_chunks, body, 0)


def _layer_kernel(x_ref, p_ref, g_ref, win_hbm, qscale_ref, cost_ref, sint_ref,
                  cosk_ref, sak_ref, sbk_ref, lng_ref, lnb_ref,
                  ws_ref, bs_ref, ag_ref, gg_ref, wo_hbm, pg_ref,
                  g1_ref, wup_hbm, wdn_hbm, g2_ref, wgate_hbm, bgate_ref, wple_hbm, g3_ref,
                  o_ref,
                  k_scr, vt_scr, km_scr, ug_scr, vn_scr, h1_scr,
                  win_ref, wo_ref, wup_ref, wdn_ref, wgate_ref, wple_ref, wsm_ref, stage, sem,
                  *scratch):
    sel_scr, qz_scr, acc_scr, s_scr, m_scr = (
        scratch[j * N_HEADS:(j + 1) * N_HEADS] for j in range(5))
    nb = k_scr.shape[0]
    tq = MOBA_BLOCK
    n_tiles = pl.num_programs(0) - 1
    step = pl.program_id(0)
    live = step < n_tiles
    i = lax.rem(jnp.minimum(step, n_tiles - 1), nb)
    slot = lax.rem(step, 2)

    @pl.when(step == 0)
    def _():
        km_scr[...] = jnp.zeros(km_scr.shape, F32)
        h1_scr[...] = jnp.zeros(h1_scr.shape, F32)
        _load_weight_bf16(win_hbm, win_ref, stage, sem, col_scale=qscale_ref[...])
        _load_weight_bf16(wo_hbm, wo_ref, stage, sem)
        _load_weight_bf16(wup_hbm, wup_ref, stage, sem)
        _load_weight_bf16(wdn_hbm, wdn_ref, stage, sem)
        _load_weight_bf16(wgate_hbm, wgate_ref, stage, sem)
        _load_weight_bf16(wple_hbm, wple_ref, stage, sem)
        t_idx = lax.broadcasted_iota(jnp.int32, (GMLP_CHUNK, GMLP_CHUNK), 0)
        s_idx = lax.broadcasted_iota(jnp.int32, (GMLP_CHUNK, GMLP_CHUNK), 1)
        for g in range(N_GROUPS):
            tile, part = divmod(g, GROUPS_PER_GATE_TILE)
            wsm_ref[tile, :, part * GMLP_CHUNK:(part + 1) * GMLP_CHUNK] = (
                jnp.where(s_idx <= t_idx, ws_ref[g], 0.0).astype(BF16))

    def scores(blk, h):
        hp = h // HEADS_PER_LANE_TILE
        return _dot(k_scr[blk, :, hp * LANES:(hp + 1) * LANES], qz_scr[h][...])

    def v_rows(blk, h):
        return vt_scr[blk, h * V_ROWS:(h + 1) * V_ROWS, :]

    @pl.when(live)
    def _():
        x = x_ref[0]
        hn = _rms_rows(x, g_ref[...]).astype(BF16)

        gv = _dot(hn, win_ref[:, 3 * D_ATTN + D_GMLP:])
        g2 = _gelu_exact(gv)
        mu = jnp.mean(g2, axis=-1, keepdims=True)
        xc = g2 - mu
        var = jnp.mean(xc * xc, axis=-1, keepdims=True)
        vn = xc * lax.rsqrt(var + NORM_EPS) * lng_ref[...] + lnb_ref[...]
        vn_scr[...] = vn.astype(BF16)

        qt = _dot(hn, win_ref[:, :D_ATTN]).T

        tok = _dot(hn, win_ref[:, D_ATTN:2 * D_ATTN])
        cos_k = cosk_ref[i]
        sa_k = sak_ref[i]
        sb_k = sbk_ref[i]
        kparts = []
        for j in range(D_ATTN // LANES):
            kb = tok[:, j * LANES:(j + 1) * LANES]
            up = pltpu.roll(kb, LANES - ROPE_HALF, axis=1)
            dn = pltpu.roll(kb, ROPE_HALF, axis=1)
            kparts.append(kb * cos_k + up * sa_k + dn * sb_k)
        k = jnp.concatenate(kparts, axis=1)
        k_scr[i] = k.astype(BF16)
        km_scr[pl.ds(i, 1), :] = jnp.mean(k, axis=0, keepdims=True)

        u = _dot(hn, win_ref[:, 3 * D_ATTN:3 * D_ATTN + D_GMLP])
        ug_scr[...] = _gelu_exact(u).astype(BF16)

        vt = _dot(hn, win_ref[:, 2 * D_ATTN:3 * D_ATTN]).T
        ones = jnp.ones((V_ROWS - HEAD_DIM, tq), F32)
        vparts = []
        for h in range(N_HEADS):
            vparts.append(vt[h * HEAD_DIM:(h + 1) * HEAD_DIM])
            vparts.append(ones)
        vt_scr[i] = jnp.concatenate(vparts, axis=0).astype(BF16)

        cos_t = cost_ref[i]
        sin_t = sint_ref[i]
        zeros_head = jnp.zeros((HEAD_DIM, tq), F32)
        for h in range(N_HEADS):
            hh = h % HEADS_PER_LANE_TILE
            r0 = h * HEAD_DIM
            x1 = qt[r0:r0 + ROPE_HALF]
            x2 = qt[r0 + ROPE_HALF:r0 + ROPE_DIM]
            q_h = [x1 * cos_t - x2 * sin_t, x2 * cos_t + x1 * sin_t, qt[r0 + ROPE_DIM:r0 + HEAD_DIM]]
            padded = [zeros_head] * hh + q_h + [zeros_head] * (HEADS_PER_LANE_TILE - 1 - hh)
            qz_scr[h][...] = jnp.concatenate(padded, axis=0).astype(BF16)

    @pl.when(live)
    def _():
        blk_row = lax.broadcasted_iota(jnp.int32, (nb, tq), 0)
        past = blk_row < i
        km = km_scr[...]
        km_hi = km.astype(BF16)
        r1 = km - km_hi.astype(F32)
        km_mid = r1.astype(BF16)
        km_lo = (r1 - km_mid.astype(F32)).astype(BF16)
        pad = jnp.zeros((-3 * nb % BF16_SUBLANES, D_ATTN), BF16)
        km3 = jnp.concatenate([km_hi, km_mid, km_lo, pad], axis=0)
        for h in range(N_HEADS):
            lanes = slice(h // HEADS_PER_LANE_TILE * LANES, (h // HEADS_PER_LANE_TILE + 1) * LANES)
            both = _dot(jnp.concatenate([k_scr[0, :, lanes], km3[:, lanes]], axis=0), qz_scr[h][...])
            s_scr[h][...] = both[:MOBA_BLOCK]

            g3 = both[MOBA_BLOCK:]
            gate = g3[0:nb] + g3[nb:2 * nb] + g3[2 * nb:3 * nb]
            gate = jnp.where(past, gate, NEG_INF)
            rank = jnp.zeros((nb, tq), jnp.int32)
            for m in range(nb):
                gm = gate[m:m + 1, :]
                beats = (gm > gate) | ((gm == gate) & (m < blk_row))
                rank = rank + beats.astype(jnp.int32)
            sel = past & (rank < MOBA_TOPK)
            sel_scr[h][...] = sel.astype(F32)
            m_scr[h][...] = jnp.full((1, tq), NEG_INF, F32)
            acc_scr[h][...] = jnp.zeros((V_ROWS, tq), F32)

    def past_blocks(kb0, n_blocks):
        units = [(kb0 + j, h) for j in range(n_blocks) for h in range(N_HEADS)]
        ahead = {}
        for u in range(-SCORE_LEAD, len(units)):
            if 0 <= u + SCORE_LEAD < len(units):
                blk, head = units[u + SCORE_LEAD]
                ahead[u + SCORE_LEAD] = scores(blk + 1, head)
            if u < 0:
                continue
            kb, h = units[u]
            s = s_scr[h][...]
            keep = sel_scr[h][pl.ds(kb, 1), :] > 0.5
            m_run = m_scr[h][...]
            m_new = jnp.where(keep, jnp.maximum(m_run, jnp.max(s, axis=0, keepdims=True)), m_run)
            m_sub = jnp.where(keep, m_new, BIG)
            alpha = jnp.exp2(m_run - m_new)
            pb = jnp.exp2((s - m_sub).astype(BF16))
            m_scr[h][...] = m_new
            acc_scr[h][...] = alpha * acc_scr[h][...] + _dot(v_rows(kb, h), pb)
            s_scr[h][...] = ahead.pop(u)

    def two_blocks(t, carry):
        past_blocks(2 * t, 2)
        return carry

    n_past = jnp.where(live, i, 0)
    lax.fori_loop(0, n_past // 2, two_blocks, 0)

    @pl.when(lax.rem(n_past, 2) == 1)
    def _():
        past_blocks(n_past - 1, 1)

    mlp = {}

    def mlp_begin():
        mlp['h'] = h1_scr[1 - slot]
        mlp['hn'] = _rms_rows(mlp['h'], g1_ref[...]).astype(BF16)
        mlp['acc'] = None

    def mlp_up(c):
        up = jnp.maximum(_dot(mlp['hn'], wup_ref[:, c * FF_CHUNK:(c + 1) * FF_CHUNK]), 0.0)
        mlp['up'] = (up * up).astype(BF16)

    def mlp_down(c):
        d = _dot(mlp['up'], wdn_ref[c * FF_CHUNK:(c + 1) * FF_CHUNK, :])
        mlp['acc'] = d if mlp['acc'] is None else mlp['acc'] + d

    def mlp_gate():
        mlp['h2'] = mlp['h'] + _rms_rows(mlp['acc'], g2_ref[...])
        mlp['z'] = _dot(mlp['h2'].astype(BF16), wgate_ref[...]) + bgate_ref[...]

    def mlp_ple():
        mlp['ple'] = _dot(p_ref[0].astype(BF16), wple_ref[...])

    def mlp_end():
        gate = 1.0 / (1.0 + jnp.exp(-mlp['z']))
        o_ref[0] = mlp['h2'] + _rms_rows(mlp['ple'] * gate, g3_ref[...])

    n_ff = D_FF // FF_CHUNK
    heads_per_piece = N_HEADS // (2 * n_ff)
    assert heads_per_piece * 2 * n_ff == N_HEADS
    pieces_before_head = [[mlp_begin]] + [[] for _ in range(N_HEADS - 1)]
    for c in range(n_ff):
        pieces_before_head[2 * c * heads_per_piece].append(functools.partial(mlp_up, c))
        pieces_before_head[(2 * c + 1) * heads_per_piece].append(functools.partial(mlp_down, c))

    kpos = lax.broadcasted_iota(jnp.int32, (MOBA_BLOCK, tq), 0)
    qpos = lax.broadcasted_iota(jnp.int32, (MOBA_BLOCK, tq), 1)
    causal = kpos <= qpos
    lane = lax.broadcasted_iota(jnp.int32, (GMLP_CHUNK, MXU_COLS), 1)
    gate_jobs = [(c, t) for c in range(tq // GMLP_CHUNK) for t in range(D_GMLP // MXU_COLS)]
    heads_per_job = N_HEADS // len(gate_jobs)
    assert heads_per_job * len(gate_jobs) == N_HEADS
    heads_out = []
    cols = [[None] * (D_GMLP // MXU_COLS) for _ in range(tq // GMLP_CHUNK)]
    for h in range(N_HEADS):
        for piece in pieces_before_head[h]:
            piece()
        if h % heads_per_job == 0:
            c, t = gate_jobs[h // heads_per_job]
            vp = vn_scr[c * GMLP_CHUNK:(c + 1) * GMLP_CHUNK, t * MXU_COLS:(t + 1) * MXU_COLS]
            stacked = jnp.concatenate(
                [jnp.where(lane // GROUP_DIM == part, vp, jnp.zeros_like(vp))
                 for part in range(GROUPS_PER_GATE_TILE)], axis=0)
            cols[c][t] = _dot(wsm_ref[t], stacked)

        st = jnp.where(causal, s_scr[h][...], NEG_INF)
        m_run = m_scr[h][...]
        m_new = jnp.maximum(m_run, jnp.max(st, axis=0, keepdims=True))
        alpha = jnp.exp2(m_run - m_new)
        p = jnp.exp2((st - m_new).astype(BF16))
        acc = alpha * acc_scr[h][...] + _dot(v_rows(i, h), p)
        heads_out.append(acc[0:HEAD_DIM] / acc[HEAD_DIM:HEAD_DIM + 1])

    mlp_ple()
    mixed = jnp.concatenate(
        [jnp.concatenate(cols[c], axis=1) + bs_ref[...] for c in range(len(cols))], axis=0)
    gm_out = ug_scr[...].astype(F32) * mixed
    gm_n = _rms_rows(gm_out, gg_ref[...]).astype(BF16)
    y = _dot(gm_n, wo_ref[D_ATTN:, :])
    mlp_gate()

    attn_t = jnp.concatenate(heads_out, axis=0)
    ms_a = jnp.mean(attn_t * attn_t, axis=0, keepdims=True)
    attn_n = (attn_t * lax.rsqrt(ms_a + NORM_EPS) * ag_ref[...]).astype(BF16)
    y = y + _dot_tn(attn_n, wo_ref[:D_ATTN, :])
    h1_scr[slot] = x_ref[0] + _rms_rows(y, pg_ref[...])
    mlp_end()


def _const_spec(shape):
    zeros = (0,) * len(shape)
    return pl.BlockSpec(shape, lambda *_: zeros, pipeline_mode=pl.Buffered(1))


def _rope_tables(seq, block):
    inv_freq = ROPE_THETA ** (-np.arange(ROPE_HALF, dtype=np.float64) / ROPE_HALF)
    ang = np.arange(seq, dtype=np.float64)[:, None] * inv_freq[None, :]
    cos, sin = np.cos(ang), np.sin(ang)
    n_blocks = seq // block
    cos_t, sin_t = (t.T.reshape(ROPE_HALF, n_blocks, block).transpose(1, 0, 2) for t in (cos, sin))
    ones = np.ones((seq, HEAD_DIM - ROPE_DIM))
    zeros = np.zeros((seq, HEAD_DIM - ROPE_DIM))
    zh = np.zeros((seq, ROPE_HALF))
    cos_h = np.concatenate([cos, cos, ones], axis=1)
    sa_h = np.concatenate([-sin, zh, zeros], axis=1)
    sb_h = np.concatenate([zh, sin, zeros], axis=1)
    tile = lambda t: np.concatenate([t] * HEADS_PER_LANE_TILE, axis=1).reshape(n_blocks, block, LANES)
    tables = (cos_t, sin_t, tile(cos_h), tile(sa_h), tile(sb_h))
    return tuple(jnp.asarray(np.ascontiguousarray(t), dtype=F32) for t in tables)


def _layer(h, p, mix_pre_g, w_in, gmlp_ln_g, gmlp_ln_b, w_s, b_s, attn_out_g,
           gmlp_out_g, w_o, mix_post_g, mlp_pre_g, w_up, w_down, mlp_post_g,
           w_ple, w_ple_gate, b_ple_gate, ple_post_g):
    B, S, D = h.shape
    assert D == D_MODEL and S % MOBA_BLOCK == 0
    nb = S // MOBA_BLOCK
    T = MOBA_BLOCK
    n_tiles = B * nb
    row = lambda v: v.reshape(1, -1).astype(F32)

    qscale = np.ones((1, D_IN_PROJ), np.float32)
    qscale[:, :D_ATTN] = HEAD_DIM ** -0.5 * math.log2(math.e)
    cos_t, sin_t, cos_k, sa_k, sb_k = _rope_tables(S, T)
    bs_exp = jnp.repeat(b_s.T, GROUP_DIM, axis=1).astype(F32)

    def cur(s):
        t = jnp.minimum(s, n_tiles - 1)
        return t // nb, t % nb

    def prev(s):
        t = jnp.maximum(s - 1, 0)
        return t // nb, t % nb

    hbm = pl.BlockSpec(memory_space=pl.ANY)
    out = pl.pallas_call(
        _layer_kernel,
        grid=(n_tiles + 1,),
        in_specs=[
            pl.BlockSpec((1, T, D), lambda s: (*cur(s), 0)),
            pl.BlockSpec((1, T, D_PLE), lambda s: (*prev(s), 0)),
            _const_spec((1, D)),
            hbm,
            _const_spec((1, D_IN_PROJ)),
            _const_spec((nb, ROPE_HALF, T)),
            _const_spec((nb, ROPE_HALF, T)),
            _const_spec((nb, T, LANES)),
            _const_spec((nb, T, LANES)),
            _const_spec((nb, T, LANES)),
            _const_spec((1, D_GMLP)),
            _const_spec((1, D_GMLP)),
            _const_spec((N_GROUPS, GMLP_CHUNK, GMLP_CHUNK)),
            _const_spec((GMLP_CHUNK, D_GMLP)),
            _const_spec((D_ATTN, 1)),
            _const_spec((1, D_GMLP)),
            hbm,
            _const_spec((1, D)),
            _const_spec((1, D)),
            hbm,
            hbm,
            _const_spec((1, D)),
            hbm,
            _const_spec((1, D)),
            hbm,
            _const_spec((1, D)),
        ],
        out_specs=pl.BlockSpec((1, T, D), lambda s: (*prev(s), 0)),
        out_shape=jax.ShapeDtypeStruct((B, S, D), F32),
        scratch_shapes=([pltpu.VMEM((nb, T, D_ATTN), BF16),
                         pltpu.VMEM((nb, N_HEADS * V_ROWS, T), BF16),
                         pltpu.VMEM((nb, D_ATTN), F32),
                         pltpu.VMEM((T, D_GMLP), BF16),
                         pltpu.VMEM((T, D_GMLP), BF16),
                         pltpu.VMEM((2, T, D), F32),
                         pltpu.VMEM((D, D_IN_PROJ), BF16),
                         pltpu.VMEM((D_ATTN + D_GMLP, D), BF16),
                         pltpu.VMEM((D, D_FF), BF16),
                         pltpu.VMEM((D_FF, D), BF16),
                         pltpu.VMEM((D, D), BF16),
                         pltpu.VMEM((D_PLE, D), BF16),
                         pltpu.VMEM((D_GMLP // MXU_COLS, GMLP_CHUNK, GROUPS_PER_GATE_TILE * GMLP_CHUNK), BF16),
                         pltpu.VMEM((2, STAGE_ROWS, D_FF), F32),
                         pltpu.SemaphoreType.DMA((2,))]
                        + [pltpu.VMEM((nb, T), F32)] * N_HEADS
                        + [pltpu.VMEM((LANES, T), BF16)] * N_HEADS
                        + [pltpu.VMEM((V_ROWS, T), F32)] * N_HEADS
                        + [pltpu.VMEM((MOBA_BLOCK, T), F32)] * N_HEADS
                        + [pltpu.VMEM((1, T), F32)] * N_HEADS),
        compiler_params=pltpu.CompilerParams(dimension_semantics=("arbitrary",),
                                             vmem_limit_bytes=VMEM_LIMIT_BYTES),
        name="layer",
    )(h, p, row(mix_pre_g), w_in.astype(F32), jnp.asarray(qscale), cos_t, sin_t, cos_k, sa_k, sb_k,
      row(gmlp_ln_g), row(gmlp_ln_b), w_s.astype(F32), bs_exp,
      attn_out_g.reshape(-1, 1).astype(F32), row(gmlp_out_g), w_o.astype(F32), row(mix_post_g),
      row(mlp_pre_g), w_up.astype(F32), w_down.astype(F32), row(mlp_post_g),
      w_ple_gate.astype(F32), row(b_ple_gate), w_ple.astype(F32), row(ple_post_g))
    return out


def kernel(x, p, mix_pre_g, w_in, gmlp_ln_g, gmlp_ln_b, w_s, b_s, attn_out_g,
           gmlp_out_g, w_o, mix_post_g, mlp_pre_g, w_up, w_down, mlp_post_g,
           w_ple, w_ple_gate, b_ple_gate, ple_post_g):
    h = x
    for i in range(w_in.shape[0]):
        h = _layer(h, p[i], mix_pre_g[i], w_in[i], gmlp_ln_g[i], gmlp_ln_b[i],
                   w_s[i], b_s[i], attn_out_g[i], gmlp_out_g[i], w_o[i],
                   mix_post_g[i], mlp_pre_g[i], w_up[i], w_down[i], mlp_post_g[i],
                   w_ple[i], w_ple_gate[i], b_ple_gate[i], ple_post_g[i])
    return h
```

```python
import functools
import math

import jax
import jax.numpy as jnp
import numpy as np
from jax import lax
from jax.experimental import pallas as pl
from jax.experimental.pallas import tpu as pltpu

D_MODEL = 1024
D_PLE = 256
N_HEADS = 8
HEAD_DIM = 64
D_ATTN = N_HEADS * HEAD_DIM
N_GROUPS = 8
GROUP_DIM = 64
D_GMLP = N_GROUPS * GROUP_DIM
D_IN_PROJ = 3 * D_ATTN + 2 * D_GMLP
D_FF = 4 * D_MODEL
MOBA_BLOCK = 256
MOBA_TOPK = 3
GMLP_CHUNK = 128
ROPE_THETA = 500000.0
ROPE_DIM = HEAD_DIM // 4
ROPE_HALF = ROPE_DIM // 2
NORM_EPS = 1e-6
NEG_INF = -1e30

LANES = 128
HEADS_PER_LANE_TILE = LANES // HEAD_DIM
GROUPS_PER_LANE_TILE = LANES // GROUP_DIM
BF16_SUBLANES = 16
V_ROWS = HEAD_DIM + BF16_SUBLANES
BIG = 1e30
SCORE_LEAD = 2
VMEM_LIMIT_BYTES = 58 * 1024 * 1024
FF_CHUNK = 1024
STAGE_ROWS = 128

F32 = jnp.float32
BF16 = jnp.bfloat16


def _dot(a, b):
    return jnp.dot(a, b, preferred_element_type=F32)


def _dot_tn(a, b):
    return lax.dot_general(a, b, (((0,), (0,)), ((), ())), preferred_element_type=F32)


def _rms_rows(x, g):
    ms = jnp.mean(x * x, axis=-1, keepdims=True)
    return x * lax.rsqrt(ms + NORM_EPS) * g


def _gelu_exact(x):
    return 0.5 * x * (1.0 + lax.erf(x * math.sqrt(0.5)))


def _load_weight_bf16(src_hbm, dst_ref, stage_ref, sem_ref, col_scale=None):
    n_rows, width = src_hbm.shape
    chunk = min(stage_ref.shape[1], n_rows)
    n_chunks = n_rows // chunk
    assert n_chunks * chunk == n_rows and width <= stage_ref.shape[2]

    def copy(c, slot):
        return pltpu.make_async_copy(src_hbm.at[pl.ds(c * chunk, chunk), :],
                                     stage_ref.at[slot, pl.ds(0, chunk), pl.ds(0, width)],
                                     sem_ref.at[slot])

    copy(0, 0).start()

    def body(c, carry):
        slot = lax.rem(c, 2)

        @pl.when(c + 1 < n_chunks)
        def _():
            copy(c + 1, 1 - slot).start()

        copy(c, slot).wait()
        w = stage_ref[slot, pl.ds(0, chunk), pl.ds(0, width)]
        if col_scale is not None:
            w = w * col_scale
        rows = pl.ds(pl.multiple_of(c * chunk, chunk), chunk)
        dst_ref[rows, :] = w.astype(BF16)
        return carry

    lax.fori_loop(0, n_chunks, body, 0)


def _layer_kernel(x_ref, xn_ref, p_ref, g_ref, win_hbm, qscale_ref, cost_ref, sint_ref,
                  cosk_ref, sak_ref, sbk_ref, lng_ref, lnb_ref,
                  ws_ref, bs_ref, ag_ref, gg_ref, wo_hbm, pg_ref,
                  g1_ref, wup_hbm, wdn_hbm, g2_ref, wgate_hbm, bgate_ref, wple_hbm, g3_ref,
                  o_ref,
                  k_scr, vt_scr, km_scr, ug_scr, vn_scr, h1_scr,
                  win_ref, wo_ref, wup_ref, wdn_ref, wgate_ref, wple_ref, wsm_ref, stage, sem,
                  *scratch):
    sel_scr, qz_scr, m_scr, acc_scr, s_scr = (
        scratch[j * N_HEADS:(j + 1) * N_HEADS] for j in range(5))
    nb = k_scr.shape[0]
    tq = MOBA_BLOCK
    n_tiles = pl.num_programs(0) - 1
    step = pl.program_id(0)
    live = step < n_tiles
    i = lax.rem(jnp.minimum(step, n_tiles - 1), nb)
    i_next = lax.rem(jnp.minimum(step + 1, n_tiles - 1), nb)
    slot = lax.rem(step, 2)

    @pl.when(step == 0)
    def _():
        km_scr[...] = jnp.zeros(km_scr.shape, F32)
        h1_scr[...] = jnp.zeros(h1_scr.shape, F32)
        _load_weight_bf16(win_hbm, win_ref, stage, sem, col_scale=qscale_ref[...])
        _load_weight_bf16(wo_hbm, wo_ref, stage, sem)
        _load_weight_bf16(wup_hbm, wup_ref, stage, sem)
        _load_weight_bf16(wdn_hbm, wdn_ref, stage, sem)
        _load_weight_bf16(wgate_hbm, wgate_ref, stage, sem)
        _load_weight_bf16(wple_hbm, wple_ref, stage, sem)
        t_idx = lax.broadcasted_iota(jnp.int32, (GMLP_CHUNK, GMLP_CHUNK), 0)
        s_idx = lax.broadcasted_iota(jnp.int32, (GMLP_CHUNK, GMLP_CHUNK), 1)
        for g in range(N_GROUPS):
            tile, part = divmod(g, GROUPS_PER_LANE_TILE)
            wsm_ref[tile, :, part * GMLP_CHUNK:(part + 1) * GMLP_CHUNK] = (
                jnp.where(s_idx <= t_idx, ws_ref[g], 0.0).astype(BF16))

    def scores(blk, h):
        hp = h // HEADS_PER_LANE_TILE
        return _dot(k_scr[blk, :, hp * LANES:(hp + 1) * LANES], qz_scr[h][...])

    def v_rows(blk, h):
        return vt_scr[blk, h * V_ROWS:(h + 1) * V_ROWS, :]

    proj = {}

    def proj_begin(x, blk, dst):
        proj.update(blk=blk, dst=dst, hn=_rms_rows(x, g_ref[...]).astype(BF16))

    def proj_gv():
        g2 = _gelu_exact(_dot(proj['hn'], win_ref[:, 3 * D_ATTN + D_GMLP:]))
        mu = jnp.mean(g2, axis=-1, keepdims=True)
        xc = g2 - mu
        var = jnp.mean(xc * xc, axis=-1, keepdims=True)
        vn = xc * lax.rsqrt(var + NORM_EPS) * lng_ref[...] + lnb_ref[...]
        vn_scr[proj['dst']] = vn.astype(BF16)

    def proj_q():
        proj['qt'] = _dot(proj['hn'], win_ref[:, :D_ATTN]).T

    def proj_k():
        blk = proj['blk']
        tok = _dot(proj['hn'], win_ref[:, D_ATTN:2 * D_ATTN])
        cos_k = cosk_ref[blk]
        sa_k = sak_ref[blk]
        sb_k = sbk_ref[blk]
        kparts = []
        for j in range(D_ATTN // LANES):
            kb = tok[:, j * LANES:(j + 1) * LANES]
            up = pltpu.roll(kb, LANES - ROPE_HALF, axis=1)
            dn = pltpu.roll(kb, ROPE_HALF, axis=1)
            kparts.append(kb * cos_k + up * sa_k + dn * sb_k)
        k = jnp.concatenate(kparts, axis=1)
        k_scr[blk] = k.astype(BF16)
        km_scr[pl.ds(blk, 1), :] = jnp.mean(k, axis=0, keepdims=True)

    def proj_u():
        u = _dot(proj['hn'], win_ref[:, 3 * D_ATTN:3 * D_ATTN + D_GMLP])
        ug_scr[proj['dst']] = _gelu_exact(u).astype(BF16)

    def proj_v():
        vt = _dot(proj['hn'], win_ref[:, 2 * D_ATTN:3 * D_ATTN]).T
        ones = jnp.ones((V_ROWS - HEAD_DIM, tq), F32)
        vparts = []
        for h in range(N_HEADS):
            vparts.append(vt[h * HEAD_DIM:(h + 1) * HEAD_DIM])
            vparts.append(ones)
        vt_scr[proj['blk']] = jnp.concatenate(vparts, axis=0).astype(BF16)

    def proj_qz():
        qt = proj['qt']
        cos_t = cost_ref[proj['blk']]
        sin_t = sint_ref[proj['blk']]
        zeros_head = jnp.zeros((HEAD_DIM, tq), F32)
        for h in range(N_HEADS):
            hh = h % HEADS_PER_LANE_TILE
            r0 = h * HEAD_DIM
            x1 = qt[r0:r0 + ROPE_HALF]
            x2 = qt[r0 + ROPE_HALF:r0 + ROPE_DIM]
            q_h = [x1 * cos_t - x2 * sin_t, x2 * cos_t + x1 * sin_t, qt[r0 + ROPE_DIM:r0 + HEAD_DIM]]
            padded = [zeros_head] * hh + q_h + [zeros_head] * (HEADS_PER_LANE_TILE - 1 - hh)
            qz_scr[h][...] = jnp.concatenate(padded, axis=0).astype(BF16)

    @pl.when(step == 0)
    def _():
        proj_begin(x_ref[0], i, slot)
        for piece in (proj_gv, proj_q, proj_k, proj_u, proj_v, proj_qz):
            piece()

    @pl.when(live)
    def _():
        blk_row = lax.broadcasted_iota(jnp.int32, (nb, tq), 0)
        past = blk_row < i
        km = km_scr[...]
        km_hi = km.astype(BF16)
        r1 = km - km_hi.astype(F32)
        km_mid = r1.astype(BF16)
        km_lo = (r1 - km_mid.astype(F32)).astype(BF16)
        km3 = jnp.concatenate([km_hi, km_mid, km_lo], axis=0)
        for h in range(N_HEADS):
            hp = h // HEADS_PER_LANE_TILE
            s_scr[h][...] = scores(0, h)

            g3 = _dot(km3[:, hp * LANES:(hp + 1) * LANES], qz_scr[h][...])
            gate = g3[0:nb] + g3[nb:2 * nb] + g3[2 * nb:3 * nb]
            gate = jnp.where(past, gate, NEG_INF)
            rank = jnp.zeros((nb, tq), jnp.int32)
            for m in range(nb):
                gm = gate[m:m + 1, :]
                beats = (gm > gate) | ((gm == gate) & (m < blk_row))
                rank = rank + beats.astype(jnp.int32)
            sel = past & (rank < MOBA_TOPK)
            sel_scr[h][...] = sel.astype(F32)
            m_scr[h][...] = jnp.full((1, tq), NEG_INF, F32)
            acc_scr[h][...] = jnp.zeros((V_ROWS, tq), F32)

    def past_blocks(kb0, n_blocks):
        units = [(kb0 + j, h) for j in range(n_blocks) for h in range(N_HEADS)]
        ahead = {}
        for u in range(-SCORE_LEAD, len(units)):
            if 0 <= u + SCORE_LEAD < len(units):
                blk, head = units[u + SCORE_LEAD]
                ahead[u + SCORE_LEAD] = scores(blk + 1, head)
            if u < 0:
                continue
            kb, h = units[u]
            s = s_scr[h][...]
            keep = sel_scr[h][pl.ds(kb, 1), :] > 0.5
            m_run = m_scr[h][...]
            m_new = jnp.where(keep, jnp.maximum(m_run, jnp.max(s, axis=0, keepdims=True)), m_run)
            m_sub = jnp.where(keep, m_new, BIG)
            alpha = jnp.exp2(m_run - m_new)
            pb = jnp.exp2((s - m_sub).astype(BF16))
            m_scr[h][...] = m_new
            acc_scr[h][...] = alpha * acc_scr[h][...] + _dot(v_rows(kb, h), pb)
            s_scr[h][...] = ahead.pop(u)

    def two_blocks(t, carry):
        past_blocks(2 * t, 2)
        return carry

    n_past = jnp.where(live, i, 0)
    lax.fori_loop(0, n_past // 2, two_blocks, 0)

    @pl.when(lax.rem(n_past, 2) == 1)
    def _():
        past_blocks(n_past - 1, 1)

    mlp = {}

    def mlp_begin():
        mlp['h'] = h1_scr[1 - slot]
        mlp['hn'] = _rms_rows(mlp['h'], g1_ref[...]).astype(BF16)
        mlp['acc'] = None

    def mlp_up(c):
        up = jnp.maximum(_dot(mlp['hn'], wup_ref[:, c * FF_CHUNK:(c + 1) * FF_CHUNK]), 0.0)
        mlp['up'] = (up * up).astype(BF16)

    def mlp_down(c):
        d = _dot(mlp['up'], wdn_ref[c * FF_CHUNK:(c + 1) * FF_CHUNK, :])
        mlp['acc'] = d if mlp['acc'] is None else mlp['acc'] + d

    def mlp_gate():
        mlp['h2'] = mlp['h'] + _rms_rows(mlp['acc'], g2_ref[...])
        mlp['z'] = _dot(mlp['h2'].astype(BF16), wgate_ref[...]) + bgate_ref[...]

    def mlp_ple():
        mlp['ple'] = _dot(p_ref[0].astype(BF16), wple_ref[...])

    def mlp_end():
        gate = 1.0 / (1.0 + jnp.exp(-mlp['z']))
        o_ref[0] = mlp['h2'] + _rms_rows(mlp['ple'] * gate, g3_ref[...])

    n_ff = D_FF // FF_CHUNK
    heads_per_piece = N_HEADS // (2 * n_ff)
    assert heads_per_piece * 2 * n_ff == N_HEADS
    pieces_before_head = [[mlp_begin]] + [[] for _ in range(N_HEADS - 1)]
    for c in range(n_ff):
        pieces_before_head[2 * c * heads_per_piece].append(functools.partial(mlp_up, c))
        pieces_before_head[(2 * c + 1) * heads_per_piece].append(functools.partial(mlp_down, c))

    kpos = lax.broadcasted_iota(jnp.int32, (MOBA_BLOCK, tq), 0)
    qpos = lax.broadcasted_iota(jnp.int32, (MOBA_BLOCK, tq), 1)
    causal = kpos <= qpos
    lane = lax.broadcasted_iota(jnp.int32, (GMLP_CHUNK, LANES), 1)
    n_lane_tiles = D_GMLP // LANES
    n_chunks = tq // GMLP_CHUNK
    assert n_lane_tiles * n_chunks == N_HEADS
    heads_out = []
    cols = [[None] * n_lane_tiles for _ in range(n_chunks)]
    for h in range(N_HEADS):
        for piece in pieces_before_head[h]:
            piece()
        c, gp = divmod(h, n_lane_tiles)
        vp = vn_scr[slot, c * GMLP_CHUNK:(c + 1) * GMLP_CHUNK, gp * LANES:(gp + 1) * LANES]
        stacked = jnp.concatenate(
            [jnp.where(lane // GROUP_DIM == part, vp, jnp.zeros_like(vp))
             for part in range(GROUPS_PER_LANE_TILE)], axis=0)
        cols[c][gp] = _dot(wsm_ref[gp], stacked)

        st = jnp.where(causal, s_scr[h][...], NEG_INF)
        m_run = m_scr[h][...]
        m_new = jnp.maximum(m_run, jnp.max(st, axis=0, keepdims=True))
        alpha = jnp.exp2(m_run - m_new)
        p = jnp.exp2((st - m_new).astype(BF16))
        acc = alpha * acc_scr[h][...] + _dot(v_rows(i, h), p)
        heads_out.append(acc[0:HEAD_DIM] / acc[HEAD_DIM:HEAD_DIM + 1])

    mlp_ple()
    mixed = jnp.concatenate(
        [jnp.concatenate(cols[c], axis=1) + bs_ref[...] for c in range(n_chunks)], axis=0)
    gm_out = ug_scr[slot].astype(F32) * mixed
    gm_n = _rms_rows(gm_out, gg_ref[...]).astype(BF16)
    y = _dot(gm_n, wo_ref[D_ATTN:, :])
    mlp_gate()

    attn_t = jnp.concatenate(heads_out, axis=0)
    ms_a = jnp.mean(attn_t * attn_t, axis=0, keepdims=True)
    attn_n = (attn_t * lax.rsqrt(ms_a + NORM_EPS) * ag_ref[...]).astype(BF16)
    y = y + _dot_tn(attn_n, wo_ref[:D_ATTN, :])
    proj_begin(xn_ref[0], i_next, 1 - slot)
    proj_gv()
    h1_scr[slot] = x_ref[0] + _rms_rows(y, pg_ref[...])
    proj_q()
    mlp_end()
    for piece in (proj_k, proj_u, proj_v, proj_qz):
        piece()


def _const_spec(shape):
    zeros = (0,) * len(shape)
    return pl.BlockSpec(shape, lambda *_: zeros, pipeline_mode=pl.Buffered(1))


def _rope_tables(seq, block):
    inv_freq = ROPE_THETA ** (-np.arange(ROPE_HALF, dtype=np.float64) / ROPE_HALF)
    ang = np.arange(seq, dtype=np.float64)[:, None] * inv_freq[None, :]
    cos, sin = np.cos(ang), np.sin(ang)
    n_blocks = seq // block
    cos_t, sin_t = (t.T.reshape(ROPE_HALF, n_blocks, block).transpose(1, 0, 2) for t in (cos, sin))
    ones = np.ones((seq, HEAD_DIM - ROPE_DIM))
    zeros = np.zeros((seq, HEAD_DIM - ROPE_DIM))
    zh = np.zeros((seq, ROPE_HALF))
    cos_h = np.concatenate([cos, cos, ones], axis=1)
    sa_h = np.concatenate([-sin, zh, zeros], axis=1)
    sb_h = np.concatenate([zh, sin, zeros], axis=1)
    tile = lambda t: np.concatenate([t] * HEADS_PER_LANE_TILE, axis=1).reshape(n_blocks, block, LANES)
    tables = (cos_t, sin_t, tile(cos_h), tile(sa_h), tile(sb_h))
    return tuple(jnp.asarray(np.ascontiguousarray(t), dtype=F32) for t in tables)


def _layer(h, p, mix_pre_g, w_in, gmlp_ln_g, gmlp_ln_b, w_s, b_s, attn_out_g,
           gmlp_out_g, w_o, mix_post_g, mlp_pre_g, w_up, w_down, mlp_post_g,
           w_ple, w_ple_gate, b_ple_gate, ple_post_g):
    B, S, D = h.shape
    assert D == D_MODEL and S % MOBA_BLOCK == 0
    nb = S // MOBA_BLOCK
    T = MOBA_BLOCK
    n_tiles = B * nb
    row = lambda v: v.reshape(1, -1).astype(F32)

    qscale = np.ones((1, D_IN_PROJ), np.float32)
    qscale[:, :D_ATTN] = HEAD_DIM ** -0.5 * math.log2(math.e)
    cos_t, sin_t, cos_k, sa_k, sb_k = _rope_tables(S, T)
    bs_exp = jnp.repeat(b_s.T, GROUP_DIM, axis=1).astype(F32)

    def cur(s):
        t = jnp.minimum(s, n_tiles - 1)
        return t // nb, t % nb

    def nxt(s):
        t = jnp.minimum(s + 1, n_tiles - 1)
        return t // nb, t % nb

    def prev(s):
        t = jnp.maximum(s - 1, 0)
        return t // nb, t % nb

    hbm = pl.BlockSpec(memory_space=pl.ANY)
    out = pl.pallas_call(
        _layer_kernel,
        grid=(n_tiles + 1,),
        in_specs=[
            pl.BlockSpec((1, T, D), lambda s: (*cur(s), 0)),
            pl.BlockSpec((1, T, D), lambda s: (*nxt(s), 0)),
            pl.BlockSpec((1, T, D_PLE), lambda s: (*prev(s), 0)),
            _const_spec((1, D)),
            hbm,
            _const_spec((1, D_IN_PROJ)),
            _const_spec((nb, ROPE_HALF, T)),
            _const_spec((nb, ROPE_HALF, T)),
            _const_spec((nb, T, LANES)),
            _const_spec((nb, T, LANES)),
            _const_spec((nb, T, LANES)),
            _const_spec((1, D_GMLP)),
            _const_spec((1, D_GMLP)),
            _const_spec((N_GROUPS, GMLP_CHUNK, GMLP_CHUNK)),
            _const_spec((GMLP_CHUNK, D_GMLP)),
            _const_spec((D_ATTN, 1)),
            _const_spec((1, D_GMLP)),
            hbm,
            _const_spec((1, D)),
            _const_spec((1, D)),
            hbm,
            hbm,
            _const_spec((1, D)),
            hbm,
            _const_spec((1, D)),
            hbm,
            _const_spec((1, D)),
        ],
        out_specs=pl.BlockSpec((1, T, D), lambda s: (*prev(s), 0)),
        out_shape=jax.ShapeDtypeStruct((B, S, D), F32),
        scratch_shapes=([pltpu.VMEM((nb, T, D_ATTN), BF16),
                         pltpu.VMEM((nb, N_HEADS * V_ROWS, T), BF16),
                         pltpu.VMEM((nb, D_ATTN), F32),
                         pltpu.VMEM((2, T, D_GMLP), BF16),
                         pltpu.VMEM((2, T, D_GMLP), BF16),
                         pltpu.VMEM((2, T, D), F32),
                         pltpu.VMEM((D, D_IN_PROJ), BF16),
                         pltpu.VMEM((D_ATTN + D_GMLP, D), BF16),
                         pltpu.VMEM((D, D_FF), BF16),
                         pltpu.VMEM((D_FF, D), BF16),
                         pltpu.VMEM((D, D), BF16),
                         pltpu.VMEM((D_PLE, D), BF16),
                         pltpu.VMEM((D_GMLP // LANES, GMLP_CHUNK, GROUPS_PER_LANE_TILE * GMLP_CHUNK), BF16),
                         pltpu.VMEM((2, STAGE_ROWS, D_FF), F32),
                         pltpu.SemaphoreType.DMA((2,))]
                        + [pltpu.VMEM((nb, T), F32)] * N_HEADS
                        + [pltpu.VMEM((LANES, T), BF16)] * N_HEADS
                        + [pltpu.VMEM((1, T), F32)] * N_HEADS
                        + [pltpu.VMEM((V_ROWS, T), F32)] * N_HEADS
                        + [pltpu.VMEM((MOBA_BLOCK, T), F32)] * N_HEADS),
        compiler_params=pltpu.CompilerParams(dimension_semantics=("arbitrary",),
                                             vmem_limit_bytes=VMEM_LIMIT_BYTES),
        name="layer",
    )(h, h, p, row(mix_pre_g), w_in.astype(F32), jnp.asarray(qscale), cos_t, sin_t, cos_k, sa_k, sb_k,
      row(gmlp_ln_g), row(gmlp_ln_b), w_s.astype(F32), bs_exp,
      attn_out_g.reshape(-1, 1).astype(F32), row(gmlp_out_g), w_o.astype(F32), row(mix_post_g),
      row(mlp_pre_g), w_up.astype(F32), w_down.astype(F32), row(mlp_post_g),
      w_ple_gate.astype(F32), row(b_ple_gate), w_ple.astype(F32), row(ple_post_g))
    return out


def kernel(x, p, mix_pre_g, w_in, gmlp_ln_g, gmlp_ln_b, w_s, b_s, attn_out_g,
           gmlp_out_g, w_o, mix_post_g, mlp_pre_g, w_up, w_down, mlp_post_g,
           w_ple, w_ple_gate, b_ple_gate, ple_post_g):
    h = x
    for i in range(w_in.shape[0]):
        h = _layer(h, p[i], mix_pre_g[i], w_in[i], gmlp_ln_g[i], gmlp_ln_b[i],
                   w_s[i], b_s[i], attn_out_g[i], gmlp_out_g[i], w_o[i],
                   mix_post_g[i], mlp_pre_g[i], w_up[i], w_down[i], mlp_post_g[i],
                   w_ple[i], w_ple_gate[i], b_ple_gate[i], ple_post_g[i])
    return h
```

```python
import functools
import math

import jax
import jax.numpy as jnp
import numpy as np
from jax import lax
from jax.experimental import pallas as pl
from jax.experimental.pallas import tpu as pltpu

D_MODEL = 1024
D_PLE = 256
N_HEADS = 8
HEAD_DIM = 64
D_ATTN = N_HEADS * HEAD_DIM
N_GROUPS = 8
GROUP_DIM = 64
D_GMLP = N_GROUPS * GROUP_DIM
D_IN_PROJ = 3 * D_ATTN + 2 * D_GMLP
D_FF = 4 * D_MODEL
MOBA_BLOCK = 256
MOBA_TOPK = 3
GMLP_CHUNK = 128
ROPE_THETA = 500000.0
ROPE_DIM = HEAD_DIM // 4
ROPE_HALF = ROPE_DIM // 2
NORM_EPS = 1e-6
NEG_INF = -1e30

LANES = 128
HEADS_PER_LANE_TILE = LANES // HEAD_DIM
GROUPS_PER_LANE_TILE = LANES // GROUP_DIM
BF16_SUBLANES = 16
V_ROWS = HEAD_DIM + BF16_SUBLANES
BIG = 1e30
SCORE_LEAD = 2
VMEM_LIMIT_BYTES = 58 * 1024 * 1024
FF_CHUNK = 1024
STAGE_ROWS = 128

F32 = jnp.float32
BF16 = jnp.bfloat16


def _dot(a, b):
    return jnp.dot(a, b, preferred_element_type=F32)


def _dot_tn(a, b):
    return lax.dot_general(a, b, (((0,), (0,)), ((), ())), preferred_element_type=F32)


def _rms_rows(x, g):
    ms = jnp.mean(x * x, axis=-1, keepdims=True)
    return x * lax.rsqrt(ms + NORM_EPS) * g


def _gelu_exact(x):
    return 0.5 * x * (1.0 + lax.erf(x * math.sqrt(0.5)))


def _load_weight_bf16(src_hbm, dst_ref, stage_ref, sem_ref, col_scale=None):
    n_rows, width = src_hbm.shape
    chunk = min(stage_ref.shape[1], n_rows)
    n_chunks = n_rows // chunk
    assert n_chunks * chunk == n_rows and width <= stage_ref.shape[2]

    def copy(c, slot):
        return pltpu.make_async_copy(src_hbm.at[pl.ds(c * chunk, chunk), :],
                                     stage_ref.at[slot, pl.ds(0, chunk), pl.ds(0, width)],
                                     sem_ref.at[slot])

    copy(0, 0).start()

    def body(c, carry):
        slot = lax.rem(c, 2)

        @pl.when(c + 1 < n_chunks)
        def _():
            copy(c + 1, 1 - slot).start()

        copy(c, slot).wait()
        w = stage_ref[slot, pl.ds(0, chunk), pl.ds(0, width)]
        if col_scale is not None:
            w = w * col_scale
        rows = pl.ds(pl.multiple_of(c * chunk, chunk), chunk)
        dst_ref[rows, :] = w.astype(BF16)
        return carry

    lax.fori_loop(0, n_chunks, body, 0)


def _layer_kernel(x_ref, p_ref, g_ref, win_hbm, qscale_ref, cost_ref, sint_ref,
                  cosk_ref, sak_ref, sbk_ref, lng_ref, lnb_ref,
                  ws_ref, bs_ref, ag_ref, gg_ref, wo_hbm, pg_ref,
                  g1_ref, wup_hbm, wdn_hbm, g2_ref, wgate_hbm, bgate_ref, wple_hbm, g3_ref,
                  o_ref,
                  k_scr, vt_scr, km_scr, ug_scr, vn_scr, h1_scr,
                  win_ref, wo_ref, wup_ref, wdn_ref, wgate_ref, wple_ref, wsm_ref, stage, sem,
                  *scratch):
    sel_scr, qz_scr, m_scr, acc_scr, s_scr = (
        scratch[j * N_HEADS:(j + 1) * N_HEADS] for j in range(5))
    nb = k_scr.shape[0]
    tq = MOBA_BLOCK
    n_tiles = pl.num_programs(0) - 1
    step = pl.program_id(0)
    live = step < n_tiles
    i = lax.rem(jnp.minimum(step, n_tiles - 1), nb)
    slot = lax.rem(step, 2)

    @pl.when(step == 0)
    def _():
        km_scr[...] = jnp.zeros(km_scr.shape, F32)
        _load_weight_bf16(win_hbm, win_ref, stage, sem, col_scale=qscale_ref[...])
        _load_weight_bf16(wo_hbm, wo_ref, stage, sem)
        _load_weight_bf16(wup_hbm, wup_ref, stage, sem)
        _load_weight_bf16(wdn_hbm, wdn_ref, stage, sem)
        _load_weight_bf16(wgate_hbm, wgate_ref, stage, sem)
        _load_weight_bf16(wple_hbm, wple_ref, stage, sem)
        t_idx = lax.broadcasted_iota(jnp.int32, (GMLP_CHUNK, GMLP_CHUNK), 0)
        s_idx = lax.broadcasted_iota(jnp.int32, (GMLP_CHUNK, GMLP_CHUNK), 1)
        for g in range(N_GROUPS):
            tile, part = divmod(g, GROUPS_PER_LANE_TILE)
            wsm_ref[tile, :, part * GMLP_CHUNK:(part + 1) * GMLP_CHUNK] = (
                jnp.where(s_idx <= t_idx, ws_ref[g], 0.0).astype(BF16))

    def scores(blk, h):
        hp = h // HEADS_PER_LANE_TILE
        return _dot(k_scr[blk, :, hp * LANES:(hp + 1) * LANES], qz_scr[h][...])

    def v_rows(blk, h):
        return vt_scr[blk, h * V_ROWS:(h + 1) * V_ROWS, :]

    @pl.when(live)
    def _():
        x = x_ref[0]
        hn = _rms_rows(x, g_ref[...]).astype(BF16)

        gv = _dot(hn, win_ref[:, 3 * D_ATTN + D_GMLP:])
        g2 = _gelu_exact(gv)
        mu = jnp.mean(g2, axis=-1, keepdims=True)
        xc = g2 - mu
        var = jnp.mean(xc * xc, axis=-1, keepdims=True)
        vn = xc * lax.rsqrt(var + NORM_EPS) * lng_ref[...] + lnb_ref[...]
        vn_scr[...] = vn.astype(BF16)

        qt = _dot(hn, win_ref[:, :D_ATTN]).T

        tok = _dot(hn, win_ref[:, D_ATTN:2 * D_ATTN])
        cos_k = cosk_ref[i]
        sa_k = sak_ref[i]
        sb_k = sbk_ref[i]
        kparts = []
        for j in range(D_ATTN // LANES):
            kb = tok[:, j * LANES:(j + 1) * LANES]
            up = pltpu.roll(kb, LANES - ROPE_HALF, axis=1)
            dn = pltpu.roll(kb, ROPE_HALF, axis=1)
            kparts.append(kb * cos_k + up * sa_k + dn * sb_k)
        k = jnp.concatenate(kparts, axis=1)
        k_scr[i] = k.astype(BF16)
        km_scr[pl.ds(i, 1), :] = jnp.mean(k, axis=0, keepdims=True)

        u = _dot(hn, win_ref[:, 3 * D_ATTN:3 * D_ATTN + D_GMLP])
        ug_scr[...] = _gelu_exact(u).astype(BF16)

        vt = _dot(hn, win_ref[:, 2 * D_ATTN:3 * D_ATTN]).T
        ones = jnp.ones((V_ROWS - HEAD_DIM, tq), F32)
        vparts = []
        for h in range(N_HEADS):
            vparts.append(vt[h * HEAD_DIM:(h + 1) * HEAD_DIM])
            vparts.append(ones)
        vt_scr[i] = jnp.concatenate(vparts, axis=0).astype(BF16)

        cos_t = cost_ref[i]
        sin_t = sint_ref[i]
        zeros_head = jnp.zeros((HEAD_DIM, tq), F32)
        for h in range(N_HEADS):
            hh = h % HEADS_PER_LANE_TILE
            r0 = h * HEAD_DIM
            x1 = qt[r0:r0 + ROPE_HALF]
            x2 = qt[r0 + ROPE_HALF:r0 + ROPE_DIM]
            q_h = [x1 * cos_t - x2 * sin_t, x2 * cos_t + x1 * sin_t, qt[r0 + ROPE_DIM:r0 + HEAD_DIM]]
            padded = [zeros_head] * hh + q_h + [zeros_head] * (HEADS_PER_LANE_TILE - 1 - hh)
            qz_scr[h][...] = jnp.concatenate(padded, axis=0).astype(BF16)

    @pl.when(live)
    def _():
        blk_row = lax.broadcasted_iota(jnp.int32, (nb, tq), 0)
        past = blk_row < i
        km = km_scr[...]
        km_hi = km.astype(BF16)
        r1 = km - km_hi.astype(F32)
        km_mid = r1.astype(BF16)
        km_lo = (r1 - km_mid.astype(F32)).astype(BF16)
        km3 = jnp.concatenate([km_hi, km_mid, km_lo], axis=0)
        for h in range(N_HEADS):
            hp = h // HEADS_PER_LANE_TILE
            s_scr[h][...] = scores(0, h)

            g3 = _dot(km3[:, hp * LANES:(hp + 1) * LANES], qz_scr[h][...])
            gate = g3[0:nb] + g3[nb:2 * nb] + g3[2 * nb:3 * nb]
            gate = jnp.where(past, gate, NEG_INF)
            rank = jnp.zeros((nb, tq), jnp.int32)
            for m in range(nb):
                gm = gate[m:m + 1, :]
                beats = (gm > gate) | ((gm == gate) & (m < blk_row))
                rank = rank + beats.astype(jnp.int32)
            sel = past & (rank < MOBA_TOPK)
            sel_scr[h][...] = sel.astype(F32)
            m_scr[h][...] = jnp.full((1, tq), NEG_INF, F32)
            acc_scr[h][...] = jnp.zeros((V_ROWS, tq), F32)

    def past_blocks(kb0, n_blocks):
        units = [(kb0 + j, h) for j in range(n_blocks) for h in range(N_HEADS)]
        ahead = {}
        for u in range(-SCORE_LEAD, len(units)):
            if 0 <= u + SCORE_LEAD < len(units):
                blk, head = units[u + SCORE_LEAD]
                ahead[u + SCORE_LEAD] = scores(blk + 1, head)
            if u < 0:
                continue
            kb, h = units[u]
            s = s_scr[h][...]
            keep = sel_scr[h][pl.ds(kb, 1), :] > 0.5
            m_run = m_scr[h][...]
            m_new = jnp.where(keep, jnp.maximum(m_run, jnp.max(s, axis=0, keepdims=True)), m_run)
            m_sub = jnp.where(keep, m_new, BIG)
            alpha = jnp.exp2(m_run - m_new)
            pb = jnp.exp2((s - m_sub).astype(BF16))
            m_scr[h][...] = m_new
            acc_scr[h][...] = alpha * acc_scr[h][...] + _dot(v_rows(kb, h), pb)
            s_scr[h][...] = ahead.pop(u)

    def four_blocks(t, carry):
        past_blocks(4 * t, 4)
        return carry

    n_past = jnp.where(live, i, 0)
    lax.fori_loop(0, n_past // 4, four_blocks, 0)
    n_done = 4 * (n_past // 4)

    @pl.when(n_past - n_done >= 2)
    def _():
        past_blocks(n_done, 2)

    @pl.when(lax.rem(n_past, 2) == 1)
    def _():
        past_blocks(n_past - 1, 1)

    def mix_tail(with_mlp):
        mlp = {}

        def mlp_begin():
            mlp['h'] = h1_scr[1 - slot]
            mlp['hn'] = _rms_rows(mlp['h'], g1_ref[...]).astype(BF16)
            mlp['acc'] = None

        def mlp_up(c):
            up = jnp.maximum(_dot(mlp['hn'], wup_ref[:, c * FF_CHUNK:(c + 1) * FF_CHUNK]), 0.0)
            mlp['up'] = (up * up).astype(BF16)

        def mlp_down(c):
            d = _dot(mlp['up'], wdn_ref[c * FF_CHUNK:(c + 1) * FF_CHUNK, :])
            mlp['acc'] = d if mlp['acc'] is None else mlp['acc'] + d

        def mlp_gate():
            mlp['h2'] = mlp['h'] + _rms_rows(mlp['acc'], g2_ref[...])
            mlp['z'] = _dot(mlp['h2'].astype(BF16), wgate_ref[...]) + bgate_ref[...]

        def mlp_ple():
            mlp['ple'] = _dot(p_ref[0].astype(BF16), wple_ref[...])

        def mlp_end():
            gate = 1.0 / (1.0 + jnp.exp(-mlp['z']))
            o_ref[0] = mlp['h2'] + _rms_rows(mlp['ple'] * gate, g3_ref[...])

        n_ff = D_FF // FF_CHUNK
        heads_per_piece = N_HEADS // (2 * n_ff)
        assert heads_per_piece * 2 * n_ff == N_HEADS
        pieces_before_head = [[mlp_begin]] + [[] for _ in range(N_HEADS - 1)]
        for c in range(n_ff):
            pieces_before_head[2 * c * heads_per_piece].append(functools.partial(mlp_up, c))
            pieces_before_head[(2 * c + 1) * heads_per_piece].append(functools.partial(mlp_down, c))

        kpos = lax.broadcasted_iota(jnp.int32, (MOBA_BLOCK, tq), 0)
        qpos = lax.broadcasted_iota(jnp.int32, (MOBA_BLOCK, tq), 1)
        causal = kpos <= qpos
        lane = lax.broadcasted_iota(jnp.int32, (GMLP_CHUNK, LANES), 1)
        n_lane_tiles = D_GMLP // LANES
        n_chunks = tq // GMLP_CHUNK
        assert n_lane_tiles * n_chunks == N_HEADS
        heads_out = []
        cols = [[None] * n_lane_tiles for _ in range(n_chunks)]
        for h in range(N_HEADS):
            for piece in pieces_before_head[h] if with_mlp else ():
                piece()
            c, gp = divmod(h, n_lane_tiles)
            vp = vn_scr[c * GMLP_CHUNK:(c + 1) * GMLP_CHUNK, gp * LANES:(gp + 1) * LANES]
            stacked = jnp.concatenate(
                [jnp.where(lane // GROUP_DIM == part, vp, jnp.zeros_like(vp))
                 for part in range(GROUPS_PER_LANE_TILE)], axis=0)
            cols[c][gp] = _dot(wsm_ref[gp], stacked)

            st = jnp.where(causal, s_scr[h][...], NEG_INF)
            m_run = m_scr[h][...]
            m_new = jnp.maximum(m_run, jnp.max(st, axis=0, keepdims=True))
            alpha = jnp.exp2(m_run - m_new)
            p = jnp.exp2((st - m_new).astype(BF16))
            acc = alpha * acc_scr[h][...] + _dot(v_rows(i, h), p)
            heads_out.append(acc[0:HEAD_DIM] / acc[HEAD_DIM:HEAD_DIM + 1])

        if with_mlp:
            mlp_ple()
        mixed = jnp.concatenate(
            [jnp.concatenate(cols[c], axis=1) + bs_ref[...] for c in range(n_chunks)], axis=0)
        gm_out = ug_scr[...].astype(F32) * mixed
        gm_n = _rms_rows(gm_out, gg_ref[...]).astype(BF16)
        y = _dot(gm_n, wo_ref[D_ATTN:, :])
        if with_mlp:
            mlp_gate()

        attn_t = jnp.concatenate(heads_out, axis=0)
        ms_a = jnp.mean(attn_t * attn_t, axis=0, keepdims=True)
        attn_n = (attn_t * lax.rsqrt(ms_a + NORM_EPS) * ag_ref[...]).astype(BF16)
        y = y + _dot_tn(attn_n, wo_ref[:D_ATTN, :])
        h1_scr[slot] = x_ref[0] + _rms_rows(y, pg_ref[...])
        if with_mlp:
            mlp_end()

    @pl.when(step > 0)
    def _():
        mix_tail(True)

    @pl.when(step == 0)
    def _():
        mix_tail(False)


def _const_spec(shape):
    zeros = (0,) * len(shape)
    return pl.BlockSpec(shape, lambda *_: zeros, pipeline_mode=pl.Buffered(1))


def _rope_tables(seq, block):
    inv_freq = ROPE_THETA ** (-np.arange(ROPE_HALF, dtype=np.float64) / ROPE_HALF)
    ang = np.arange(seq, dtype=np.float64)[:, None] * inv_freq[None, :]
    cos, sin = np.cos(ang), np.sin(ang)
    n_blocks = seq // block
    cos_t, sin_t = (t.T.reshape(ROPE_HALF, n_blocks, block).transpose(1, 0, 2) for t in (cos, sin))
    ones = np.ones((seq, HEAD_DIM - ROPE_DIM))
    zeros = np.zeros((seq, HEAD_DIM - ROPE_DIM))
    zh = np.zeros((seq, ROPE_HALF))
    cos_h = np.concatenate([cos, cos, ones], axis=1)
    sa_h = np.concatenate([-sin, zh, zeros], axis=1)
    sb_h = np.concatenate([zh, sin, zeros], axis=1)
    tile = lambda t: np.concatenate([t] * HEADS_PER_LANE_TILE, axis=1).reshape(n_blocks, block, LANES)
    tables = (cos_t, sin_t, tile(cos_h), tile(sa_h), tile(sb_h))
    return tuple(jnp.asarray(np.ascontiguousarray(t), dtype=F32) for t in tables)


def _layer(h, p, mix_pre_g, w_in, gmlp_ln_g, gmlp_ln_b, w_s, b_s, attn_out_g,
           gmlp_out_g, w_o, mix_post_g, mlp_pre_g, w_up, w_down, mlp_post_g,
           w_ple, w_ple_gate, b_ple_gate, ple_post_g):
    B, S, D = h.shape
    assert D == D_MODEL and S % MOBA_BLOCK == 0
    nb = S // MOBA_BLOCK
    T = MOBA_BLOCK
    n_tiles = B * nb
    row = lambda v: v.reshape(1, -1).astype(F32)

    qscale = np.ones((1, D_IN_PROJ), np.float32)
    qscale[:, :D_ATTN] = HEAD_DIM ** -0.5 * math.log2(math.e)
    cos_t, sin_t, cos_k, sa_k, sb_k = _rope_tables(S, T)
    bs_exp = jnp.repeat(b_s.T, GROUP_DIM, axis=1).astype(F32)

    def cur(s):
        t = jnp.minimum(s, n_tiles - 1)
        return t // nb, t % nb

    def prev(s):
        t = jnp.maximum(s - 1, 0)
        return t // nb, t % nb

    hbm = pl.BlockSpec(memory_space=pl.ANY)
    out = pl.pallas_call(
        _layer_kernel,
        grid=(n_tiles + 1,),
        in_specs=[
            pl.BlockSpec((1, T, D), lambda s: (*cur(s), 0)),
            pl.BlockSpec((1, T, D_PLE), lambda s: (*prev(s), 0)),
            _const_spec((1, D)),
            hbm,
            _const_spec((1, D_IN_PROJ)),
            _const_spec((nb, ROPE_HALF, T)),
            _const_spec((nb, ROPE_HALF, T)),
            _const_spec((nb, T, LANES)),
            _const_spec((nb, T, LANES)),
            _const_spec((nb, T, LANES)),
            _const_spec((1, D_GMLP)),
            _const_spec((1, D_GMLP)),
            _const_spec((N_GROUPS, GMLP_CHUNK, GMLP_CHUNK)),
            _const_spec((GMLP_CHUNK, D_GMLP)),
            _const_spec((D_ATTN, 1)),
            _const_spec((1, D_GMLP)),
            hbm,
            _const_spec((1, D)),
            _const_spec((1, D)),
            hbm,
            hbm,
            _const_spec((1, D)),
            hbm,
            _const_spec((1, D)),
            hbm,
            _const_spec((1, D)),
        ],
        out_specs=pl.BlockSpec((1, T, D), lambda s: (*prev(s), 0)),
        out_shape=jax.ShapeDtypeStruct((B, S, D), F32),
        scratch_shapes=([pltpu.VMEM((nb, T, D_ATTN), BF16),
                         pltpu.VMEM((nb, N_HEADS * V_ROWS, T), BF16),
                         pltpu.VMEM((nb, D_ATTN), F32),
                         pltpu.VMEM((T, D_GMLP), BF16),
                         pltpu.VMEM((T, D_GMLP), BF16),
                         pltpu.VMEM((2, T, D), F32),
                         pltpu.VMEM((D, D_IN_PROJ), BF16),
                         pltpu.VMEM((D_ATTN + D_GMLP, D), BF16),
                         pltpu.VMEM((D, D_FF), BF16),
                         pltpu.VMEM((D_FF, D), BF16),
                         pltpu.VMEM((D, D), BF16),
                         pltpu.VMEM((D_PLE, D), BF16),
                         pltpu.VMEM((D_GMLP // LANES, GMLP_CHUNK, GROUPS_PER_LANE_TILE * GMLP_CHUNK), BF16),
                         pltpu.VMEM((2, STAGE_ROWS, D_FF), F32),
                         pltpu.SemaphoreType.DMA((2,))]
                        + [pltpu.VMEM((nb, T), F32)] * N_HEADS
                        + [pltpu.VMEM((LANES, T), BF16)] * N_HEADS
                        + [pltpu.VMEM((1, T), F32)] * N_HEADS
                        + [pltpu.VMEM((V_ROWS, T), F32)] * N_HEADS
                        + [pltpu.VMEM((MOBA_BLOCK, T), F32)] * N_HEADS),
        compiler_params=pltpu.CompilerParams(dimension_semantics=("arbitrary",),
                                             vmem_limit_bytes=VMEM_LIMIT_BYTES),
        name="layer",
    )(h, p, row(mix_pre_g), w_in.astype(F32), jnp.asarray(qscale), cos_t, sin_t, cos_k, sa_k, sb_k,
      row(gmlp_ln_g), row(gmlp_ln_b), w_s.astype(F32), bs_exp,
      attn_out_g.reshape(-1, 1).astype(F32), row(gmlp_out_g), w_o.astype(F32), row(mix_post_g),
      row(mlp_pre_g), w_up.astype(F32), w_down.astype(F32), row(mlp_post_g),
      w_ple_gate.astype(F32), row(b_ple_gate), w_ple.astype(F32), row(ple_post_g))
    return out


def kernel(x, p, mix_pre_g, w_in, gmlp_ln_g, gmlp_ln_b, w_s, b_s, attn_out_g,
           gmlp_out_g, w_o, mix_post_g, mlp_pre_g, w_up, w_down, mlp_post_g,
           w_ple, w_ple_gate, b_ple_gate, ple_post_g):
    h = x
    for i in range(w_in.shape[0]):
        h = _layer(h, p[i], mix_pre_g[i], w_in[i], gmlp_ln_g[i], gmlp_ln_b[i],
                   w_s[i], b_s[i], attn_out_g[i], gmlp_out_g[i], w_o[i],
                   mix_post_g[i], mlp_pre_g[i], w_up[i], w_down[i], mlp_post_g[i],
                   w_ple[i], w_ple_gate[i], b_ple_gate[i], ple_post_g[i])
    return h
```

```python
import functools
import math

import jax
import jax.numpy as jnp
import numpy as np
from jax import lax
from jax.experimental import pallas as pl
from jax.experimental.pallas import tpu as pltpu

D_MODEL = 1024
D_PLE = 256
N_HEADS = 8
HEAD_DIM = 64
D_ATTN = N_HEADS * HEAD_DIM
N_GROUPS = 8
GROUP_DIM = 64
D_GMLP = N_GROUPS * GROUP_DIM
D_IN_PROJ = 3 * D_ATTN + 2 * D_GMLP
D_FF = 4 * D_MODEL
MOBA_BLOCK = 256
MOBA_TOPK = 3
GMLP_CHUNK = 128
ROPE_THETA = 500000.0
ROPE_DIM = HEAD_DIM // 4
ROPE_HALF = ROPE_DIM // 2
NORM_EPS = 1e-6
NEG_INF = -1e30

LANES = 128
HEADS_PER_LANE_TILE = LANES // HEAD_DIM
GROUPS_PER_LANE_TILE = LANES // GROUP_DIM
BF16_SUBLANES = 16
V_ROWS = HEAD_DIM + BF16_SUBLANES
BIG = 1e30
SCORE_LEAD = 2
VMEM_LIMIT_BYTES = 58 * 1024 * 1024
FF_CHUNK = 1024
STAGE_ROWS = 128

F32 = jnp.float32
BF16 = jnp.bfloat16


def _dot(a, b):
    return jnp.dot(a, b, preferred_element_type=F32)


def _dot_tn(a, b):
    return lax.dot_general(a, b, (((0,), (0,)), ((), ())), preferred_element_type=F32)


def _rms_rows(x, g):
    ms = jnp.mean(x * x, axis=-1, keepdims=True)
    return x * lax.rsqrt(ms + NORM_EPS) * g


def _gelu_exact(x):
    return 0.5 * x * (1.0 + lax.erf(x * math.sqrt(0.5)))


def _load_weight_bf16(src_hbm, dst_ref, stage_ref, sem_ref, col_scale=None):
    n_rows, width = src_hbm.shape
    chunk = min(stage_ref.shape[1], n_rows)
    n_chunks = n_rows // chunk
    assert n_chunks * chunk == n_rows and width <= stage_ref.shape[2]

    def copy(c, slot):
        return pltpu.make_async_copy(src_hbm.at[pl.ds(c * chunk, chunk), :],
                                     stage_ref.at[slot, pl.ds(0, chunk), pl.ds(0, width)],
                                     sem_ref.at[slot])

    copy(0, 0).start()

    def body(c, carry):
        slot = lax.rem(c, 2)

        @pl.when(c + 1 < n_chunks)
        def _():
            copy(c + 1, 1 - slot).start()

        copy(c, slot).wait()
        w = stage_ref[slot, pl.ds(0, chunk), pl.ds(0, width)]
        if col_scale is not None:
            w = w * col_scale
        rows = pl.ds(pl.multiple_of(c * chunk, chunk), chunk)
        dst_ref[rows, :] = w.astype(BF16)
        return carry

    lax.fori_loop(0, n_chunks, body, 0)


def _layer_kernel(x_ref, p_ref, g_ref, win_hbm, qscale_ref, cost_ref, sint_ref,
                  cosk_ref, sak_ref, sbk_ref, lng_ref, lnb_ref,
                  ws_ref, bs_ref, ag_ref, gg_ref, wo_hbm, pg_ref,
                  g1_ref, wup_hbm, wdn_hbm, g2_ref, wgate_hbm, bgate_ref, wple_hbm, g3_ref,
                  o_ref,
                  k_scr, vt_scr, km_scr, ug_scr, vn_scr, h1_scr,
                  win_ref, wo_ref, wup_ref, wdn_ref, wgate_ref, wple_ref, wsm_ref, stage, sem,
                  *scratch):
    sel_scr, qz_scr, m_scr, acc_scr, s_scr = scratch
    nb = k_scr.shape[0]
    tq = MOBA_BLOCK
    n_tiles = pl.num_programs(0) - 1
    step = pl.program_id(0)
    live = step < n_tiles
    i = lax.rem(jnp.minimum(step, n_tiles - 1), nb)
    slot = lax.rem(step, 2)

    @pl.when(step == 0)
    def _():
        km_scr[...] = jnp.zeros(km_scr.shape, F32)
        _load_weight_bf16(win_hbm, win_ref, stage, sem, col_scale=qscale_ref[...])
        _load_weight_bf16(wo_hbm, wo_ref, stage, sem)
        _load_weight_bf16(wup_hbm, wup_ref, stage, sem)
        _load_weight_bf16(wdn_hbm, wdn_ref, stage, sem)
        _load_weight_bf16(wgate_hbm, wgate_ref, stage, sem)
        _load_weight_bf16(wple_hbm, wple_ref, stage, sem)
        t_idx = lax.broadcasted_iota(jnp.int32, (GMLP_CHUNK, GMLP_CHUNK), 0)
        s_idx = lax.broadcasted_iota(jnp.int32, (GMLP_CHUNK, GMLP_CHUNK), 1)
        for g in range(N_GROUPS):
            tile, part = divmod(g, GROUPS_PER_LANE_TILE)
            wsm_ref[tile, :, part * GMLP_CHUNK:(part + 1) * GMLP_CHUNK] = (
                jnp.where(s_idx <= t_idx, ws_ref[g], 0.0).astype(BF16))

    def scores(blk, h):
        hp = h // HEADS_PER_LANE_TILE
        return _dot(k_scr[blk, :, hp * LANES:(hp + 1) * LANES], qz_scr[h])

    def v_rows(blk, h):
        return vt_scr[blk, h * V_ROWS:(h + 1) * V_ROWS, :]

    @pl.when(live)
    def _():
        x = x_ref[0]
        hn = _rms_rows(x, g_ref[...]).astype(BF16)

        gv = _dot(hn, win_ref[:, 3 * D_ATTN + D_GMLP:])
        g2 = _gelu_exact(gv)
        mu = jnp.mean(g2, axis=-1, keepdims=True)
        xc = g2 - mu
        var = jnp.mean(xc * xc, axis=-1, keepdims=True)
        vn = xc * lax.rsqrt(var + NORM_EPS) * lng_ref[...] + lnb_ref[...]
        vn_scr[...] = vn.astype(BF16)

        qt = _dot(hn, win_ref[:, :D_ATTN]).T

        tok = _dot(hn, win_ref[:, D_ATTN:2 * D_ATTN])
        cos_k = cosk_ref[i]
        sa_k = sak_ref[i]
        sb_k = sbk_ref[i]
        kparts = []
        for j in range(D_ATTN // LANES):
            kb = tok[:, j * LANES:(j + 1) * LANES]
            up = pltpu.roll(kb, LANES - ROPE_HALF, axis=1)
            dn = pltpu.roll(kb, ROPE_HALF, axis=1)
            kparts.append(kb * cos_k + up * sa_k + dn * sb_k)
        k = jnp.concatenate(kparts, axis=1)
        k_scr[i] = k.astype(BF16)
        km_scr[pl.ds(i, 1), :] = jnp.mean(k, axis=0, keepdims=True)

        u = _dot(hn, win_ref[:, 3 * D_ATTN:3 * D_ATTN + D_GMLP])
        ug_scr[...] = _gelu_exact(u).astype(BF16)

        vt = _dot(hn, win_ref[:, 2 * D_ATTN:3 * D_ATTN]).T
        ones = jnp.ones((V_ROWS - HEAD_DIM, tq), F32)
        vparts = []
        for h in range(N_HEADS):
            vparts.append(vt[h * HEAD_DIM:(h + 1) * HEAD_DIM])
            vparts.append(ones)
        vt_scr[i] = jnp.concatenate(vparts, axis=0).astype(BF16)

        cos_t = cost_ref[i]
        sin_t = sint_ref[i]
        zeros_head = jnp.zeros((HEAD_DIM, tq), F32)
        for h in range(N_HEADS):
            hh = h % HEADS_PER_LANE_TILE
            r0 = h * HEAD_DIM
            x1 = qt[r0:r0 + ROPE_HALF]
            x2 = qt[r0 + ROPE_HALF:r0 + ROPE_DIM]
            q_h = [x1 * cos_t - x2 * sin_t, x2 * cos_t + x1 * sin_t, qt[r0 + ROPE_DIM:r0 + HEAD_DIM]]
            padded = [zeros_head] * hh + q_h + [zeros_head] * (HEADS_PER_LANE_TILE - 1 - hh)
            qz_scr[h] = jnp.concatenate(padded, axis=0).astype(BF16)

    @pl.when(live)
    def _():
        blk_row = lax.broadcasted_iota(jnp.int32, (nb, tq), 0)
        past = blk_row < i
        km = km_scr[...]
        km_hi = km.astype(BF16)
        r1 = km - km_hi.astype(F32)
        km_mid = r1.astype(BF16)
        km_lo = (r1 - km_mid.astype(F32)).astype(BF16)
        km3 = jnp.concatenate([km_hi, km_mid, km_lo], axis=0)
        for h in range(N_HEADS):
            hp = h // HEADS_PER_LANE_TILE
            s_scr[h] = scores(0, h)

            g3 = _dot(km3[:, hp * LANES:(hp + 1) * LANES], qz_scr[h])
            gate = g3[0:nb] + g3[nb:2 * nb] + g3[2 * nb:3 * nb]
            gate = jnp.where(past, gate, NEG_INF)
            rank = jnp.zeros((nb, tq), jnp.int32)
            for m in range(nb):
                gm = gate[m:m + 1, :]
                beats = (gm > gate) | ((gm == gate) & (m < blk_row))
                rank = rank + beats.astype(jnp.int32)
            sel = past & (rank < MOBA_TOPK)
            sel_scr[h] = sel.astype(F32)
            m_scr[h] = jnp.full((1, tq), NEG_INF, F32)
            acc_scr[h] = jnp.zeros((V_ROWS, tq), F32)

    def past_blocks(kb0, n_blocks):
        units = [(kb0 + j, h) for j in range(n_blocks) for h in range(N_HEADS)]
        ahead = {}
        for u in range(-SCORE_LEAD, len(units)):
            if 0 <= u + SCORE_LEAD < len(units):
                blk, head = units[u + SCORE_LEAD]
                ahead[u + SCORE_LEAD] = scores(blk + 1, head)
            if u < 0:
                continue
            kb, h = units[u]
            s = s_scr[h]
            keep = sel_scr[h, pl.ds(kb, 1), :] > 0.5
            m_run = m_scr[h]
            m_new = jnp.where(keep, jnp.maximum(m_run, jnp.max(s, axis=0, keepdims=True)), m_run)
            m_sub = jnp.where(keep, m_new, BIG)
            alpha = jnp.exp2(m_run - m_new)
            pb = jnp.exp2((s - m_sub).astype(BF16))
            m_scr[h] = m_new
            acc_scr[h] = alpha * acc_scr[h] + _dot(v_rows(kb, h), pb)
            s_scr[h] = ahead.pop(u)

    def four_blocks(t, carry):
        past_blocks(4 * t, 4)
        return carry

    n_past = jnp.where(live, i, 0)
    lax.fori_loop(0, n_past // 4, four_blocks, 0)
    n_done = 4 * (n_past // 4)

    @pl.when(n_past - n_done >= 2)
    def _():
        past_blocks(n_done, 2)

    @pl.when(lax.rem(n_past, 2) == 1)
    def _():
        past_blocks(n_past - 1, 1)

    def mix_tail(with_mlp):
        mlp = {}

        def mlp_begin():
            mlp['h'] = h1_scr[1 - slot]
            mlp['hn'] = _rms_rows(mlp['h'], g1_ref[...]).astype(BF16)
            mlp['acc'] = None

        def mlp_up(c):
            up = jnp.maximum(_dot(mlp['hn'], wup_ref[:, c * FF_CHUNK:(c + 1) * FF_CHUNK]), 0.0)
            mlp['up'] = (up * up).astype(BF16)

        def mlp_down(c):
            d = _dot(mlp['up'], wdn_ref[c * FF_CHUNK:(c + 1) * FF_CHUNK, :])
            mlp['acc'] = d if mlp['acc'] is None else mlp['acc'] + d

        def mlp_gate():
            mlp['h2'] = mlp['h'] + _rms_rows(mlp['acc'], g2_ref[...])
            mlp['z'] = _dot(mlp['h2'].astype(BF16), wgate_ref[...]) + bgate_ref[...]

        def mlp_ple():
            mlp['ple'] = _dot(p_ref[0].astype(BF16), wple_ref[...])

        def mlp_end():
            gate = 1.0 / (1.0 + jnp.exp(-mlp['z']))
            o_ref[0] = mlp['h2'] + _rms_rows(mlp['ple'] * gate, g3_ref[...])

        n_ff = D_FF // FF_CHUNK
        heads_per_piece = N_HEADS // (2 * n_ff)
        assert heads_per_piece * 2 * n_ff == N_HEADS
        pieces_before_head = [[mlp_begin]] + [[] for _ in range(N_HEADS - 1)]
        for c in range(n_ff):
            pieces_before_head[2 * c * heads_per_piece].append(functools.partial(mlp_up, c))
            pieces_before_head[(2 * c + 1) * heads_per_piece].append(functools.partial(mlp_down, c))

        kpos = lax.broadcasted_iota(jnp.int32, (MOBA_BLOCK, tq), 0)
        qpos = lax.broadcasted_iota(jnp.int32, (MOBA_BLOCK, tq), 1)
        causal = kpos <= qpos
        lane = lax.broadcasted_iota(jnp.int32, (GMLP_CHUNK, LANES), 1)
        n_lane_tiles = D_GMLP // LANES
        n_chunks = tq // GMLP_CHUNK
        assert n_lane_tiles * n_chunks == N_HEADS
        heads_out = []
        cols = [[None] * n_lane_tiles for _ in range(n_chunks)]
        for h in range(N_HEADS):
            for piece in pieces_before_head[h] if with_mlp else ():
                piece()
            c, gp = divmod(h, n_lane_tiles)
            vp = vn_scr[c * GMLP_CHUNK:(c + 1) * GMLP_CHUNK, gp * LANES:(gp + 1) * LANES]
            stacked = jnp.concatenate(
                [jnp.where(lane // GROUP_DIM == part, vp, jnp.zeros_like(vp))
                 for part in range(GROUPS_PER_LANE_TILE)], axis=0)
            cols[c][gp] = _dot(wsm_ref[gp], stacked)

            st = jnp.where(causal, s_scr[h], NEG_INF)
            m_run = m_scr[h]
            m_new = jnp.maximum(m_run, jnp.max(st, axis=0, keepdims=True))
            alpha = jnp.exp2(m_run - m_new)
            p = jnp.exp2((st - m_new).astype(BF16))
            acc = alpha * acc_scr[h] + _dot(v_rows(i, h), p)
            heads_out.append(acc[0:HEAD_DIM] / acc[HEAD_DIM:HEAD_DIM + 1])

        if with_mlp:
            mlp_ple()
        mixed = jnp.concatenate(
            [jnp.concatenate(cols[c], axis=1) + bs_ref[...] for c in range(n_chunks)], axis=0)
        gm_out = ug_scr[...].astype(F32) * mixed
        gm_n = _rms_rows(gm_out, gg_ref[...]).astype(BF16)
        y = _dot(gm_n, wo_ref[D_ATTN:, :])
        if with_mlp:
            mlp_gate()

        attn_t = jnp.concatenate(heads_out, axis=0)
        ms_a = jnp.mean(attn_t * attn_t, axis=0, keepdims=True)
        attn_n = (attn_t * lax.rsqrt(ms_a + NORM_EPS) * ag_ref[...]).astype(BF16)
        y = y + _dot_tn(attn_n, wo_ref[:D_ATTN, :])
        h1_scr[slot] = x_ref[0] + _rms_rows(y, pg_ref[...])
        if with_mlp:
            mlp_end()

    @pl.when(step > 0)
    def _():
        mix_tail(True)

    @pl.when(step == 0)
    def _():
        mix_tail(False)


def _const_spec(shape):
    zeros = (0,) * len(shape)
    return pl.BlockSpec(shape, lambda *_: zeros, pipeline_mode=pl.Buffered(1))


def _rope_tables(seq, block):
    inv_freq = ROPE_THETA ** (-np.arange(ROPE_HALF, dtype=np.float64) / ROPE_HALF)
    ang = np.arange(seq, dtype=np.float64)[:, None] * inv_freq[None, :]
    cos, sin = np.cos(ang), np.sin(ang)
    n_blocks = seq // block
    cos_t, sin_t = (t.T.reshape(ROPE_HALF, n_blocks, block).transpose(1, 0, 2) for t in (cos, sin))
    ones = np.ones((seq, HEAD_DIM - ROPE_DIM))
    zeros = np.zeros((seq, HEAD_DIM - ROPE_DIM))
    zh = np.zeros((seq, ROPE_HALF))
    cos_h = np.concatenate([cos, cos, ones], axis=1)
    sa_h = np.concatenate([-sin, zh, zeros], axis=1)
    sb_h = np.concatenate([zh, sin, zeros], axis=1)
    tile = lambda t: np.concatenate([t] * HEADS_PER_LANE_TILE, axis=1).reshape(n_blocks, block, LANES)
    tables = (cos_t, sin_t, tile(cos_h), tile(sa_h), tile(sb_h))
    return tuple(jnp.asarray(np.ascontiguousarray(t), dtype=F32) for t in tables)


def _layer(h, p, mix_pre_g, w_in, gmlp_ln_g, gmlp_ln_b, w_s, b_s, attn_out_g,
           gmlp_out_g, w_o, mix_post_g, mlp_pre_g, w_up, w_down, mlp_post_g,
           w_ple, w_ple_gate, b_ple_gate, ple_post_g):
    B, S, D = h.shape
    assert D == D_MODEL and S % MOBA_BLOCK == 0
    nb = S // MOBA_BLOCK
    T = MOBA_BLOCK
    n_tiles = B * nb
    row = lambda v: v.reshape(1, -1).astype(F32)

    qscale = np.ones((1, D_IN_PROJ), np.float32)
    qscale[:, :D_ATTN] = HEAD_DIM ** -0.5 * math.log2(math.e)
    cos_t, sin_t, cos_k, sa_k, sb_k = _rope_tables(S, T)
    bs_exp = jnp.repeat(b_s.T, GROUP_DIM, axis=1).astype(F32)

    def cur(s):
        t = jnp.minimum(s, n_tiles - 1)
        return t // nb, t % nb

    def prev(s):
        t = jnp.maximum(s - 1, 0)
        return t // nb, t % nb

    hbm = pl.BlockSpec(memory_space=pl.ANY)
    out = pl.pallas_call(
        _layer_kernel,
        grid=(n_tiles + 1,),
        in_specs=[
            pl.BlockSpec((1, T, D), lambda s: (*cur(s), 0)),
            pl.BlockSpec((1, T, D_PLE), lambda s: (*prev(s), 0)),
            _const_spec((1, D)),
            hbm,
            _const_spec((1, D_IN_PROJ)),
            _const_spec((nb, ROPE_HALF, T)),
            _const_spec((nb, ROPE_HALF, T)),
            _const_spec((nb, T, LANES)),
            _const_spec((nb, T, LANES)),
            _const_spec((nb, T, LANES)),
            _const_spec((1, D_GMLP)),
            _const_spec((1, D_GMLP)),
            _const_spec((N_GROUPS, GMLP_CHUNK, GMLP_CHUNK)),
            _const_spec((GMLP_CHUNK, D_GMLP)),
            _const_spec((D_ATTN, 1)),
            _const_spec((1, D_GMLP)),
            hbm,
            _const_spec((1, D)),
            _const_spec((1, D)),
            hbm,
            hbm,
            _const_spec((1, D)),
            hbm,
            _const_spec((1, D)),
            hbm,
            _const_spec((1, D)),
        ],
        out_specs=pl.BlockSpec((1, T, D), lambda s: (*prev(s), 0)),
        out_shape=jax.ShapeDtypeStruct((B, S, D), F32),
        scratch_shapes=([pltpu.VMEM((nb, T, D_ATTN), BF16),
                         pltpu.VMEM((nb, N_HEADS * V_ROWS, T), BF16),
                         pltpu.VMEM((nb, D_ATTN), F32),
                         pltpu.VMEM((T, D_GMLP), BF16),
                         pltpu.VMEM((T, D_GMLP), BF16),
                         pltpu.VMEM((2, T, D), F32),
                         pltpu.VMEM((D, D_IN_PROJ), BF16),
                         pltpu.VMEM((D_ATTN + D_GMLP, D), BF16),
                         pltpu.VMEM((D, D_FF), BF16),
                         pltpu.VMEM((D_FF, D), BF16),
                         pltpu.VMEM((D, D), BF16),
                         pltpu.VMEM((D_PLE, D), BF16),
                         pltpu.VMEM((D_GMLP // LANES, GMLP_CHUNK, GROUPS_PER_LANE_TILE * GMLP_CHUNK), BF16),
                         pltpu.VMEM((2, STAGE_ROWS, D_FF), F32),
                         pltpu.SemaphoreType.DMA((2,))]
                        + [pltpu.VMEM((N_HEADS, nb, T), F32),
                           pltpu.VMEM((N_HEADS, LANES, T), BF16),
                           pltpu.VMEM((N_HEADS, 1, T), F32),
                           pltpu.VMEM((N_HEADS, V_ROWS, T), F32),
                           pltpu.VMEM((N_HEADS, MOBA_BLOCK, T), F32)]),
        compiler_params=pltpu.CompilerParams(dimension_semantics=("arbitrary",),
                                             vmem_limit_bytes=VMEM_LIMIT_BYTES),
        name="layer",
    )(h, p, row(mix_pre_g), w_in.astype(F32), jnp.asarray(qscale), cos_t, sin_t, cos_k, sa_k, sb_k,
      row(gmlp_ln_g), row(gmlp_ln_b), w_s.astype(F32), bs_exp,
      attn_out_g.reshape(-1, 1).astype(F32), row(gmlp_out_g), w_o.astype(F32), row(mix_post_g),
      row(mlp_pre_g), w_up.astype(F32), w_down.astype(F32), row(mlp_post_g),
      w_ple_gate.astype(F32), row(b_ple_gate), w_ple.astype(F32), row(ple_post_g))
    return out


def kernel(x, p, mix_pre_g, w_in, gmlp_ln_g, gmlp_ln_b, w_s, b_s, attn_out_g,
           gmlp_out_g, w_o, mix_post_g, mlp_pre_g, w_up, w_down, mlp_post_g,
           w_ple, w_ple_gate, b_ple_gate, ple_post_g):
    h = x
    for i in range(w_in.shape[0]):
        h = _layer(h, p[i], mix_pre_g[i], w_in[i], gmlp_ln_g[i], gmlp_ln_b[i],
                   w_s[i], b_s[i], attn_out_g[i], gmlp_out_g[i], w_o[i],
                   mix_post_g[i], mlp_pre_g[i], w_up[i], w_down[i], mlp_post_g[i],
                   w_ple[i], w_ple_gate[i], b_ple_gate[i], ple_post_g[i])
    return h
```

```python
import functools
import math

import jax
import jax.numpy as jnp
import numpy as np
from jax import lax
from jax.experimental import pallas as pl
from jax.experimental.pallas import tpu as pltpu

D_MODEL = 1024
D_PLE = 256
N_HEADS = 8
HEAD_DIM = 64
D_ATTN = N_HEADS * HEAD_DIM
N_GROUPS = 8
GROUP_DIM = 64
D_GMLP = N_GROUPS * GROUP_DIM
D_IN_PROJ = 3 * D_ATTN + 2 * D_GMLP
D_FF = 4 * D_MODEL
MOBA_BLOCK = 256
MOBA_TOPK = 3
GMLP_CHUNK = 128
ROPE_THETA = 500000.0
ROPE_DIM = HEAD_DIM // 4
ROPE_HALF = ROPE_DIM // 2
NORM_EPS = 1e-6
NEG_INF = -1e30

LANES = 128
HEADS_PER_LANE_TILE = LANES // HEAD_DIM
GROUPS_PER_LANE_TILE = LANES // GROUP_DIM
BF16_SUBLANES = 16
V_ROWS = HEAD_DIM + BF16_SUBLANES
BIG = 1e30
SCORE_LEAD = 2
VMEM_LIMIT_BYTES = 58 * 1024 * 1024
FF_CHUNK = 1024
STAGE_ROWS = 128

F32 = jnp.float32
BF16 = jnp.bfloat16


def _dot(a, b):
    return jnp.dot(a, b, preferred_element_type=F32)


def _dot_tn(a, b):
    return lax.dot_general(a, b, (((0,), (0,)), ((), ())), preferred_element_type=F32)


def _rms_rows(x, g):
    ms = jnp.mean(x * x, axis=-1, keepdims=True)
    return x * lax.rsqrt(ms + NORM_EPS) * g


def _gelu_exact(x):
    return 0.5 * x * (1.0 + lax.erf(x * math.sqrt(0.5)))


def _load_weight_bf16(src_hbm, dst_ref, stage_ref, sem_ref, col_scale=None):
    n_rows, width = src_hbm.shape
    chunk = min(stage_ref.shape[1], n_rows)
    n_chunks = n_rows // chunk
    assert n_chunks * chunk == n_rows and width <= stage_ref.shape[2]

    def copy(c, slot):
        return pltpu.make_async_copy(src_hbm.at[pl.ds(c * chunk, chunk), :],
                                     stage_ref.at[slot, pl.ds(0, chunk), pl.ds(0, width)],
                                     sem_ref.at[slot])

    copy(0, 0).start()

    def body(c, carry):
        slot = lax.rem(c, 2)

        @pl.when(c + 1 < n_chunks)
        def _():
            copy(c + 1, 1 - slot).start()

        copy(c, slot).wait()
        w = stage_ref[slot, pl.ds(0, chunk), pl.ds(0, width)]
        if col_scale is not None:
            w = w * col_scale
        rows = pl.ds(pl.multiple_of(c * chunk, chunk), chunk)
        dst_ref[rows, :] = w.astype(BF16)
        return carry

    lax.fori_loop(0, n_chunks, body, 0)


def _layer_kernel(x_ref, p_ref, g_ref, win_hbm, qscale_ref, cost_ref, sint_ref,
                  cosk_ref, sak_ref, sbk_ref, lng_ref, lnb_ref,
                  ws_ref, bs_ref, ag_ref, gg_ref, wo_hbm, pg_ref,
                  g1_ref, wup_hbm, wdn_hbm, g2_ref, wgate_hbm, bgate_ref, wple_hbm, g3_ref,
                  o_ref,
                  k_scr, vt_scr, km_scr, ug_scr, vn_scr, h1_scr,
                  win_ref, wo_ref, wup_ref, wdn_ref, wgate_ref, wple_ref, wsm_ref, stage, sem,
                  *scratch):
    sel_scr, qz_scr, m_scr, acc_scr, s_scr = scratch
    nb = k_scr.shape[0]
    tq = MOBA_BLOCK
    n_tiles = pl.num_programs(0) - 1
    step = pl.program_id(0)
    live = step < n_tiles
    i = lax.rem(jnp.minimum(step, n_tiles - 1), nb)
    slot = lax.rem(step, 2)

    @pl.when(step == 0)
    def _():
        km_scr[...] = jnp.zeros(km_scr.shape, F32)
        _load_weight_bf16(win_hbm, win_ref, stage, sem, col_scale=qscale_ref[...])
        _load_weight_bf16(wo_hbm, wo_ref, stage, sem)
        _load_weight_bf16(wup_hbm, wup_ref, stage, sem)
        _load_weight_bf16(wdn_hbm, wdn_ref, stage, sem)
        _load_weight_bf16(wgate_hbm, wgate_ref, stage, sem)
        _load_weight_bf16(wple_hbm, wple_ref, stage, sem)
        t_idx = lax.broadcasted_iota(jnp.int32, (GMLP_CHUNK, GMLP_CHUNK), 0)
        s_idx = lax.broadcasted_iota(jnp.int32, (GMLP_CHUNK, GMLP_CHUNK), 1)
        for g in range(N_GROUPS):
            tile, part = divmod(g, GROUPS_PER_LANE_TILE)
            wsm_ref[tile, :, part * GMLP_CHUNK:(part + 1) * GMLP_CHUNK] = (
                jnp.where(s_idx <= t_idx, ws_ref[g], 0.0).astype(BF16))

    def scores(blk, h):
        hp = h // HEADS_PER_LANE_TILE
        return _dot(k_scr[blk, :, hp * LANES:(hp + 1) * LANES], qz_scr[h])

    def v_rows(blk, h):
        return vt_scr[blk, h * V_ROWS:(h + 1) * V_ROWS, :]

    @pl.when(live)
    def _():
        x = x_ref[0]
        hn = _rms_rows(x, g_ref[...]).astype(BF16)

        gv = _dot(hn, win_ref[:, 3 * D_ATTN + D_GMLP:])
        g2 = _gelu_exact(gv)
        mu = jnp.mean(g2, axis=-1, keepdims=True)
        xc = g2 - mu
        var = jnp.mean(xc * xc, axis=-1, keepdims=True)
        vn = xc * lax.rsqrt(var + NORM_EPS) * lng_ref[...] + lnb_ref[...]
        vn_scr[...] = vn.astype(BF16)

        qt = _dot(hn, win_ref[:, :D_ATTN]).T

        tok = _dot(hn, win_ref[:, D_ATTN:2 * D_ATTN])
        cos_k = cosk_ref[i]
        sa_k = sak_ref[i]
        sb_k = sbk_ref[i]
        kparts = []
        for j in range(D_ATTN // LANES):
            kb = tok[:, j * LANES:(j + 1) * LANES]
            up = pltpu.roll(kb, LANES - ROPE_HALF, axis=1)
            dn = pltpu.roll(kb, ROPE_HALF, axis=1)
            kparts.append(kb * cos_k + up * sa_k + dn * sb_k)
        k = jnp.concatenate(kparts, axis=1)
        k_scr[i] = k.astype(BF16)
        km_scr[pl.ds(i, 1), :] = jnp.mean(k, axis=0, keepdims=True)

        u = _dot(hn, win_ref[:, 3 * D_ATTN:3 * D_ATTN + D_GMLP])
        ug_scr[...] = _gelu_exact(u).astype(BF16)

        vt = _dot(hn, win_ref[:, 2 * D_ATTN:3 * D_ATTN]).T
        ones = jnp.ones((V_ROWS - HEAD_DIM, tq), F32)
        vparts = []
        for h in range(N_HEADS):
            vparts.append(vt[h * HEAD_DIM:(h + 1) * HEAD_DIM])
            vparts.append(ones)
        vt_scr[i] = jnp.concatenate(vparts, axis=0).astype(BF16)

        cos_t = cost_ref[i]
        sin_t = sint_ref[i]
        zeros_head = jnp.zeros((HEAD_DIM, tq), F32)
        for h in range(N_HEADS):
            hh = h % HEADS_PER_LANE_TILE
            r0 = h * HEAD_DIM
            x1 = qt[r0:r0 + ROPE_HALF]
            x2 = qt[r0 + ROPE_HALF:r0 + ROPE_DIM]
            q_h = [x1 * cos_t - x2 * sin_t, x2 * cos_t + x1 * sin_t, qt[r0 + ROPE_DIM:r0 + HEAD_DIM]]
            padded = [zeros_head] * hh + q_h + [zeros_head] * (HEADS_PER_LANE_TILE - 1 - hh)
            qz_scr[h] = jnp.concatenate(padded, axis=0).astype(BF16)

        blk_row = lax.broadcasted_iota(jnp.int32, (nb, tq), 0)
        past = blk_row < i
        km = km_scr[...]
        km_hi = km.astype(BF16)
        r1 = km - km_hi.astype(F32)
        km_mid = r1.astype(BF16)
        km_lo = (r1 - km_mid.astype(F32)).astype(BF16)
        km3 = jnp.concatenate([km_hi, km_mid, km_lo], axis=0)
        for h in range(N_HEADS):
            hp = h // HEADS_PER_LANE_TILE
            s_scr[h] = scores(0, h)

            g3 = _dot(km3[:, hp * LANES:(hp + 1) * LANES], qz_scr[h])
            gate = g3[0:nb] + g3[nb:2 * nb] + g3[2 * nb:3 * nb]
            gate = jnp.where(past, gate, NEG_INF)
            rank = jnp.zeros((nb, tq), jnp.int32)
            for m in range(nb):
                gm = gate[m:m + 1, :]
                beats = (gm > gate) | ((gm == gate) & (m < blk_row))
                rank = rank + beats.astype(jnp.int32)
            sel = past & (rank < MOBA_TOPK)
            sel_scr[h] = sel.astype(F32)
            m_scr[h] = jnp.full((1, tq), NEG_INF, F32)
            acc_scr[h] = jnp.zeros((V_ROWS, tq), F32)

    def past_blocks(kb0, n_blocks):
        units = [(kb0 + j, h) for j in range(n_blocks) for h in range(N_HEADS)]
        ahead = {}
        for u in range(-SCORE_LEAD, len(units)):
            if 0 <= u + SCORE_LEAD < len(units):
                blk, head = units[u + SCORE_LEAD]
                ahead[u + SCORE_LEAD] = scores(blk + 1, head)
            if u < 0:
                continue
            kb, h = units[u]
            s = s_scr[h]
            keep = sel_scr[h, pl.ds(kb, 1), :] > 0.5
            m_run = m_scr[h]
            m_new = jnp.where(keep, jnp.maximum(m_run, jnp.max(s, axis=0, keepdims=True)), m_run)
            m_sub = jnp.where(keep, m_new, BIG)
            alpha = jnp.exp2(m_run - m_new)
            pb = jnp.exp2((s - m_sub).astype(BF16))
            m_scr[h] = m_new
            acc_scr[h] = alpha * acc_scr[h] + _dot(v_rows(kb, h), pb)
            s_scr[h] = ahead.pop(u)

    def four_blocks(t, carry):
        past_blocks(4 * t, 4)
        return carry

    n_past = jnp.where(live, i, 0)
    lax.fori_loop(0, n_past // 4, four_blocks, 0)
    n_done = 4 * (n_past // 4)

    @pl.when(n_past - n_done >= 2)
    def _():
        past_blocks(n_done, 2)

    @pl.when(lax.rem(n_past, 2) == 1)
    def _():
        past_blocks(n_past - 1, 1)

    def mix_tail(with_mlp):
        mlp = {}

        def mlp_begin():
            mlp['h'] = h1_scr[1 - slot]
            mlp['hn'] = _rms_rows(mlp['h'], g1_ref[...]).astype(BF16)
            mlp['acc'] = None

        def mlp_up(c):
            up = jnp.maximum(_dot(mlp['hn'], wup_ref[:, c * FF_CHUNK:(c + 1) * FF_CHUNK]), 0.0)
            mlp['up'] = (up * up).astype(BF16)

        def mlp_down(c):
            d = _dot(mlp['up'], wdn_ref[c * FF_CHUNK:(c + 1) * FF_CHUNK, :])
            mlp['acc'] = d if mlp['acc'] is None else mlp['acc'] + d

        def mlp_gate():
            mlp['h2'] = mlp['h'] + _rms_rows(mlp['acc'], g2_ref[...])
            mlp['z'] = _dot(mlp['h2'].astype(BF16), wgate_ref[...]) + bgate_ref[...]

        def mlp_ple():
            mlp['ple'] = _dot(p_ref[0].astype(BF16), wple_ref[...])

        def mlp_end():
            gate = 1.0 / (1.0 + jnp.exp(-mlp['z']))
            o_ref[0] = mlp['h2'] + _rms_rows(mlp['ple'] * gate, g3_ref[...])

        n_ff = D_FF // FF_CHUNK
        heads_per_piece = N_HEADS // (2 * n_ff)
        assert heads_per_piece * 2 * n_ff == N_HEADS
        pieces_before_head = [[mlp_begin]] + [[] for _ in range(N_HEADS - 1)]
        for c in range(n_ff):
            pieces_before_head[2 * c * heads_per_piece].append(functools.partial(mlp_up, c))
            pieces_before_head[(2 * c + 1) * heads_per_piece].append(functools.partial(mlp_down, c))

        kpos = lax.broadcasted_iota(jnp.int32, (MOBA_BLOCK, tq), 0)
        qpos = lax.broadcasted_iota(jnp.int32, (MOBA_BLOCK, tq), 1)
        causal = kpos <= qpos
        lane = lax.broadcasted_iota(jnp.int32, (GMLP_CHUNK, LANES), 1)
        n_lane_tiles = D_GMLP // LANES
        n_chunks = tq // GMLP_CHUNK
        assert n_lane_tiles * n_chunks == N_HEADS
        heads_out = []
        cols = [[None] * n_lane_tiles for _ in range(n_chunks)]
        for h in range(N_HEADS):
            for piece in pieces_before_head[h] if with_mlp else ():
                piece()
            c, gp = divmod(h, n_lane_tiles)
            vp = vn_scr[c * GMLP_CHUNK:(c + 1) * GMLP_CHUNK, gp * LANES:(gp + 1) * LANES]
            stacked = jnp.concatenate(
                [jnp.where(lane // GROUP_DIM == part, vp, jnp.zeros_like(vp))
                 for part in range(GROUPS_PER_LANE_TILE)], axis=0)
            cols[c][gp] = _dot(wsm_ref[gp], stacked)

            st = jnp.where(causal, s_scr[h], NEG_INF)
            m_run = m_scr[h]
            m_new = jnp.maximum(m_run, jnp.max(st, axis=0, keepdims=True))
            alpha = jnp.exp2(m_run - m_new)
            p = jnp.exp2((st - m_new).astype(BF16))
            acc = alpha * acc_scr[h] + _dot(v_rows(i, h), p)
            heads_out.append(acc[0:HEAD_DIM] / acc[HEAD_DIM:HEAD_DIM + 1])

        if with_mlp:
            mlp_ple()
        mixed = jnp.concatenate(
            [jnp.concatenate(cols[c], axis=1) + bs_ref[...] for c in range(n_chunks)], axis=0)
        gm_out = ug_scr[...].astype(F32) * mixed
        gm_n = _rms_rows(gm_out, gg_ref[...]).astype(BF16)
        y = _dot(gm_n, wo_ref[D_ATTN:, :])
        if with_mlp:
            mlp_gate()

        attn_t = jnp.concatenate(heads_out, axis=0)
        ms_a = jnp.mean(attn_t * attn_t, axis=0, keepdims=True)
        attn_n = (attn_t * lax.rsqrt(ms_a + NORM_EPS) * ag_ref[...]).astype(BF16)
        y = y + _dot_tn(attn_n, wo_ref[:D_ATTN, :])
        h1_scr[slot] = x_ref[0] + _rms_rows(y, pg_ref[...])
        if with_mlp:
            mlp_end()

    @pl.when(step > 0)
    def _():
        mix_tail(True)

    @pl.when(step == 0)
    def _():
        mix_tail(False)


def _const_spec(shape):
    zeros = (0,) * len(shape)
    return pl.BlockSpec(shape, lambda *_: zeros, pipeline_mode=pl.Buffered(1))


def _rope_tables(seq, block):
    inv_freq = ROPE_THETA ** (-np.arange(ROPE_HALF, dtype=np.float64) / ROPE_HALF)
    ang = np.arange(seq, dtype=np.float64)[:, None] * inv_freq[None, :]
    cos, sin = np.cos(ang), np.sin(ang)
    n_blocks = seq // block
    cos_t, sin_t = (t.T.reshape(ROPE_HALF, n_blocks, block).transpose(1, 0, 2) for t in (cos, sin))
    ones = np.ones((seq, HEAD_DIM - ROPE_DIM))
    zeros = np.zeros((seq, HEAD_DIM - ROPE_DIM))
    zh = np.zeros((seq, ROPE_HALF))
    cos_h = np.concatenate([cos, cos, ones], axis=1)
    sa_h = np.concatenate([-sin, zh, zeros], axis=1)
    sb_h = np.concatenate([zh, sin, zeros], axis=1)
    tile = lambda t: np.concatenate([t] * HEADS_PER_LANE_TILE, axis=1).reshape(n_blocks, block, LANES)
    tables = (cos_t, sin_t, tile(cos_h), tile(sa_h), tile(sb_h))
    return tuple(jnp.asarray(np.ascontiguousarray(t), dtype=F32) for t in tables)


def _layer(h, p, mix_pre_g, w_in, gmlp_ln_g, gmlp_ln_b, w_s, b_s, attn_out_g,
           gmlp_out_g, w_o, mix_post_g, mlp_pre_g, w_up, w_down, mlp_post_g,
           w_ple, w_ple_gate, b_ple_gate, ple_post_g):
    B, S, D = h.shape
    assert D == D_MODEL and S % MOBA_BLOCK == 0
    nb = S // MOBA_BLOCK
    T = MOBA_BLOCK
    n_tiles = B * nb
    row = lambda v: v.reshape(1, -1).astype(F32)

    qscale = np.ones((1, D_IN_PROJ), np.float32)
    qscale[:, :D_ATTN] = HEAD_DIM ** -0.5 * math.log2(math.e)
    cos_t, sin_t, cos_k, sa_k, sb_k = _rope_tables(S, T)
    bs_exp = jnp.repeat(b_s.T, GROUP_DIM, axis=1).astype(F32)

    def cur(s):
        t = jnp.minimum(s, n_tiles - 1)
        return t // nb, t % nb

    def prev(s):
        t = jnp.maximum(s - 1, 0)
        return t // nb, t % nb

    hbm = pl.BlockSpec(memory_space=pl.ANY)
    out = pl.pallas_call(
        _layer_kernel,
        grid=(n_tiles + 1,),
        in_specs=[
            pl.BlockSpec((1, T, D), lambda s: (*cur(s), 0)),
            pl.BlockSpec((1, T, D_PLE), lambda s: (*prev(s), 0)),
            _const_spec((1, D)),
            hbm,
            _const_spec((1, D_IN_PROJ)),
            _const_spec((nb, ROPE_HALF, T)),
            _const_spec((nb, ROPE_HALF, T)),
            _const_spec((nb, T, LANES)),
            _const_spec((nb, T, LANES)),
            _const_spec((nb, T, LANES)),
            _const_spec((1, D_GMLP)),
            _const_spec((1, D_GMLP)),
            _const_spec((N_GROUPS, GMLP_CHUNK, GMLP_CHUNK)),
            _const_spec((GMLP_CHUNK, D_GMLP)),
            _const_spec((D_ATTN, 1)),
            _const_spec((1, D_GMLP)),
            hbm,
            _const_spec((1, D)),
            _const_spec((1, D)),
            hbm,
            hbm,
            _const_spec((1, D)),
            hbm,
            _const_spec((1, D)),
            hbm,
            _const_spec((1, D)),
        ],
        out_specs=pl.BlockSpec((1, T, D), lambda s: (*prev(s), 0)),
        out_shape=jax.ShapeDtypeStruct((B, S, D), F32),
        scratch_shapes=([pltpu.VMEM((nb, T, D_ATTN), BF16),
                         pltpu.VMEM((nb, N_HEADS * V_ROWS, T), BF16),
                         pltpu.VMEM((nb, D_ATTN), F32),
                         pltpu.VMEM((T, D_GMLP), BF16),
                         pltpu.VMEM((T, D_GMLP), BF16),
                         pltpu.VMEM((2, T, D), F32),
                         pltpu.VMEM((D, D_IN_PROJ), BF16),
                         pltpu.VMEM((D_ATTN + D_GMLP, D), BF16),
                         pltpu.VMEM((D, D_FF), BF16),
                         pltpu.VMEM((D_FF, D), BF16),
                         pltpu.VMEM((D, D), BF16),
                         pltpu.VMEM((D_PLE, D), BF16),
                         pltpu.VMEM((D_GMLP // LANES, GMLP_CHUNK, GROUPS_PER_LANE_TILE * GMLP_CHUNK), BF16),
                         pltpu.VMEM((2, STAGE_ROWS, D_FF), F32),
                         pltpu.SemaphoreType.DMA((2,))]
                        + [pltpu.VMEM((N_HEADS, nb, T), F32),
                           pltpu.VMEM((N_HEADS, LANES, T), BF16),
                           pltpu.VMEM((N_HEADS, 1, T), F32),
                           pltpu.VMEM((N_HEADS, V_ROWS, T), F32),
                           pltpu.VMEM((N_HEADS, MOBA_BLOCK, T), F32)]),
        compiler_params=pltpu.CompilerParams(dimension_semantics=("arbitrary",),
                                             vmem_limit_bytes=VMEM_LIMIT_BYTES),
        name="layer",
    )(h, p, row(mix_pre_g), w_in.astype(F32), jnp.asarray(qscale), cos_t, sin_t, cos_k, sa_k, sb_k,
      row(gmlp_ln_g), row(gmlp_ln_b), w_s.astype(F32), bs_exp,
      attn_out_g.reshape(-1, 1).astype(F32), row(gmlp_out_g), w_o.astype(F32), row(mix_post_g),
      row(mlp_pre_g), w_up.astype(F32), w_down.astype(F32), row(mlp_post_g),
      w_ple_gate.astype(F32), row(b_ple_gate), w_ple.astype(F32), row(ple_post_g))
    return out


def kernel(x, p, mix_pre_g, w_in, gmlp_ln_g, gmlp_ln_b, w_s, b_s, attn_out_g,
           gmlp_out_g, w_o, mix_post_g, mlp_pre_g, w_up, w_down, mlp_post_g,
           w_ple, w_ple_gate, b_ple_gate, ple_post_g):
    h = x
    for i in range(w_in.shape[0]):
        h = _layer(h, p[i], mix_pre_g[i], w_in[i], gmlp_ln_g[i], gmlp_ln_b[i],
                   w_s[i], b_s[i], attn_out_g[i], gmlp_out_g[i], w_o[i],
                   mix_post_g[i], mlp_pre_g[i], w_up[i], w_down[i], mlp_post_g[i],
                   w_ple[i], w_ple_gate[i], b_ple_gate[i], ple_post_g[i])
    return h
```

```python
import functools
import math

import jax
import jax.numpy as jnp
import numpy as np
from jax import lax
from jax.experimental import pallas as pl
from jax.experimental.pallas import tpu as pltpu

D_MODEL = 1024
D_PLE = 256
N_HEADS = 8
HEAD_DIM = 64
D_ATTN = N_HEADS * HEAD_DIM
N_GROUPS = 8
GROUP_DIM = 64
D_GMLP = N_GROUPS * GROUP_DIM
D_IN_PROJ = 3 * D_ATTN + 2 * D_GMLP
D_FF = 4 * D_MODEL
MOBA_BLOCK = 256
MOBA_TOPK = 3
GMLP_CHUNK = 128
ROPE_THETA = 500000.0
ROPE_DIM = HEAD_DIM // 4
ROPE_HALF = ROPE_DIM // 2
NORM_EPS = 1e-6
NEG_INF = -1e30

LANES = 128
HEADS_PER_LANE_TILE = LANES // HEAD_DIM
GROUPS_PER_LANE_TILE = LANES // GROUP_DIM
BF16_SUBLANES = 16
V_ROWS = HEAD_DIM + BF16_SUBLANES
BIG = 1e30
SCORE_LEAD = 2
VMEM_LIMIT_BYTES = 58 * 1024 * 1024
FF_CHUNK = 1024
STAGE_ROWS = 128

F32 = jnp.float32
BF16 = jnp.bfloat16


def _dot(a, b):
    return jnp.dot(a, b, preferred_element_type=F32)


def _dot_tn(a, b):
    return lax.dot_general(a, b, (((0,), (0,)), ((), ())), preferred_element_type=F32)


def _rms_rows(x, g):
    ms = jnp.mean(x * x, axis=-1, keepdims=True)
    return x * lax.rsqrt(ms + NORM_EPS) * g


def _gelu_exact(x):
    return 0.5 * x * (1.0 + lax.erf(x * math.sqrt(0.5)))


def _load_weight_bf16(src_hbm, dst_ref, stage_ref, sem_ref, col_scale=None):
    n_rows, width = src_hbm.shape
    chunk = min(stage_ref.shape[1], n_rows)
    n_chunks = n_rows // chunk
    assert n_chunks * chunk == n_rows and width <= stage_ref.shape[2]

    def copy(c, slot):
        return pltpu.make_async_copy(src_hbm.at[pl.ds(c * chunk, chunk), :],
                                     stage_ref.at[slot, pl.ds(0, chunk), pl.ds(0, width)],
                                     sem_ref.at[slot])

    copy(0, 0).start()

    def body(c, carry):
        slot = lax.rem(c, 2)

        @pl.when(c + 1 < n_chunks)
        def _():
            copy(c + 1, 1 - slot).start()

        copy(c, slot).wait()
        w = stage_ref[slot, pl.ds(0, chunk), pl.ds(0, width)]
        if col_scale is not None:
            w = w * col_scale
        rows = pl.ds(pl.multiple_of(c * chunk, chunk), chunk)
        dst_ref[rows, :] = w.astype(BF16)
        return carry

    lax.fori_loop(0, n_chunks, body, 0)


def _layer_kernel(x_ref, p_ref, g_ref, win_hbm, qscale_ref, cost_ref, sint_ref,
                  cosk_ref, sak_ref, sbk_ref, lng_ref, lnb_ref,
                  ws_ref, bs_ref, ag_ref, gg_ref, wo_hbm, pg_ref,
                  g1_ref, wup_hbm, wdn_hbm, g2_ref, wgate_hbm, bgate_ref, wple_hbm, g3_ref,
                  o_ref,
                  k_scr, vt_scr, km_scr, ug_scr, vn_scr, h1_scr,
                  win_ref, wo_ref, wup_ref, wdn_ref, wgate_ref, wple_ref, wsm_ref, stage, sem,
                  *scratch):
    sel_scr, qz_scr, m_scr, acc_scr, s_scr = scratch
    nb = k_scr.shape[0]
    tq = MOBA_BLOCK
    n_tiles = pl.num_programs(0) - 1
    step = pl.program_id(0)
    live = step < n_tiles
    i = lax.rem(jnp.minimum(step, n_tiles - 1), nb)
    slot = lax.rem(step, 2)

    @pl.when(step == 0)
    def _():
        km_scr[...] = jnp.zeros(km_scr.shape, F32)
        _load_weight_bf16(win_hbm, win_ref, stage, sem, col_scale=qscale_ref[...])
        _load_weight_bf16(wo_hbm, wo_ref, stage, sem)
        _load_weight_bf16(wup_hbm, wup_ref, stage, sem)
        _load_weight_bf16(wdn_hbm, wdn_ref, stage, sem)
        _load_weight_bf16(wgate_hbm, wgate_ref, stage, sem)
        _load_weight_bf16(wple_hbm, wple_ref, stage, sem)
        t_idx = lax.broadcasted_iota(jnp.int32, (GMLP_CHUNK, GMLP_CHUNK), 0)
        s_idx = lax.broadcasted_iota(jnp.int32, (GMLP_CHUNK, GMLP_CHUNK), 1)
        for g in range(N_GROUPS):
            tile, part = divmod(g, GROUPS_PER_LANE_TILE)
            wsm_ref[tile, :, part * GMLP_CHUNK:(part + 1) * GMLP_CHUNK] = (
                jnp.where(s_idx <= t_idx, ws_ref[g], 0.0).astype(BF16))

    def scores(blk, h):
        hp = h // HEADS_PER_LANE_TILE
        return _dot(k_scr[blk, :, hp * LANES:(hp + 1) * LANES], qz_scr[h])

    def v_rows(blk, h):
        return vt_scr[blk, h * V_ROWS:(h + 1) * V_ROWS, :]

    @pl.when(live)
    def _():
        x = x_ref[0]
        hn = _rms_rows(x, g_ref[...]).astype(BF16)

        gv = _dot(hn, win_ref[:, 3 * D_ATTN + D_GMLP:])
        g2 = _gelu_exact(gv)
        mu = jnp.mean(g2, axis=-1, keepdims=True)
        xc = g2 - mu
        var = jnp.mean(xc * xc, axis=-1, keepdims=True)
        vn = xc * lax.rsqrt(var + NORM_EPS) * lng_ref[...] + lnb_ref[...]
        vn_scr[...] = vn.astype(BF16)

        qt = _dot(hn, win_ref[:, :D_ATTN]).T

        tok = _dot(hn, win_ref[:, D_ATTN:2 * D_ATTN])
        cos_k = cosk_ref[i]
        sa_k = sak_ref[i]
        sb_k = sbk_ref[i]
        kparts = []
        for j in range(D_ATTN // LANES):
            kb = tok[:, j * LANES:(j + 1) * LANES]
            up = pltpu.roll(kb, LANES - ROPE_HALF, axis=1)
            dn = pltpu.roll(kb, ROPE_HALF, axis=1)
            kparts.append(kb * cos_k + up * sa_k + dn * sb_k)
        k = jnp.concatenate(kparts, axis=1)
        k_scr[i] = k.astype(BF16)
        km_scr[pl.ds(i, 1), :] = jnp.mean(k, axis=0, keepdims=True)

        u = _dot(hn, win_ref[:, 3 * D_ATTN:3 * D_ATTN + D_GMLP])
        ug_scr[...] = _gelu_exact(u).astype(BF16)

        vt = _dot(hn, win_ref[:, 2 * D_ATTN:3 * D_ATTN]).T
        ones = jnp.ones((V_ROWS - HEAD_DIM, tq), F32)
        vparts = []
        for h in range(N_HEADS):
            vparts.append(vt[h * HEAD_DIM:(h + 1) * HEAD_DIM])
            vparts.append(ones)
        vt_scr[i] = jnp.concatenate(vparts, axis=0).astype(BF16)

        cos_t = cost_ref[i]
        sin_t = sint_ref[i]
        zeros_head = jnp.zeros((HEAD_DIM, tq), F32)
        for h in range(N_HEADS):
            hh = h % HEADS_PER_LANE_TILE
            r0 = h * HEAD_DIM
            x1 = qt[r0:r0 + ROPE_HALF]
            x2 = qt[r0 + ROPE_HALF:r0 + ROPE_DIM]
            q_h = [x1 * cos_t - x2 * sin_t, x2 * cos_t + x1 * sin_t, qt[r0 + ROPE_DIM:r0 + HEAD_DIM]]
            padded = [zeros_head] * hh + q_h + [zeros_head] * (HEADS_PER_LANE_TILE - 1 - hh)
            qz_scr[h] = jnp.concatenate(padded, axis=0).astype(BF16)

        blk_row = lax.broadcasted_iota(jnp.int32, (nb, tq), 0)
        past = blk_row < i
        km = km_scr[...]
        km_hi = km.astype(BF16)
        r1 = km - km_hi.astype(F32)
        km_mid = r1.astype(BF16)
        km_lo = (r1 - km_mid.astype(F32)).astype(BF16)
        km3 = jnp.concatenate([km_hi, km_mid, km_lo], axis=0)
        for h in range(N_HEADS):
            hp = h // HEADS_PER_LANE_TILE
            s_scr[h] = scores(0, h)

            g3 = _dot(km3[:, hp * LANES:(hp + 1) * LANES], qz_scr[h])
            gate = g3[0:nb] + g3[nb:2 * nb] + g3[2 * nb:3 * nb]
            gate = jnp.where(past, gate, NEG_INF)
            rank = jnp.zeros((nb, tq), jnp.int32)
            for m in range(nb):
                gm = gate[m:m + 1, :]
                beats = (gm > gate) | ((gm == gate) & (m < blk_row))
                rank = rank + beats.astype(jnp.int32)
            sel = past & (rank < MOBA_TOPK)
            sel_scr[h] = sel.astype(F32)
            m_scr[h] = jnp.full((1, tq), NEG_INF, F32)
            acc_scr[h] = jnp.zeros((V_ROWS, tq), F32)

    def past_blocks(kb0, n_blocks):
        units = [(kb0 + j, h) for j in range(n_blocks) for h in range(N_HEADS)]
        ahead = {}
        for u in range(-SCORE_LEAD, len(units)):
            if 0 <= u + SCORE_LEAD < len(units):
                blk, head = units[u + SCORE_LEAD]
                ahead[u + SCORE_LEAD] = scores(blk + 1, head)
            if u < 0:
                continue
            kb, h = units[u]
            s = s_scr[h]
            keep = sel_scr[h, pl.ds(kb, 1), :] > 0.5
            m_run = m_scr[h]
            m_new = jnp.where(keep, jnp.maximum(m_run, jnp.max(s, axis=0, keepdims=True)), m_run)
            m_sub = jnp.where(keep, m_new, BIG)
            alpha = jnp.exp2(m_run - m_new)
            pb = jnp.exp2((s - m_sub).astype(BF16))
            m_scr[h] = m_new
            acc_scr[h] = alpha * acc_scr[h] + _dot(v_rows(kb, h), pb)
            s_scr[h] = ahead.pop(u)

    def four_blocks(t, carry):
        past_blocks(4 * t, 4)
        return carry

    n_past = jnp.where(live, i, 0)
    lax.fori_loop(0, n_past // 4, four_blocks, 0)
    n_done = 4 * (n_past // 4)

    @pl.when(n_past - n_done >= 2)
    def _():
        past_blocks(n_done, 2)

    @pl.when(lax.rem(n_past, 2) == 1)
    def _():
        past_blocks(n_past - 1, 1)

    def mix_tail(with_mlp):
        mlp = {}

        def mlp_begin():
            mlp['h'] = h1_scr[1 - slot]
            mlp['hn'] = _rms_rows(mlp['h'], g1_ref[...]).astype(BF16)
            mlp['acc'] = None

        def mlp_up(c):
            up = jnp.maximum(_dot(mlp['hn'], wup_ref[:, c * FF_CHUNK:(c + 1) * FF_CHUNK]), 0.0)
            mlp['up'] = (up * up).astype(BF16)

        def mlp_down(c):
            d = _dot(mlp['up'], wdn_ref[c * FF_CHUNK:(c + 1) * FF_CHUNK, :])
            mlp['acc'] = d if mlp['acc'] is None else mlp['acc'] + d

        def mlp_gate():
            mlp['h2'] = mlp['h'] + _rms_rows(mlp['acc'], g2_ref[...])
            mlp['z'] = _dot(mlp['h2'].astype(BF16), wgate_ref[...]) + bgate_ref[...]

        def mlp_ple():
            mlp['ple'] = _dot(p_ref[0].astype(BF16), wple_ref[...])

        def mlp_end():
            gate = 1.0 / (1.0 + jnp.exp(-mlp['z']))
            o_ref[0] = mlp['h2'] + _rms_rows(mlp['ple'] * gate, g3_ref[...])

        n_ff = D_FF // FF_CHUNK
        heads_per_piece = N_HEADS // (2 * n_ff)
        assert heads_per_piece * 2 * n_ff == N_HEADS
        pieces_before_head = [[mlp_begin]] + [[] for _ in range(N_HEADS - 1)]
        for c in range(n_ff):
            pieces_before_head[2 * c * heads_per_piece].append(functools.partial(mlp_up, c))
            pieces_before_head[(2 * c + 1) * heads_per_piece].append(functools.partial(mlp_down, c))

        kpos = lax.broadcasted_iota(jnp.int32, (MOBA_BLOCK, tq), 0)
        qpos = lax.broadcasted_iota(jnp.int32, (MOBA_BLOCK, tq), 1)
        causal = kpos <= qpos
        lane = lax.broadcasted_iota(jnp.int32, (GMLP_CHUNK, LANES), 1)
        n_lane_tiles = D_GMLP // LANES
        n_chunks = tq // GMLP_CHUNK
        heads_per_job = N_HEADS // n_lane_tiles
        assert heads_per_job * n_lane_tiles == N_HEADS
        heads_out = []
        cols = [[None] * n_lane_tiles for _ in range(n_chunks)]
        for h in range(N_HEADS):
            for piece in pieces_before_head[h] if with_mlp else ():
                piece()
            if h % heads_per_job == 0:
                gp = h // heads_per_job
                per_chunk = []
                for c in range(n_chunks):
                    vp = vn_scr[c * GMLP_CHUNK:(c + 1) * GMLP_CHUNK, gp * LANES:(gp + 1) * LANES]
                    per_chunk.append(jnp.concatenate(
                        [jnp.where(lane // GROUP_DIM == part, vp, jnp.zeros_like(vp))
                         for part in range(GROUPS_PER_LANE_TILE)], axis=0))
                both = _dot(wsm_ref[gp], jnp.concatenate(per_chunk, axis=1))
                for c in range(n_chunks):
                    cols[c][gp] = both[:, c * LANES:(c + 1) * LANES]

            st = jnp.where(causal, s_scr[h], NEG_INF)
            m_run = m_scr[h]
            m_new = jnp.maximum(m_run, jnp.max(st, axis=0, keepdims=True))
            alpha = jnp.exp2(m_run - m_new)
            p = jnp.exp2((st - m_new).astype(BF16))
            acc = alpha * acc_scr[h] + _dot(v_rows(i, h), p)
            heads_out.append(acc[0:HEAD_DIM] / acc[HEAD_DIM:HEAD_DIM + 1])

        if with_mlp:
            mlp_ple()
        mixed = jnp.concatenate(
            [jnp.concatenate(cols[c], axis=1) + bs_ref[...] for c in range(n_chunks)], axis=0)
        gm_out = ug_scr[...].astype(F32) * mixed
        gm_n = _rms_rows(gm_out, gg_ref[...]).astype(BF16)
        y = _dot(gm_n, wo_ref[D_ATTN:, :])
        if with_mlp:
            mlp_gate()

        attn_t = jnp.concatenate(heads_out, axis=0)
        ms_a = jnp.mean(attn_t * attn_t, axis=0, keepdims=True)
        attn_n = (attn_t * lax.rsqrt(ms_a + NORM_EPS) * ag_ref[...]).astype(BF16)
        y = y + _dot_tn(attn_n, wo_ref[:D_ATTN, :])
        h1_scr[slot] = x_ref[0] + _rms_rows(y, pg_ref[...])
        if with_mlp:
            mlp_end()

    @pl.when(step > 0)
    def _():
        mix_tail(True)

    @pl.when(step == 0)
    def _():
        mix_tail(False)


def _const_spec(shape):
    zeros = (0,) * len(shape)
    return pl.BlockSpec(shape, lambda *_: zeros, pipeline_mode=pl.Buffered(1))


def _rope_tables(seq, block):
    inv_freq = ROPE_THETA ** (-np.arange(ROPE_HALF, dtype=np.float64) / ROPE_HALF)
    ang = np.arange(seq, dtype=np.float64)[:, None] * inv_freq[None, :]
    cos, sin = np.cos(ang), np.sin(ang)
    n_blocks = seq // block
    cos_t, sin_t = (t.T.reshape(ROPE_HALF, n_blocks, block).transpose(1, 0, 2) for t in (cos, sin))
    ones = np.ones((seq, HEAD_DIM - ROPE_DIM))
    zeros = np.zeros((seq, HEAD_DIM - ROPE_DIM))
    zh = np.zeros((seq, ROPE_HALF))
    cos_h = np.concatenate([cos, cos, ones], axis=1)
    sa_h = np.concatenate([-sin, zh, zeros], axis=1)
    sb_h = np.concatenate([zh, sin, zeros], axis=1)
    tile = lambda t: np.concatenate([t] * HEADS_PER_LANE_TILE, axis=1).reshape(n_blocks, block, LANES)
    tables = (cos_t, sin_t, tile(cos_h), tile(sa_h), tile(sb_h))
    return tuple(jnp.asarray(np.ascontiguousarray(t), dtype=F32) for t in tables)


def _layer(h, p, mix_pre_g, w_in, gmlp_ln_g, gmlp_ln_b, w_s, b_s, attn_out_g,
           gmlp_out_g, w_o, mix_post_g, mlp_pre_g, w_up, w_down, mlp_post_g,
           w_ple, w_ple_gate, b_ple_gate, ple_post_g):
    B, S, D = h.shape
    assert D == D_MODEL and S % MOBA_BLOCK == 0
    nb = S // MOBA_BLOCK
    T = MOBA_BLOCK
    n_tiles = B * nb
    row = lambda v: v.reshape(1, -1).astype(F32)

    qscale = np.ones((1, D_IN_PROJ), np.float32)
    qscale[:, :D_ATTN] = HEAD_DIM ** -0.5 * math.log2(math.e)
    cos_t, sin_t, cos_k, sa_k, sb_k = _rope_tables(S, T)
    bs_exp = jnp.repeat(b_s.T, GROUP_DIM, axis=1).astype(F32)

    def cur(s):
        t = jnp.minimum(s, n_tiles - 1)
        return t // nb, t % nb

    def prev(s):
        t = jnp.maximum(s - 1, 0)
        return t // nb, t % nb

    hbm = pl.BlockSpec(memory_space=pl.ANY)
    out = pl.pallas_call(
        _layer_kernel,
        grid=(n_tiles + 1,),
        in_specs=[
            pl.BlockSpec((1, T, D), lambda s: (*cur(s), 0)),
            pl.BlockSpec((1, T, D_PLE), lambda s: (*prev(s), 0)),
            _const_spec((1, D)),
            hbm,
            _const_spec((1, D_IN_PROJ)),
            _const_spec((nb, ROPE_HALF, T)),
            _const_spec((nb, ROPE_HALF, T)),
            _const_spec((nb, T, LANES)),
            _const_spec((nb, T, LANES)),
            _const_spec((nb, T, LANES)),
            _const_spec((1, D_GMLP)),
            _const_spec((1, D_GMLP)),
            _const_spec((N_GROUPS, GMLP_CHUNK, GMLP_CHUNK)),
            _const_spec((GMLP_CHUNK, D_GMLP)),
            _const_spec((D_ATTN, 1)),
            _const_spec((1, D_GMLP)),
            hbm,
            _const_spec((1, D)),
            _const_spec((1, D)),
            hbm,
            hbm,
            _const_spec((1, D)),
            hbm,
            _const_spec((1, D)),
            hbm,
            _const_spec((1, D)),
        ],
        out_specs=pl.BlockSpec((1, T, D), lambda s: (*prev(s), 0)),
        out_shape=jax.ShapeDtypeStruct((B, S, D), F32),
        scratch_shapes=([pltpu.VMEM((nb, T, D_ATTN), BF16),
                         pltpu.VMEM((nb, N_HEADS * V_ROWS, T), BF16),
                         pltpu.VMEM((nb, D_ATTN), F32),
                         pltpu.VMEM((T, D_GMLP), BF16),
                         pltpu.VMEM((T, D_GMLP), BF16),
                         pltpu.VMEM((2, T, D), F32),
                         pltpu.VMEM((D, D_IN_PROJ), BF16),
                         pltpu.VMEM((D_ATTN + D_GMLP, D), BF16),
                         pltpu.VMEM((D, D_FF), BF16),
                         pltpu.VMEM((D_FF, D), BF16),
                         pltpu.VMEM((D, D), BF16),
                         pltpu.VMEM((D_PLE, D), BF16),
                         pltpu.VMEM((D_GMLP // LANES, GMLP_CHUNK, GROUPS_PER_LANE_TILE * GMLP_CHUNK), BF16),
                         pltpu.VMEM((2, STAGE_ROWS, D_FF), F32),
                         pltpu.SemaphoreType.DMA((2,))]
                        + [pltpu.VMEM((N_HEADS, nb, T), F32),
                           pltpu.VMEM((N_HEADS, LANES, T), BF16),
                           pltpu.VMEM((N_HEADS, 1, T), F32),
                           pltpu.VMEM((N_HEADS, V_ROWS, T), F32),
                           pltpu.VMEM((N_HEADS, MOBA_BLOCK, T), F32)]),
        compiler_params=pltpu.CompilerParams(dimension_semantics=("arbitrary",),
                                             vmem_limit_bytes=VMEM_LIMIT_BYTES),
        name="layer",
    )(h, p, row(mix_pre_g), w_in.astype(F32), jnp.asarray(qscale), cos_t, sin_t, cos_k, sa_k, sb_k,
      row(gmlp_ln_g), row(gmlp_ln_b), w_s.astype(F32), bs_exp,
      attn_out_g.reshape(-1, 1).astype(F32), row(gmlp_out_g), w_o.astype(F32), row(mix_post_g),
      row(mlp_pre_g), w_up.astype(F32), w_down.astype(F32), row(mlp_post_g),
      w_ple_gate.astype(F32), row(b_ple_gate), w_ple.astype(F32), row(ple_post_g))
    return out


def kernel(x, p, mix_pre_g, w_in, gmlp_ln_g, gmlp_ln_b, w_s, b_s, attn_out_g,
           gmlp_out_g, w_o, mix_post_g, mlp_pre_g, w_up, w_down, mlp_post_g,
           w_ple, w_ple_gate, b_ple_gate, ple_post_g):
    h = x
    for i in range(w_in.shape[0]):
        h = _layer(h, p[i], mix_pre_g[i], w_in[i], gmlp_ln_g[i], gmlp_ln_b[i],
                   w_s[i], b_s[i], attn_out_g[i], gmlp_out_g[i], w_o[i],
                   mix_post_g[i], mlp_pre_g[i], w_up[i], w_down[i], mlp_post_g[i],
                   w_ple[i], w_ple_gate[i], b_ple_gate[i], ple_post_g[i])
    return h
```

```python
import functools
import math

import jax
import jax.numpy as jnp
import numpy as np
from jax import lax
from jax.experimental import pallas as pl
from jax.experimental.pallas import tpu as pltpu

D_MODEL = 1024
D_PLE = 256
N_HEADS = 8
HEAD_DIM = 64
D_ATTN = N_HEADS * HEAD_DIM
N_GROUPS = 8
GROUP_DIM = 64
D_GMLP = N_GROUPS * GROUP_DIM
D_IN_PROJ = 3 * D_ATTN + 2 * D_GMLP
D_FF = 4 * D_MODEL
MOBA_BLOCK = 256
MOBA_TOPK = 3
GMLP_CHUNK = 128
ROPE_THETA = 500000.0
ROPE_DIM = HEAD_DIM // 4
ROPE_HALF = ROPE_DIM // 2
NORM_EPS = 1e-6
NEG_INF = -1e30

LANES = 128
HEADS_PER_LANE_TILE = LANES // HEAD_DIM
GROUPS_PER_LANE_TILE = LANES // GROUP_DIM
BF16_SUBLANES = 16
V_ROWS = HEAD_DIM + BF16_SUBLANES
BIG = 1e30
SCORE_LEAD = 2
VMEM_LIMIT_BYTES = 58 * 1024 * 1024
FF_CHUNK = 1024
FINISH_ROWS = 64
STAGE_ROWS = 128

F32 = jnp.float32
BF16 = jnp.bfloat16


def _dot(a, b):
    return jnp.dot(a, b, preferred_element_type=F32)


def _dot_tn(a, b):
    return lax.dot_general(a, b, (((0,), (0,)), ((), ())), preferred_element_type=F32)


def _rms_rows(x, g):
    ms = jnp.mean(x * x, axis=-1, keepdims=True)
    return x * lax.rsqrt(ms + NORM_EPS) * g


def _gelu_exact(x):
    return 0.5 * x * (1.0 + lax.erf(x * math.sqrt(0.5)))


def _load_weight_bf16(src_hbm, dst_ref, stage_ref, sem_ref, col_scale=None):
    n_rows, width = src_hbm.shape
    chunk = min(stage_ref.shape[1], n_rows)
    n_chunks = n_rows // chunk
    assert n_chunks * chunk == n_rows and width <= stage_ref.shape[2]

    def copy(c, slot):
        return pltpu.make_async_copy(src_hbm.at[pl.ds(c * chunk, chunk), :],
                                     stage_ref.at[slot, pl.ds(0, chunk), pl.ds(0, width)],
                                     sem_ref.at[slot])

    copy(0, 0).start()

    def body(c, carry):
        slot = lax.rem(c, 2)

        @pl.when(c + 1 < n_chunks)
        def _():
            copy(c + 1, 1 - slot).start()

        copy(c, slot).wait()
        w = stage_ref[slot, pl.ds(0, chunk), pl.ds(0, width)]
        if col_scale is not None:
            w = w * col_scale
        rows = pl.ds(pl.multiple_of(c * chunk, chunk), chunk)
        dst_ref[rows, :] = w.astype(BF16)
        return carry

    lax.fori_loop(0, n_chunks, body, 0)


def _layer_kernel(x_ref, xp_ref, p_ref, g_ref, win_hbm, qscale_ref, cost_ref, sint_ref,
                  cosk_ref, sak_ref, sbk_ref, lng_ref, lnb_ref,
                  ws_ref, bs_ref, ag_ref, gg_ref, wo_hbm, pg_ref,
                  g1_ref, wup_hbm, wdn_hbm, g2_ref, wgate_hbm, bgate_ref, wple_hbm, g3_ref,
                  o_ref,
                  k_scr, vt_scr, km_scr, ug_scr, vn_scr, h1_scr, y_scr,
                  win_ref, wo_ref, wup_ref, wdn_ref, wgate_ref, wple_ref, wsm_ref, stage, sem,
                  *scratch):
    sel_scr, qz_scr, m_scr, acc_scr, s_scr = scratch
    nb = k_scr.shape[0]
    tq = MOBA_BLOCK
    n_tiles = pl.num_programs(0) - 1
    step = pl.program_id(0)
    live = step < n_tiles
    i = lax.rem(jnp.minimum(step, n_tiles - 1), nb)

    @pl.when(step == 0)
    def _():
        km_scr[...] = jnp.zeros(km_scr.shape, F32)
        y_scr[...] = jnp.zeros(y_scr.shape, F32)
        _load_weight_bf16(win_hbm, win_ref, stage, sem, col_scale=qscale_ref[...])
        _load_weight_bf16(wo_hbm, wo_ref, stage, sem)
        _load_weight_bf16(wup_hbm, wup_ref, stage, sem)
        _load_weight_bf16(wdn_hbm, wdn_ref, stage, sem)
        _load_weight_bf16(wgate_hbm, wgate_ref, stage, sem)
        _load_weight_bf16(wple_hbm, wple_ref, stage, sem)
        t_idx = lax.broadcasted_iota(jnp.int32, (GMLP_CHUNK, GMLP_CHUNK), 0)
        s_idx = lax.broadcasted_iota(jnp.int32, (GMLP_CHUNK, GMLP_CHUNK), 1)
        for g in range(N_GROUPS):
            tile, part = divmod(g, GROUPS_PER_LANE_TILE)
            wsm_ref[tile, :, part * GMLP_CHUNK:(part + 1) * GMLP_CHUNK] = (
                jnp.where(s_idx <= t_idx, ws_ref[g], 0.0).astype(BF16))

    def scores(blk, h):
        hp = h // HEADS_PER_LANE_TILE
        return _dot(k_scr[blk, :, hp * LANES:(hp + 1) * LANES], qz_scr[h])

    def v_rows(blk, h):
        return vt_scr[blk, h * V_ROWS:(h + 1) * V_ROWS, :]

    def finish_prev(r):
        rows = slice(r * FINISH_ROWS, (r + 1) * FINISH_ROWS)
        h1_scr[rows, :] = xp_ref[0, rows, :] + _rms_rows(y_scr[rows, :], pg_ref[...])

    @pl.when(jnp.logical_not(live))
    def _():
        for r in range(tq // FINISH_ROWS):
            finish_prev(r)

    @pl.when(live)
    def _():
        x = x_ref[0]
        hn = _rms_rows(x, g_ref[...]).astype(BF16)

        gv = _dot(hn, win_ref[:, 3 * D_ATTN + D_GMLP:])
        finish_prev(0)
        g2 = _gelu_exact(gv)
        mu = jnp.mean(g2, axis=-1, keepdims=True)
        xc = g2 - mu
        var = jnp.mean(xc * xc, axis=-1, keepdims=True)
        vn = xc * lax.rsqrt(var + NORM_EPS) * lng_ref[...] + lnb_ref[...]
        vn_scr[...] = vn.astype(BF16)

        qt = _dot(hn, win_ref[:, :D_ATTN]).T
        finish_prev(1)

        tok = _dot(hn, win_ref[:, D_ATTN:2 * D_ATTN])
        finish_prev(2)
        cos_k = cosk_ref[i]
        sa_k = sak_ref[i]
        sb_k = sbk_ref[i]
        kparts = []
        for j in range(D_ATTN // LANES):
            kb = tok[:, j * LANES:(j + 1) * LANES]
            up = pltpu.roll(kb, LANES - ROPE_HALF, axis=1)
            dn = pltpu.roll(kb, ROPE_HALF, axis=1)
            kparts.append(kb * cos_k + up * sa_k + dn * sb_k)
        k = jnp.concatenate(kparts, axis=1)
        k_scr[i] = k.astype(BF16)
        km_scr[pl.ds(i, 1), :] = jnp.mean(k, axis=0, keepdims=True)

        u = _dot(hn, win_ref[:, 3 * D_ATTN:3 * D_ATTN + D_GMLP])
        finish_prev(3)
        ug_scr[...] = _gelu_exact(u).astype(BF16)

        vt = _dot(hn, win_ref[:, 2 * D_ATTN:3 * D_ATTN]).T
        ones = jnp.ones((V_ROWS - HEAD_DIM, tq), F32)
        vparts = []
        for h in range(N_HEADS):
            vparts.append(vt[h * HEAD_DIM:(h + 1) * HEAD_DIM])
            vparts.append(ones)
        vt_scr[i] = jnp.concatenate(vparts, axis=0).astype(BF16)

        cos_t = cost_ref[i]
        sin_t = sint_ref[i]
        zeros_head = jnp.zeros((HEAD_DIM, tq), F32)
        for h in range(N_HEADS):
            hh = h % HEADS_PER_LANE_TILE
            r0 = h * HEAD_DIM
            x1 = qt[r0:r0 + ROPE_HALF]
            x2 = qt[r0 + ROPE_HALF:r0 + ROPE_DIM]
            q_h = [x1 * cos_t - x2 * sin_t, x2 * cos_t + x1 * sin_t, qt[r0 + ROPE_DIM:r0 + HEAD_DIM]]
            padded = [zeros_head] * hh + q_h + [zeros_head] * (HEADS_PER_LANE_TILE - 1 - hh)
            qz_scr[h] = jnp.concatenate(padded, axis=0).astype(BF16)

        blk_row = lax.broadcasted_iota(jnp.int32, (nb, tq), 0)
        past = blk_row < i
        km = km_scr[...]
        km_hi = km.astype(BF16)
        r1 = km - km_hi.astype(F32)
        km_mid = r1.astype(BF16)
        km_lo = (r1 - km_mid.astype(F32)).astype(BF16)
        km3 = jnp.concatenate([km_hi, km_mid, km_lo], axis=0)
        for h in range(N_HEADS):
            hp = h // HEADS_PER_LANE_TILE
            s_scr[h] = scores(0, h)

            g3 = _dot(km3[:, hp * LANES:(hp + 1) * LANES], qz_scr[h])
            gate = g3[0:nb] + g3[nb:2 * nb] + g3[2 * nb:3 * nb]
            gate = jnp.where(past, gate, NEG_INF)
            rank = jnp.zeros((nb, tq), jnp.int32)
            for m in range(nb):
                gm = gate[m:m + 1, :]
                beats = (gm > gate) | ((gm == gate) & (m < blk_row))
                rank = rank + beats.astype(jnp.int32)
            sel = past & (rank < MOBA_TOPK)
            sel_scr[h] = sel.astype(F32)
            m_scr[h] = jnp.full((1, tq), NEG_INF, F32)
            acc_scr[h] = jnp.zeros((V_ROWS, tq), F32)

    def past_blocks(kb0, n_blocks):
        units = [(kb0 + j, h) for j in range(n_blocks) for h in range(N_HEADS)]
        ahead = {}
        for u in range(-SCORE_LEAD, len(units)):
            if 0 <= u + SCORE_LEAD < len(units):
                blk, head = units[u + SCORE_LEAD]
                ahead[u + SCORE_LEAD] = scores(blk + 1, head)
            if u < 0:
                continue
            kb, h = units[u]
            s = s_scr[h]
            keep = sel_scr[h, pl.ds(kb, 1), :] > 0.5
            m_run = m_scr[h]
            m_new = jnp.where(keep, jnp.maximum(m_run, jnp.max(s, axis=0, keepdims=True)), m_run)
            m_sub = jnp.where(keep, m_new, BIG)
            alpha = jnp.exp2(m_run - m_new)
            pb = jnp.exp2((s - m_sub).astype(BF16))
            m_scr[h] = m_new
            acc_scr[h] = alpha * acc_scr[h] + _dot(v_rows(kb, h), pb)
            s_scr[h] = ahead.pop(u)

    def four_blocks(t, carry):
        past_blocks(4 * t, 4)
        return carry

    n_past = jnp.where(live, i, 0)
    lax.fori_loop(0, n_past // 4, four_blocks, 0)
    n_done = 4 * (n_past // 4)

    @pl.when(n_past - n_done >= 2)
    def _():
        past_blocks(n_done, 2)

    @pl.when(lax.rem(n_past, 2) == 1)
    def _():
        past_blocks(n_past - 1, 1)

    def mix_tail(with_mlp):
        mlp = {}

        def mlp_begin():
            mlp['h'] = h1_scr[...]
            mlp['hn'] = _rms_rows(mlp['h'], g1_ref[...]).astype(BF16)
            mlp['acc'] = None

        def mlp_up(c):
            up = jnp.maximum(_dot(mlp['hn'], wup_ref[:, c * FF_CHUNK:(c + 1) * FF_CHUNK]), 0.0)
            mlp['up'] = (up * up).astype(BF16)

        def mlp_down(c):
            d = _dot(mlp['up'], wdn_ref[c * FF_CHUNK:(c + 1) * FF_CHUNK, :])
            mlp['acc'] = d if mlp['acc'] is None else mlp['acc'] + d

        def mlp_gate():
            mlp['h2'] = mlp['h'] + _rms_rows(mlp['acc'], g2_ref[...])
            mlp['z'] = _dot(mlp['h2'].astype(BF16), wgate_ref[...]) + bgate_ref[...]

        def mlp_ple():
            mlp['ple'] = _dot(p_ref[0].astype(BF16), wple_ref[...])

        def mlp_end():
            gate = 1.0 / (1.0 + jnp.exp(-mlp['z']))
            o_ref[0] = mlp['h2'] + _rms_rows(mlp['ple'] * gate, g3_ref[...])

        n_ff = D_FF // FF_CHUNK
        heads_per_piece = N_HEADS // (2 * n_ff)
        assert heads_per_piece * 2 * n_ff == N_HEADS
        pieces_before_head = [[mlp_begin]] + [[] for _ in range(N_HEADS - 1)]
        for c in range(n_ff):
            pieces_before_head[2 * c * heads_per_piece].append(functools.partial(mlp_up, c))
            pieces_before_head[(2 * c + 1) * heads_per_piece].append(functools.partial(mlp_down, c))

        kpos = lax.broadcasted_iota(jnp.int32, (MOBA_BLOCK, tq), 0)
        qpos = lax.broadcasted_iota(jnp.int32, (MOBA_BLOCK, tq), 1)
        causal = kpos <= qpos
        lane = lax.broadcasted_iota(jnp.int32, (GMLP_CHUNK, LANES), 1)
        n_lane_tiles = D_GMLP // LANES
        n_chunks = tq // GMLP_CHUNK
        assert n_lane_tiles * n_chunks == N_HEADS
        heads_out = []
        cols = [[None] * n_lane_tiles for _ in range(n_chunks)]
        for h in range(N_HEADS):
            for piece in pieces_before_head[h] if with_mlp else ():
                piece()
            c, gp = divmod(h, n_lane_tiles)
            vp = vn_scr[c * GMLP_CHUNK:(c + 1) * GMLP_CHUNK, gp * LANES:(gp + 1) * LANES]
            stacked = jnp.concatenate(
                [jnp.where(lane // GROUP_DIM == part, vp, jnp.zeros_like(vp))
                 for part in range(GROUPS_PER_LANE_TILE)], axis=0)
            cols[c][gp] = _dot(wsm_ref[gp], stacked)

            st = jnp.where(causal, s_scr[h], NEG_INF)
            m_run = m_scr[h]
            m_new = jnp.maximum(m_run, jnp.max(st, axis=0, keepdims=True))
            alpha = jnp.exp2(m_run - m_new)
            p = jnp.exp2((st - m_new).astype(BF16))
            acc = alpha * acc_scr[h] + _dot(v_rows(i, h), p)
            heads_out.append(acc[0:HEAD_DIM] / acc[HEAD_DIM:HEAD_DIM + 1])

        if with_mlp:
            mlp_ple()
        mixed = jnp.concatenate(
            [jnp.concatenate(cols[c], axis=1) + bs_ref[...] for c in range(n_chunks)], axis=0)
        gm_out = ug_scr[...].astype(F32) * mixed
        gm_n = _rms_rows(gm_out, gg_ref[...]).astype(BF16)
        y = _dot(gm_n, wo_ref[D_ATTN:, :])
        if with_mlp:
            mlp_gate()

        attn_t = jnp.concatenate(heads_out, axis=0)
        ms_a = jnp.mean(attn_t * attn_t, axis=0, keepdims=True)
        attn_n = (attn_t * lax.rsqrt(ms_a + NORM_EPS) * ag_ref[...]).astype(BF16)
        y = y + _dot_tn(attn_n, wo_ref[:D_ATTN, :])
        y_scr[...] = y
        if with_mlp:
            mlp_end()

    @pl.when(step > 0)
    def _():
        mix_tail(True)

    @pl.when(step == 0)
    def _():
        mix_tail(False)


def _const_spec(shape):
    zeros = (0,) * len(shape)
    return pl.BlockSpec(shape, lambda *_: zeros, pipeline_mode=pl.Buffered(1))


def _rope_tables(seq, block):
    inv_freq = ROPE_THETA ** (-np.arange(ROPE_HALF, dtype=np.float64) / ROPE_HALF)
    ang = np.arange(seq, dtype=np.float64)[:, None] * inv_freq[None, :]
    cos, sin = np.cos(ang), np.sin(ang)
    n_blocks = seq // block
    cos_t, sin_t = (t.T.reshape(ROPE_HALF, n_blocks, block).transpose(1, 0, 2) for t in (cos, sin))
    ones = np.ones((seq, HEAD_DIM - ROPE_DIM))
    zeros = np.zeros((seq, HEAD_DIM - ROPE_DIM))
    zh = np.zeros((seq, ROPE_HALF))
    cos_h = np.concatenate([cos, cos, ones], axis=1)
    sa_h = np.concatenate([-sin, zh, zeros], axis=1)
    sb_h = np.concatenate([zh, sin, zeros], axis=1)
    tile = lambda t: np.concatenate([t] * HEADS_PER_LANE_TILE, axis=1).reshape(n_blocks, block, LANES)
    tables = (cos_t, sin_t, tile(cos_h), tile(sa_h), tile(sb_h))
    return tuple(jnp.asarray(np.ascontiguousarray(t), dtype=F32) for t in tables)


def _layer(h, p, mix_pre_g, w_in, gmlp_ln_g, gmlp_ln_b, w_s, b_s, attn_out_g,
           gmlp_out_g, w_o, mix_post_g, mlp_pre_g, w_up, w_down, mlp_post_g,
           w_ple, w_ple_gate, b_ple_gate, ple_post_g):
    B, S, D = h.shape
    assert D == D_MODEL and S % MOBA_BLOCK == 0
    nb = S // MOBA_BLOCK
    T = MOBA_BLOCK
    n_tiles = B * nb
    row = lambda v: v.reshape(1, -1).astype(F32)

    qscale = np.ones((1, D_IN_PROJ), np.float32)
    qscale[:, :D_ATTN] = HEAD_DIM ** -0.5 * math.log2(math.e)
    cos_t, sin_t, cos_k, sa_k, sb_k = _rope_tables(S, T)
    bs_exp = jnp.repeat(b_s.T, GROUP_DIM, axis=1).astype(F32)

    def cur(s):
        t = jnp.minimum(s, n_tiles - 1)
        return t // nb, t % nb

    def prev(s):
        t = jnp.maximum(s - 1, 0)
        return t // nb, t % nb

    hbm = pl.BlockSpec(memory_space=pl.ANY)
    out = pl.pallas_call(
        _layer_kernel,
        grid=(n_tiles + 1,),
        in_specs=[
            pl.BlockSpec((1, T, D), lambda s: (*cur(s), 0)),
            pl.BlockSpec((1, T, D), lambda s: (*prev(s), 0)),
            pl.BlockSpec((1, T, D_PLE), lambda s: (*prev(s), 0)),
            _const_spec((1, D)),
            hbm,
            _const_spec((1, D_IN_PROJ)),
            _const_spec((nb, ROPE_HALF, T)),
            _const_spec((nb, ROPE_HALF, T)),
            _const_spec((nb, T, LANES)),
            _const_spec((nb, T, LANES)),
            _const_spec((nb, T, LANES)),
            _const_spec((1, D_GMLP)),
            _const_spec((1, D_GMLP)),
            _const_spec((N_GROUPS, GMLP_CHUNK, GMLP_CHUNK)),
            _const_spec((GMLP_CHUNK, D_GMLP)),
            _const_spec((D_ATTN, 1)),
            _const_spec((1, D_GMLP)),
            hbm,
            _const_spec((1, D)),
            _const_spec((1, D)),
            hbm,
            hbm,
            _const_spec((1, D)),
            hbm,
            _const_spec((1, D)),
            hbm,
            _const_spec((1, D)),
        ],
        out_specs=pl.BlockSpec((1, T, D), lambda s: (*prev(s), 0)),
        out_shape=jax.ShapeDtypeStruct((B, S, D), F32),
        scratch_shapes=([pltpu.VMEM((nb, T, D_ATTN), BF16),
                         pltpu.VMEM((nb, N_HEADS * V_ROWS, T), BF16),
                         pltpu.VMEM((nb, D_ATTN), F32),
                         pltpu.VMEM((T, D_GMLP), BF16),
                         pltpu.VMEM((T, D_GMLP), BF16),
                         pltpu.VMEM((T, D), F32),
                         pltpu.VMEM((T, D), F32),
                         pltpu.VMEM((D, D_IN_PROJ), BF16),
                         pltpu.VMEM((D_ATTN + D_GMLP, D), BF16),
                         pltpu.VMEM((D, D_FF), BF16),
                         pltpu.VMEM((D_FF, D), BF16),
                         pltpu.VMEM((D, D), BF16),
                         pltpu.VMEM((D_PLE, D), BF16),
                         pltpu.VMEM((D_GMLP // LANES, GMLP_CHUNK, GROUPS_PER_LANE_TILE * GMLP_CHUNK), BF16),
                         pltpu.VMEM((2, STAGE_ROWS, D_FF), F32),
                         pltpu.SemaphoreType.DMA((2,))]
                        + [pltpu.VMEM((N_HEADS, nb, T), F32),
                           pltpu.VMEM((N_HEADS, LANES, T), BF16),
                           pltpu.VMEM((N_HEADS, 1, T), F32),
                           pltpu.VMEM((N_HEADS, V_ROWS, T), F32),
                           pltpu.VMEM((N_HEADS, MOBA_BLOCK, T), F32)]),
        compiler_params=pltpu.CompilerParams(dimension_semantics=("arbitrary",),
                                             vmem_limit_bytes=VMEM_LIMIT_BYTES),
        name="layer",
    )(h, h, p, row(mix_pre_g), w_in.astype(F32), jnp.asarray(qscale), cos_t, sin_t, cos_k, sa_k, sb_k,
      row(gmlp_ln_g), row(gmlp_ln_b), w_s.astype(F32), bs_exp,
      attn_out_g.reshape(-1, 1).astype(F32), row(gmlp_out_g), w_o.astype(F32), row(mix_post_g),
      row(mlp_pre_g), w_up.astype(F32), w_down.astype(F32), row(mlp_post_g),
      w_ple_gate.astype(F32), row(b_ple_gate), w_ple.astype(F32), row(ple_post_g))
    return out


def kernel(x, p, mix_pre_g, w_in, gmlp_ln_g, gmlp_ln_b, w_s, b_s, attn_out_g,
           gmlp_out_g, w_o, mix_post_g, mlp_pre_g, w_up, w_down, mlp_post_g,
           w_ple, w_ple_gate, b_ple_gate, ple_post_g):
    h = x
    for i in range(w_in.shape[0]):
        h = _layer(h, p[i], mix_pre_g[i], w_in[i], gmlp_ln_g[i], gmlp_ln_b[i],
                   w_s[i], b_s[i], attn_out_g[i], gmlp_out_g[i], w_o[i],
                   mix_post_g[i], mlp_pre_g[i], w_up[i], w_down[i], mlp_post_g[i],
                   w_ple[i], w_ple_gate[i], b_ple_gate[i], ple_post_g[i])
    return h
```

```python
import functools
import math

import jax
import jax.numpy as jnp
import numpy as np
from jax import lax
from jax.experimental import pallas as pl
from jax.experimental.pallas import tpu as pltpu

D_MODEL = 1024
D_PLE = 256
N_HEADS = 8
HEAD_DIM = 64
D_ATTN = N_HEADS * HEAD_DIM
N_GROUPS = 8
GROUP_DIM = 64
D_GMLP = N_GROUPS * GROUP_DIM
D_IN_PROJ = 3 * D_ATTN + 2 * D_GMLP
D_FF = 4 * D_MODEL
MOBA_BLOCK = 256
MOBA_TOPK = 3
GMLP_CHUNK = 128
ROPE_THETA = 500000.0
ROPE_DIM = HEAD_DIM // 4
ROPE_HALF = ROPE_DIM // 2
NORM_EPS = 1e-6
NEG_INF = -1e30

LANES = 128
HEADS_PER_LANE_TILE = LANES // HEAD_DIM
GROUPS_PER_LANE_TILE = LANES // GROUP_DIM
BF16_SUBLANES = 16
V_ROWS = HEAD_DIM + BF16_SUBLANES
BIG = 1e30
SCORE_LEAD = 2
VMEM_LIMIT_BYTES = 58 * 1024 * 1024
FF_CHUNK = 1024
FINISH_ROWS = 64
STAGE_ROWS = 128

F32 = jnp.float32
BF16 = jnp.bfloat16


def _dot(a, b):
    return jnp.dot(a, b, preferred_element_type=F32)


def _dot_tn(a, b):
    return lax.dot_general(a, b, (((0,), (0,)), ((), ())), preferred_element_type=F32)


def _rms_rows(x, g):
    ms = jnp.mean(x * x, axis=-1, keepdims=True)
    return x * lax.rsqrt(ms + NORM_EPS) * g


def _gelu_exact(x):
    return 0.5 * x * (1.0 + lax.erf(x * math.sqrt(0.5)))


def _load_weight_bf16(src_hbm, dst_ref, slots, sem_ref, col_scale=None):
    n_rows, width = src_hbm.shape
    chunk = min(slots[0].shape[0], n_rows)
    n_chunks = n_rows // chunk
    assert n_chunks * chunk == n_rows and width <= slots[0].shape[1]
    assert n_chunks == 1 or n_chunks % 2 == 0

    def copy(c, k):
        return pltpu.make_async_copy(src_hbm.at[pl.ds(c * chunk, chunk), :],
                                     slots[k].at[pl.ds(0, chunk), pl.ds(0, width)],
                                     sem_ref.at[k])

    def convert(c, k):
        w = slots[k][pl.ds(0, chunk), pl.ds(0, width)]
        if col_scale is not None:
            w = w * col_scale
        dst_ref[pl.ds(pl.multiple_of(c * chunk, chunk), chunk), :] = w.astype(BF16)

    copy(0, 0).start()
    if n_chunks == 1:
        copy(0, 0).wait()
        convert(0, 0)
        return

    def body(t, carry):
        c = 2 * t
        copy(c + 1, 1).start()
        copy(c, 0).wait()
        convert(c, 0)

        @pl.when(c + 2 < n_chunks)
        def _():
            copy(c + 2, 0).start()

        copy(c + 1, 1).wait()
        convert(c + 1, 1)
        return carry

    lax.fori_loop(0, n_chunks // 2, body, 0)


def _layer_kernel(x_ref, xp_ref, p_ref, g_ref, win_hbm, qscale_ref, cost_ref, sint_ref,
                  cosk_ref, sak_ref, sbk_ref, lng_ref, lnb_ref,
                  ws_ref, bs_ref, ag_ref, gg_ref, wo_hbm, pg_ref,
                  g1_ref, wup_hbm, wdn_hbm, g2_ref, wgate_hbm, bgate_ref, wple_hbm, g3_ref,
                  o_ref,
                  k_scr, vt_scr, km_scr, ug_scr, vn_scr, h1_scr, y_scr,
                  win_ref, wo_ref, wup_ref, wdn_ref, wgate_ref, wple_ref, wsm_ref, stage, sem,
                  *scratch):
    sel_scr, qz_scr, m_scr, acc_scr, s_scr = scratch
    nb = k_scr.shape[0]
    tq = MOBA_BLOCK
    n_tiles = pl.num_programs(0) - 1
    step = pl.program_id(0)
    live = step < n_tiles
    i = lax.rem(jnp.minimum(step, n_tiles - 1), nb)

    @pl.when(step == 0)
    def _():
        km_scr[...] = jnp.zeros(km_scr.shape, F32)
        wide = (stage.at[0], stage.at[1])
        narrow = (h1_scr, y_scr)
        _load_weight_bf16(win_hbm, win_ref, wide, sem, col_scale=qscale_ref[...])
        _load_weight_bf16(wo_hbm, wo_ref, narrow, sem)
        _load_weight_bf16(wup_hbm, wup_ref, wide, sem)
        _load_weight_bf16(wdn_hbm, wdn_ref, narrow, sem)
        _load_weight_bf16(wgate_hbm, wgate_ref, narrow, sem)
        _load_weight_bf16(wple_hbm, wple_ref, narrow, sem)
        y_scr[...] = jnp.zeros(y_scr.shape, F32)
        t_idx = lax.broadcasted_iota(jnp.int32, (GMLP_CHUNK, GMLP_CHUNK), 0)
        s_idx = lax.broadcasted_iota(jnp.int32, (GMLP_CHUNK, GMLP_CHUNK), 1)
        for g in range(N_GROUPS):
            tile, part = divmod(g, GROUPS_PER_LANE_TILE)
            wsm_ref[tile, :, part * GMLP_CHUNK:(part + 1) * GMLP_CHUNK] = (
                jnp.where(s_idx <= t_idx, ws_ref[g], 0.0).astype(BF16))

    def scores(blk, h):
        hp = h // HEADS_PER_LANE_TILE
        return _dot(k_scr[blk, :, hp * LANES:(hp + 1) * LANES], qz_scr[h])

    def v_rows(blk, h):
        return vt_scr[blk, h * V_ROWS:(h + 1) * V_ROWS, :]

    def finish_prev(r):
        rows = slice(r * FINISH_ROWS, (r + 1) * FINISH_ROWS)
        h1_scr[rows, :] = xp_ref[0, rows, :] + _rms_rows(y_scr[rows, :], pg_ref[...])

    @pl.when(jnp.logical_not(live))
    def _():
        for r in range(tq // FINISH_ROWS):
            finish_prev(r)

    @pl.when(live)
    def _():
        x = x_ref[0]
        hn = _rms_rows(x, g_ref[...]).astype(BF16)

        gv = _dot(hn, win_ref[:, 3 * D_ATTN + D_GMLP:])
        finish_prev(0)
        g2 = _gelu_exact(gv)
        mu = jnp.mean(g2, axis=-1, keepdims=True)
        xc = g2 - mu
        var = jnp.mean(xc * xc, axis=-1, keepdims=True)
        vn = xc * lax.rsqrt(var + NORM_EPS) * lng_ref[...] + lnb_ref[...]
        vn_scr[...] = vn.astype(BF16)

        qt = _dot(hn, win_ref[:, :D_ATTN]).T
        finish_prev(1)

        tok = _dot(hn, win_ref[:, D_ATTN:2 * D_ATTN])
        finish_prev(2)
        cos_k = cosk_ref[i]
        sa_k = sak_ref[i]
        sb_k = sbk_ref[i]
        kparts = []
        for j in range(D_ATTN // LANES):
            kb = tok[:, j * LANES:(j + 1) * LANES]
            up = pltpu.roll(kb, LANES - ROPE_HALF, axis=1)
            dn = pltpu.roll(kb, ROPE_HALF, axis=1)
            kparts.append(kb * cos_k + up * sa_k + dn * sb_k)
        k = jnp.concatenate(kparts, axis=1)
        k_scr[i] = k.astype(BF16)
        km_scr[pl.ds(i, 1), :] = jnp.mean(k, axis=0, keepdims=True)

        u = _dot(hn, win_ref[:, 3 * D_ATTN:3 * D_ATTN + D_GMLP])
        finish_prev(3)
        ug_scr[...] = _gelu_exact(u).astype(BF16)

        vt = _dot(hn, win_ref[:, 2 * D_ATTN:3 * D_ATTN]).T
        ones = jnp.ones((V_ROWS - HEAD_DIM, tq), F32)
        vparts = []
        for h in range(N_HEADS):
            vparts.append(vt[h * HEAD_DIM:(h + 1) * HEAD_DIM])
            vparts.append(ones)
        vt_scr[i] = jnp.concatenate(vparts, axis=0).astype(BF16)

        cos_t = cost_ref[i]
        sin_t = sint_ref[i]
        zeros_head = jnp.zeros((HEAD_DIM, tq), F32)
        for h in range(N_HEADS):
            hh = h % HEADS_PER_LANE_TILE
            r0 = h * HEAD_DIM
            x1 = qt[r0:r0 + ROPE_HALF]
            x2 = qt[r0 + ROPE_HALF:r0 + ROPE_DIM]
            q_h = [x1 * cos_t - x2 * sin_t, x2 * cos_t + x1 * sin_t, qt[r0 + ROPE_DIM:r0 + HEAD_DIM]]
            padded = [zeros_head] * hh + q_h + [zeros_head] * (HEADS_PER_LANE_TILE - 1 - hh)
            qz_scr[h] = jnp.concatenate(padded, axis=0).astype(BF16)

        blk_row = lax.broadcasted_iota(jnp.int32, (nb, tq), 0)
        past = blk_row < i
        km = km_scr[...]
        km_hi = km.astype(BF16)
        r1 = km - km_hi.astype(F32)
        km_mid = r1.astype(BF16)
        km_lo = (r1 - km_mid.astype(F32)).astype(BF16)
        km3 = jnp.concatenate([km_hi, km_mid, km_lo], axis=0)
        for h in range(N_HEADS):
            hp = h // HEADS_PER_LANE_TILE
            s_scr[h] = scores(0, h)

            g3 = _dot(km3[:, hp * LANES:(hp + 1) * LANES], qz_scr[h])
            gate = g3[0:nb] + g3[nb:2 * nb] + g3[2 * nb:3 * nb]
            gate = jnp.where(past, gate, NEG_INF)
            rank = jnp.zeros((nb, tq), jnp.int32)
            for m in range(nb):
                gm = gate[m:m + 1, :]
                beats = (gm > gate) | ((gm == gate) & (m < blk_row))
                rank = rank + beats.astype(jnp.int32)
            sel = past & (rank < MOBA_TOPK)
            sel_scr[h] = sel.astype(F32)
            m_scr[h] = jnp.full((1, tq), NEG_INF, F32)
            acc_scr[h] = jnp.zeros((V_ROWS, tq), F32)

    def past_blocks(kb0, n_blocks):
        units = [(kb0 + j, h) for j in range(n_blocks) for h in range(N_HEADS)]
        ahead = {}
        for u in range(-SCORE_LEAD, len(units)):
            if 0 <= u + SCORE_LEAD < len(units):
                blk, head = units[u + SCORE_LEAD]
                ahead[u + SCORE_LEAD] = scores(blk + 1, head)
            if u < 0:
                continue
            kb, h = units[u]
            s = s_scr[h]
            keep = sel_scr[h, pl.ds(kb, 1), :] > 0.5
            m_run = m_scr[h]
            m_new = jnp.where(keep, jnp.maximum(m_run, jnp.max(s, axis=0, keepdims=True)), m_run)
            m_sub = jnp.where(keep, m_new, BIG)
            alpha = jnp.exp2(m_run - m_new)
            pb = jnp.exp2((s - m_sub).astype(BF16))
            m_scr[h] = m_new
            acc_scr[h] = alpha * acc_scr[h] + _dot(v_rows(kb, h), pb)
            s_scr[h] = ahead.pop(u)

    def four_blocks(t, carry):
        past_blocks(4 * t, 4)
        return carry

    n_past = jnp.where(live, i, 0)
    lax.fori_loop(0, n_past // 4, four_blocks, 0)
    n_done = 4 * (n_past // 4)

    @pl.when(n_past - n_done >= 2)
    def _():
        past_blocks(n_done, 2)

    @pl.when(lax.rem(n_past, 2) == 1)
    def _():
        past_blocks(n_past - 1, 1)

    def mix_tail(with_mlp):
        mlp = {}

        def mlp_begin():
            mlp['h'] = h1_scr[...]
            mlp['hn'] = _rms_rows(mlp['h'], g1_ref[...]).astype(BF16)
            mlp['acc'] = None

        def mlp_up(c):
            up = jnp.maximum(_dot(mlp['hn'], wup_ref[:, c * FF_CHUNK:(c + 1) * FF_CHUNK]), 0.0)
            mlp['up'] = (up * up).astype(BF16)

        def mlp_down(c):
            d = _dot(mlp['up'], wdn_ref[c * FF_CHUNK:(c + 1) * FF_CHUNK, :])
            mlp['acc'] = d if mlp['acc'] is None else mlp['acc'] + d

        def mlp_gate():
            mlp['h2'] = mlp['h'] + _rms_rows(mlp['acc'], g2_ref[...])
            mlp['z'] = _dot(mlp['h2'].astype(BF16), wgate_ref[...]) + bgate_ref[...]

        def mlp_ple():
            mlp['ple'] = _dot(p_ref[0].astype(BF16), wple_ref[...])

        def mlp_end():
            gate = 1.0 / (1.0 + jnp.exp(-mlp['z']))
            o_ref[0] = mlp['h2'] + _rms_rows(mlp['ple'] * gate, g3_ref[...])

        n_ff = D_FF // FF_CHUNK
        heads_per_piece = N_HEADS // (2 * n_ff)
        assert heads_per_piece * 2 * n_ff == N_HEADS
        pieces_before_head = [[mlp_begin]] + [[] for _ in range(N_HEADS - 1)]
        for c in range(n_ff):
            pieces_before_head[2 * c * heads_per_piece].append(functools.partial(mlp_up, c))
            pieces_before_head[(2 * c + 1) * heads_per_piece].append(functools.partial(mlp_down, c))

        kpos = lax.broadcasted_iota(jnp.int32, (MOBA_BLOCK, tq), 0)
        qpos = lax.broadcasted_iota(jnp.int32, (MOBA_BLOCK, tq), 1)
        causal = kpos <= qpos
        lane = lax.broadcasted_iota(jnp.int32, (GMLP_CHUNK, LANES), 1)
        n_lane_tiles = D_GMLP // LANES
        n_chunks = tq // GMLP_CHUNK
        assert n_lane_tiles * n_chunks == N_HEADS
        heads_out = []
        cols = [[None] * n_lane_tiles for _ in range(n_chunks)]
        for h in range(N_HEADS):
            for piece in pieces_before_head[h] if with_mlp else ():
                piece()
            c, gp = divmod(h, n_lane_tiles)
            vp = vn_scr[c * GMLP_CHUNK:(c + 1) * GMLP_CHUNK, gp * LANES:(gp + 1) * LANES]
            stacked = jnp.concatenate(
                [jnp.where(lane // GROUP_DIM == part, vp, jnp.zeros_like(vp))
                 for part in range(GROUPS_PER_LANE_TILE)], axis=0)
            cols[c][gp] = _dot(wsm_ref[gp], stacked)

            st = jnp.where(causal, s_scr[h], NEG_INF)
            m_run = m_scr[h]
            m_new = jnp.maximum(m_run, jnp.max(st, axis=0, keepdims=True))
            alpha = jnp.exp2(m_run - m_new)
            p = jnp.exp2((st - m_new).astype(BF16))
            acc = alpha * acc_scr[h] + _dot(v_rows(i, h), p)
            heads_out.append(acc[0:HEAD_DIM] / acc[HEAD_DIM:HEAD_DIM + 1])

        if with_mlp:
            mlp_ple()
        mixed = jnp.concatenate(
            [jnp.concatenate(cols[c], axis=1) + bs_ref[...] for c in range(n_chunks)], axis=0)
        gm_out = ug_scr[...].astype(F32) * mixed
        gm_n = _rms_rows(gm_out, gg_ref[...]).astype(BF16)
        y = _dot(gm_n, wo_ref[D_ATTN:, :])
        if with_mlp:
            mlp_gate()

        attn_t = jnp.concatenate(heads_out, axis=0)
        ms_a = jnp.mean(attn_t * attn_t, axis=0, keepdims=True)
        attn_n = (attn_t * lax.rsqrt(ms_a + NORM_EPS) * ag_ref[...]).astype(BF16)
        y = y + _dot_tn(attn_n, wo_ref[:D_ATTN, :])
        y_scr[...] = y
        if with_mlp:
            mlp_end()

    @pl.when(step > 0)
    def _():
        mix_tail(True)

    @pl.when(step == 0)
    def _():
        mix_tail(False)


def _const_spec(shape):
    zeros = (0,) * len(shape)
    return pl.BlockSpec(shape, lambda *_: zeros, pipeline_mode=pl.Buffered(1))


def _rope_tables(seq, block):
    inv_freq = ROPE_THETA ** (-np.arange(ROPE_HALF, dtype=np.float64) / ROPE_HALF)
    ang = np.arange(seq, dtype=np.float64)[:, None] * inv_freq[None, :]
    cos, sin = np.cos(ang), np.sin(ang)
    n_blocks = seq // block
    cos_t, sin_t = (t.T.reshape(ROPE_HALF, n_blocks, block).transpose(1, 0, 2) for t in (cos, sin))
    ones = np.ones((seq, HEAD_DIM - ROPE_DIM))
    zeros = np.zeros((seq, HEAD_DIM - ROPE_DIM))
    zh = np.zeros((seq, ROPE_HALF))
    cos_h = np.concatenate([cos, cos, ones], axis=1)
    sa_h = np.concatenate([-sin, zh, zeros], axis=1)
    sb_h = np.concatenate([zh, sin, zeros], axis=1)
    tile = lambda t: np.concatenate([t] * HEADS_PER_LANE_TILE, axis=1).reshape(n_blocks, block, LANES)
    tables = (cos_t, sin_t, tile(cos_h), tile(sa_h), tile(sb_h))
    return tuple(jnp.asarray(np.ascontiguousarray(t), dtype=F32) for t in tables)


def _layer(h, p, mix_pre_g, w_in, gmlp_ln_g, gmlp_ln_b, w_s, b_s, attn_out_g,
           gmlp_out_g, w_o, mix_post_g, mlp_pre_g, w_up, w_down, mlp_post_g,
           w_ple, w_ple_gate, b_ple_gate, ple_post_g):
    B, S, D = h.shape
    assert D == D_MODEL and S % MOBA_BLOCK == 0
    nb = S // MOBA_BLOCK
    T = MOBA_BLOCK
    n_tiles = B * nb
    row = lambda v: v.reshape(1, -1).astype(F32)

    qscale = np.ones((1, D_IN_PROJ), np.float32)
    qscale[:, :D_ATTN] = HEAD_DIM ** -0.5 * math.log2(math.e)
    cos_t, sin_t, cos_k, sa_k, sb_k = _rope_tables(S, T)
    bs_exp = jnp.repeat(b_s.T, GROUP_DIM, axis=1).astype(F32)

    def cur(s):
        t = jnp.minimum(s, n_tiles - 1)
        return t // nb, t % nb

    def prev(s):
        t = jnp.maximum(s - 1, 0)
        return t // nb, t % nb

    hbm = pl.BlockSpec(memory_space=pl.ANY)
    out = pl.pallas_call(
        _layer_kernel,
        grid=(n_tiles + 1,),
        in_specs=[
            pl.BlockSpec((1, T, D), lambda s: (*cur(s), 0)),
            pl.BlockSpec((1, T, D), lambda s: (*prev(s), 0)),
            pl.BlockSpec((1, T, D_PLE), lambda s: (*prev(s), 0)),
            _const_spec((1, D)),
            hbm,
            _const_spec((1, D_IN_PROJ)),
            _const_spec((nb, ROPE_HALF, T)),
            _const_spec((nb, ROPE_HALF, T)),
            _const_spec((nb, T, LANES)),
            _const_spec((nb, T, LANES)),
            _const_spec((nb, T, LANES)),
            _const_spec((1, D_GMLP)),
            _const_spec((1, D_GMLP)),
            _const_spec((N_GROUPS, GMLP_CHUNK, GMLP_CHUNK)),
            _const_spec((GMLP_CHUNK, D_GMLP)),
            _const_spec((D_ATTN, 1)),
            _const_spec((1, D_GMLP)),
            hbm,
            _const_spec((1, D)),
            _const_spec((1, D)),
            hbm,
            hbm,
            _const_spec((1, D)),
            hbm,
            _const_spec((1, D)),
            hbm,
            _const_spec((1, D)),
        ],
        out_specs=pl.BlockSpec((1, T, D), lambda s: (*prev(s), 0)),
        out_shape=jax.ShapeDtypeStruct((B, S, D), F32),
        scratch_shapes=([pltpu.VMEM((nb, T, D_ATTN), BF16),
                         pltpu.VMEM((nb, N_HEADS * V_ROWS, T), BF16),
                         pltpu.VMEM((nb, D_ATTN), F32),
                         pltpu.VMEM((T, D_GMLP), BF16),
                         pltpu.VMEM((T, D_GMLP), BF16),
                         pltpu.VMEM((T, D), F32),
                         pltpu.VMEM((T, D), F32),
                         pltpu.VMEM((D, D_IN_PROJ), BF16),
                         pltpu.VMEM((D_ATTN + D_GMLP, D), BF16),
                         pltpu.VMEM((D, D_FF), BF16),
                         pltpu.VMEM((D_FF, D), BF16),
                         pltpu.VMEM((D, D), BF16),
                         pltpu.VMEM((D_PLE, D), BF16),
                         pltpu.VMEM((D_GMLP // LANES, GMLP_CHUNK, GROUPS_PER_LANE_TILE * GMLP_CHUNK), BF16),
                         pltpu.VMEM((2, STAGE_ROWS, D_FF), F32),
                         pltpu.SemaphoreType.DMA((2,))]
                        + [pltpu.VMEM((N_HEADS, nb, T), F32),
                           pltpu.VMEM((N_HEADS, LANES, T), BF16),
                           pltpu.VMEM((N_HEADS, 1, T), F32),
                           pltpu.VMEM((N_HEADS, V_ROWS, T), F32),
                           pltpu.VMEM((N_HEADS, MOBA_BLOCK, T), F32)]),
        compiler_params=pltpu.CompilerParams(dimension_semantics=("arbitrary",),
                                             vmem_limit_bytes=VMEM_LIMIT_BYTES),
        name="layer",
    )(h, h, p, row(mix_pre_g), w_in.astype(F32), jnp.asarray(qscale), cos_t, sin_t, cos_k, sa_k, sb_k,
      row(gmlp_ln_g), row(gmlp_ln_b), w_s.astype(F32), bs_exp,
      attn_out_g.reshape(-1, 1).astype(F32), row(gmlp_out_g), w_o.astype(F32), row(mix_post_g),
      row(mlp_pre_g), w_up.astype(F32), w_down.astype(F32), row(mlp_post_g),
      w_ple_gate.astype(F32), row(b_ple_gate), w_ple.astype(F32), row(ple_post_g))
    return out


def kernel(x, p, mix_pre_g, w_in, gmlp_ln_g, gmlp_ln_b, w_s, b_s, attn_out_g,
           gmlp_out_g, w_o, mix_post_g, mlp_pre_g, w_up, w_down, mlp_post_g,
           w_ple, w_ple_gate, b_ple_gate, ple_post_g):
    h = x
    for i in range(w_in.shape[0]):
        h = _layer(h, p[i], mix_pre_g[i], w_in[i], gmlp_ln_g[i], gmlp_ln_b[i],
                   w_s[i], b_s[i], attn_out_g[i], gmlp_out_g[i], w_o[i],
                   mix_post_g[i], mlp_pre_g[i], w_up[i], w_down[i], mlp_post_g[i],
                   w_ple[i], w_ple_gate[i], b_ple_gate[i], ple_post_g[i])
    return h
```

```python
import functools
import math

import jax
import jax.numpy as jnp
import numpy as np
from jax import lax
from jax.experimental import pallas as pl
from jax.experimental.pallas import tpu as pltpu

D_MODEL = 1024
D_PLE = 256
N_HEADS = 8
HEAD_DIM = 64
D_ATTN = N_HEADS * HEAD_DIM
N_GROUPS = 8
GROUP_DIM = 64
D_GMLP = N_GROUPS * GROUP_DIM
D_IN_PROJ = 3 * D_ATTN + 2 * D_GMLP
D_FF = 4 * D_MODEL
MOBA_BLOCK = 256
MOBA_TOPK = 3
GMLP_CHUNK = 128
ROPE_THETA = 500000.0
ROPE_DIM = HEAD_DIM // 4
ROPE_HALF = ROPE_DIM // 2
NORM_EPS = 1e-6
NEG_INF = -1e30

LANES = 128
HEADS_PER_LANE_TILE = LANES // HEAD_DIM
GROUPS_PER_LANE_TILE = LANES // GROUP_DIM
BF16_SUBLANES = 16
V_ROWS = HEAD_DIM + BF16_SUBLANES
BIG = 1e30
SCORE_LEAD = 2
VMEM_LIMIT_BYTES = 58 * 1024 * 1024
FF_CHUNK = 1024
FINISH_ROWS = 64
COPY_SPLIT = 4
STAGE_ROWS = 128

F32 = jnp.float32
BF16 = jnp.bfloat16


def _dot(a, b):
    return jnp.dot(a, b, preferred_element_type=F32)


def _dot_tn(a, b):
    return lax.dot_general(a, b, (((0,), (0,)), ((), ())), preferred_element_type=F32)


def _rms_rows(x, g):
    ms = jnp.mean(x * x, axis=-1, keepdims=True)
    return x * lax.rsqrt(ms + NORM_EPS) * g


def _gelu_exact(x):
    return 0.5 * x * (1.0 + lax.erf(x * math.sqrt(0.5)))


def _load_weight_bf16(src_hbm, dst_ref, slots, sem_ref, col_scale=None):
    n_rows, width = src_hbm.shape
    chunk = min(slots[0].shape[0], n_rows)
    n_chunks = n_rows // chunk
    assert n_chunks * chunk == n_rows and width <= slots[0].shape[1]
    assert n_chunks == 1 or n_chunks % 2 == 0

    part = chunk // COPY_SPLIT
    assert part * COPY_SPLIT == chunk

    def copies(c, k):
        return [pltpu.make_async_copy(src_hbm.at[pl.ds(c * chunk + j * part, part), :],
                                      slots[k].at[pl.ds(j * part, part), pl.ds(0, width)],
                                      sem_ref.at[k, j]) for j in range(COPY_SPLIT)]

    def start(c, k):
        for d in copies(c, k):
            d.start()

    def wait(c, k):
        for d in copies(c, k):
            d.wait()

    def convert(c, k):
        w = slots[k][pl.ds(0, chunk), pl.ds(0, width)]
        if col_scale is not None:
            w = w * col_scale
        dst_ref[pl.ds(pl.multiple_of(c * chunk, chunk), chunk), :] = w.astype(BF16)

    start(0, 0)
    if n_chunks == 1:
        wait(0, 0)
        convert(0, 0)
        return

    def body(t, carry):
        c = 2 * t
        start(c + 1, 1)
        wait(c, 0)
        convert(c, 0)

        @pl.when(c + 2 < n_chunks)
        def _():
            start(c + 2, 0)

        wait(c + 1, 1)
        convert(c + 1, 1)
        return carry

    lax.fori_loop(0, n_chunks // 2, body, 0)


def _layer_kernel(x_ref, xp_ref, p_ref, g_ref, win_hbm, qscale_ref, cost_ref, sint_ref,
                  cosk_ref, sak_ref, sbk_ref, lng_ref, lnb_ref,
                  ws_ref, bs_ref, ag_ref, gg_ref, wo_hbm, pg_ref,
                  g1_ref, wup_hbm, wdn_hbm, g2_ref, wgate_hbm, bgate_ref, wple_hbm, g3_ref,
                  o_ref,
                  k_scr, vt_scr, km_scr, ug_scr, vn_scr, h1_scr, y_scr,
                  win_ref, wo_ref, wup_ref, wdn_ref, wgate_ref, wple_ref, wsm_ref, stage, sem,
                  *scratch):
    sel_scr, qz_scr, m_scr, acc_scr, s_scr = scratch
    nb = k_scr.shape[0]
    tq = MOBA_BLOCK
    n_tiles = pl.num_programs(0) - 1
    step = pl.program_id(0)
    live = step < n_tiles
    i = lax.rem(jnp.minimum(step, n_tiles - 1), nb)

    @pl.when(step == 0)
    def _():
        km_scr[...] = jnp.zeros(km_scr.shape, F32)
        wide = (stage.at[0], stage.at[1])
        narrow = (h1_scr, y_scr)
        _load_weight_bf16(win_hbm, win_ref, wide, sem, col_scale=qscale_ref[...])
        _load_weight_bf16(wo_hbm, wo_ref, narrow, sem)
        _load_weight_bf16(wup_hbm, wup_ref, wide, sem)
        _load_weight_bf16(wdn_hbm, wdn_ref, narrow, sem)
        _load_weight_bf16(wgate_hbm, wgate_ref, narrow, sem)
        _load_weight_bf16(wple_hbm, wple_ref, narrow, sem)
        y_scr[...] = jnp.zeros(y_scr.shape, F32)
        t_idx = lax.broadcasted_iota(jnp.int32, (GMLP_CHUNK, GMLP_CHUNK), 0)
        s_idx = lax.broadcasted_iota(jnp.int32, (GMLP_CHUNK, GMLP_CHUNK), 1)
        for g in range(N_GROUPS):
            tile, part = divmod(g, GROUPS_PER_LANE_TILE)
            wsm_ref[tile, :, part * GMLP_CHUNK:(part + 1) * GMLP_CHUNK] = (
                jnp.where(s_idx <= t_idx, ws_ref[g], 0.0).astype(BF16))

    def scores(blk, h):
        hp = h // HEADS_PER_LANE_TILE
        return _dot(k_scr[blk, :, hp * LANES:(hp + 1) * LANES], qz_scr[h])

    def v_rows(blk, h):
        return vt_scr[blk, h * V_ROWS:(h + 1) * V_ROWS, :]

    def finish_prev(r):
        rows = slice(r * FINISH_ROWS, (r + 1) * FINISH_ROWS)
        h1_scr[rows, :] = xp_ref[0, rows, :] + _rms_rows(y_scr[rows, :], pg_ref[...])

    @pl.when(jnp.logical_not(live))
    def _():
        for r in range(tq // FINISH_ROWS):
            finish_prev(r)

    @pl.when(live)
    def _():
        x = x_ref[0]
        hn = _rms_rows(x, g_ref[...]).astype(BF16)

        gv = _dot(hn, win_ref[:, 3 * D_ATTN + D_GMLP:])
        finish_prev(0)
        g2 = _gelu_exact(gv)
        mu = jnp.mean(g2, axis=-1, keepdims=True)
        xc = g2 - mu
        var = jnp.mean(xc * xc, axis=-1, keepdims=True)
        vn = xc * lax.rsqrt(var + NORM_EPS) * lng_ref[...] + lnb_ref[...]
        vn_scr[...] = vn.astype(BF16)

        qt = _dot(hn, win_ref[:, :D_ATTN]).T
        finish_prev(1)

        tok = _dot(hn, win_ref[:, D_ATTN:2 * D_ATTN])
        finish_prev(2)
        cos_k = cosk_ref[i]
        sa_k = sak_ref[i]
        sb_k = sbk_ref[i]
        kparts = []
        for j in range(D_ATTN // LANES):
            kb = tok[:, j * LANES:(j + 1) * LANES]
            up = pltpu.roll(kb, LANES - ROPE_HALF, axis=1)
            dn = pltpu.roll(kb, ROPE_HALF, axis=1)
            kparts.append(kb * cos_k + up * sa_k + dn * sb_k)
        k = jnp.concatenate(kparts, axis=1)
        k_scr[i] = k.astype(BF16)
        km_scr[pl.ds(i, 1), :] = jnp.mean(k, axis=0, keepdims=True)

        u = _dot(hn, win_ref[:, 3 * D_ATTN:3 * D_ATTN + D_GMLP])
        finish_prev(3)
        ug_scr[...] = _gelu_exact(u).astype(BF16)

        vt = _dot(hn, win_ref[:, 2 * D_ATTN:3 * D_ATTN]).T
        ones = jnp.ones((V_ROWS - HEAD_DIM, tq), F32)
        vparts = []
        for h in range(N_HEADS):
            vparts.append(vt[h * HEAD_DIM:(h + 1) * HEAD_DIM])
            vparts.append(ones)
        vt_scr[i] = jnp.concatenate(vparts, axis=0).astype(BF16)

        cos_t = cost_ref[i]
        sin_t = sint_ref[i]
        zeros_head = jnp.zeros((HEAD_DIM, tq), F32)
        for h in range(N_HEADS):
            hh = h % HEADS_PER_LANE_TILE
            r0 = h * HEAD_DIM
            x1 = qt[r0:r0 + ROPE_HALF]
            x2 = qt[r0 + ROPE_HALF:r0 + ROPE_DIM]
            q_h = [x1 * cos_t - x2 * sin_t, x2 * cos_t + x1 * sin_t, qt[r0 + ROPE_DIM:r0 + HEAD_DIM]]
            padded = [zeros_head] * hh + q_h + [zeros_head] * (HEADS_PER_LANE_TILE - 1 - hh)
            qz_scr[h] = jnp.concatenate(padded, axis=0).astype(BF16)

        blk_row = lax.broadcasted_iota(jnp.int32, (nb, tq), 0)
        past = blk_row < i
        km = km_scr[...]
        km_hi = km.astype(BF16)
        r1 = km - km_hi.astype(F32)
        km_mid = r1.astype(BF16)
        km_lo = (r1 - km_mid.astype(F32)).astype(BF16)
        km3 = jnp.concatenate([km_hi, km_mid, km_lo], axis=0)
        for h in range(N_HEADS):
            hp = h // HEADS_PER_LANE_TILE
            s_scr[h] = scores(0, h)

            g3 = _dot(km3[:, hp * LANES:(hp + 1) * LANES], qz_scr[h])
            gate = g3[0:nb] + g3[nb:2 * nb] + g3[2 * nb:3 * nb]
            gate = jnp.where(past, gate, NEG_INF)
            rank = jnp.zeros((nb, tq), jnp.int32)
            for m in range(nb):
                gm = gate[m:m + 1, :]
                beats = (gm > gate) | ((gm == gate) & (m < blk_row))
                rank = rank + beats.astype(jnp.int32)
            sel = past & (rank < MOBA_TOPK)
            sel_scr[h] = sel.astype(F32)
            m_scr[h] = jnp.full((1, tq), NEG_INF, F32)
            acc_scr[h] = jnp.zeros((V_ROWS, tq), F32)

    def past_blocks(kb0, n_blocks):
        units = [(kb0 + j, h) for j in range(n_blocks) for h in range(N_HEADS)]
        ahead = {}
        for u in range(-SCORE_LEAD, len(units)):
            if 0 <= u + SCORE_LEAD < len(units):
                blk, head = units[u + SCORE_LEAD]
                ahead[u + SCORE_LEAD] = scores(blk + 1, head)
            if u < 0:
                continue
            kb, h = units[u]
            s = s_scr[h]
            keep = sel_scr[h, pl.ds(kb, 1), :] > 0.5
            m_run = m_scr[h]
            m_new = jnp.where(keep, jnp.maximum(m_run, jnp.max(s, axis=0, keepdims=True)), m_run)
            m_sub = jnp.where(keep, m_new, BIG)
            alpha = jnp.exp2(m_run - m_new)
            pb = jnp.exp2((s - m_sub).astype(BF16))
            m_scr[h] = m_new
            acc_scr[h] = alpha * acc_scr[h] + _dot(v_rows(kb, h), pb)
            s_scr[h] = ahead.pop(u)

    def four_blocks(t, carry):
        past_blocks(4 * t, 4)
        return carry

    n_past = jnp.where(live, i, 0)
    lax.fori_loop(0, n_past // 4, four_blocks, 0)
    n_done = 4 * (n_past // 4)

    @pl.when(n_past - n_done >= 2)
    def _():
        past_blocks(n_done, 2)

    @pl.when(lax.rem(n_past, 2) == 1)
    def _():
        past_blocks(n_past - 1, 1)

    def mix_tail(with_mlp):
        mlp = {}

        def mlp_begin():
            mlp['h'] = h1_scr[...]
            mlp['hn'] = _rms_rows(mlp['h'], g1_ref[...]).astype(BF16)
            mlp['acc'] = None

        def mlp_up(c):
            up = jnp.maximum(_dot(mlp['hn'], wup_ref[:, c * FF_CHUNK:(c + 1) * FF_CHUNK]), 0.0)
            mlp['up'] = (up * up).astype(BF16)

        def mlp_down(c):
            d = _dot(mlp['up'], wdn_ref[c * FF_CHUNK:(c + 1) * FF_CHUNK, :])
            mlp['acc'] = d if mlp['acc'] is None else mlp['acc'] + d

        def mlp_gate():
            mlp['h2'] = mlp['h'] + _rms_rows(mlp['acc'], g2_ref[...])
            mlp['z'] = _dot(mlp['h2'].astype(BF16), wgate_ref[...]) + bgate_ref[...]

        def mlp_ple():
            mlp['ple'] = _dot(p_ref[0].astype(BF16), wple_ref[...])

        def mlp_end():
            gate = 1.0 / (1.0 + jnp.exp(-mlp['z']))
            o_ref[0] = mlp['h2'] + _rms_rows(mlp['ple'] * gate, g3_ref[...])

        n_ff = D_FF // FF_CHUNK
        heads_per_piece = N_HEADS // (2 * n_ff)
        assert heads_per_piece * 2 * n_ff == N_HEADS
        pieces_before_head = [[mlp_begin]] + [[] for _ in range(N_HEADS - 1)]
        for c in range(n_ff):
            pieces_before_head[2 * c * heads_per_piece].append(functools.partial(mlp_up, c))
            pieces_before_head[(2 * c + 1) * heads_per_piece].append(functools.partial(mlp_down, c))

        kpos = lax.broadcasted_iota(jnp.int32, (MOBA_BLOCK, tq), 0)
        qpos = lax.broadcasted_iota(jnp.int32, (MOBA_BLOCK, tq), 1)
        causal = kpos <= qpos
        lane = lax.broadcasted_iota(jnp.int32, (GMLP_CHUNK, LANES), 1)
        n_lane_tiles = D_GMLP // LANES
        n_chunks = tq // GMLP_CHUNK
        assert n_lane_tiles * n_chunks == N_HEADS
        heads_out = []
        cols = [[None] * n_lane_tiles for _ in range(n_chunks)]
        for h in range(N_HEADS):
            for piece in pieces_before_head[h] if with_mlp else ():
                piece()
            c, gp = divmod(h, n_lane_tiles)
            vp = vn_scr[c * GMLP_CHUNK:(c + 1) * GMLP_CHUNK, gp * LANES:(gp + 1) * LANES]
            stacked = jnp.concatenate(
                [jnp.where(lane // GROUP_DIM == part, vp, jnp.zeros_like(vp))
                 for part in range(GROUPS_PER_LANE_TILE)], axis=0)
            cols[c][gp] = _dot(wsm_ref[gp], stacked)

            st = jnp.where(causal, s_scr[h], NEG_INF)
            m_run = m_scr[h]
            m_new = jnp.maximum(m_run, jnp.max(st, axis=0, keepdims=True))
            alpha = jnp.exp2(m_run - m_new)
            p = jnp.exp2((st - m_new).astype(BF16))
            acc = alpha * acc_scr[h] + _dot(v_rows(i, h), p)
            heads_out.append(acc[0:HEAD_DIM] / acc[HEAD_DIM:HEAD_DIM + 1])

        if with_mlp:
            mlp_ple()
        mixed = jnp.concatenate(
            [jnp.concatenate(cols[c], axis=1) + bs_ref[...] for c in range(n_chunks)], axis=0)
        gm_out = ug_scr[...].astype(F32) * mixed
        gm_n = _rms_rows(gm_out, gg_ref[...]).astype(BF16)
        y = _dot(gm_n, wo_ref[D_ATTN:, :])
        if with_mlp:
            mlp_gate()

        attn_t = jnp.concatenate(heads_out, axis=0)
        ms_a = jnp.mean(attn_t * attn_t, axis=0, keepdims=True)
        attn_n = (attn_t * lax.rsqrt(ms_a + NORM_EPS) * ag_ref[...]).astype(BF16)
        y = y + _dot_tn(attn_n, wo_ref[:D_ATTN, :])
        y_scr[...] = y
        if with_mlp:
            mlp_end()

    @pl.when(step > 0)
    def _():
        mix_tail(True)

    @pl.when(step == 0)
    def _():
        mix_tail(False)


def _const_spec(shape):
    zeros = (0,) * len(shape)
    return pl.BlockSpec(shape, lambda *_: zeros, pipeline_mode=pl.Buffered(1))


def _rope_tables(seq, block):
    inv_freq = ROPE_THETA ** (-np.arange(ROPE_HALF, dtype=np.float64) / ROPE_HALF)
    ang = np.arange(seq, dtype=np.float64)[:, None] * inv_freq[None, :]
    cos, sin = np.cos(ang), np.sin(ang)
    n_blocks = seq // block
    cos_t, sin_t = (t.T.reshape(ROPE_HALF, n_blocks, block).transpose(1, 0, 2) for t in (cos, sin))
    ones = np.ones((seq, HEAD_DIM - ROPE_DIM))
    zeros = np.zeros((seq, HEAD_DIM - ROPE_DIM))
    zh = np.zeros((seq, ROPE_HALF))
    cos_h = np.concatenate([cos, cos, ones], axis=1)
    sa_h = np.concatenate([-sin, zh, zeros], axis=1)
    sb_h = np.concatenate([zh, sin, zeros], axis=1)
    tile = lambda t: np.concatenate([t] * HEADS_PER_LANE_TILE, axis=1).reshape(n_blocks, block, LANES)
    tables = (cos_t, sin_t, tile(cos_h), tile(sa_h), tile(sb_h))
    return tuple(jnp.asarray(np.ascontiguousarray(t), dtype=F32) for t in tables)


def _layer(h, p, mix_pre_g, w_in, gmlp_ln_g, gmlp_ln_b, w_s, b_s, attn_out_g,
           gmlp_out_g, w_o, mix_post_g, mlp_pre_g, w_up, w_down, mlp_post_g,
           w_ple, w_ple_gate, b_ple_gate, ple_post_g):
    B, S, D = h.shape
    assert D == D_MODEL and S % MOBA_BLOCK == 0
    nb = S // MOBA_BLOCK
    T = MOBA_BLOCK
    n_tiles = B * nb
    row = lambda v: v.reshape(1, -1).astype(F32)

    qscale = np.ones((1, D_IN_PROJ), np.float32)
    qscale[:, :D_ATTN] = HEAD_DIM ** -0.5 * math.log2(math.e)
    cos_t, sin_t, cos_k, sa_k, sb_k = _rope_tables(S, T)
    bs_exp = jnp.repeat(b_s.T, GROUP_DIM, axis=1).astype(F32)

    def cur(s):
        t = jnp.minimum(s, n_tiles - 1)
        return t // nb, t % nb

    def prev(s):
        t = jnp.maximum(s - 1, 0)
        return t // nb, t % nb

    hbm = pl.BlockSpec(memory_space=pl.ANY)
    out = pl.pallas_call(
        _layer_kernel,
        grid=(n_tiles + 1,),
        in_specs=[
            pl.BlockSpec((1, T, D), lambda s: (*cur(s), 0)),
            pl.BlockSpec((1, T, D), lambda s: (*prev(s), 0)),
            pl.BlockSpec((1, T, D_PLE), lambda s: (*prev(s), 0)),
            _const_spec((1, D)),
            hbm,
            _const_spec((1, D_IN_PROJ)),
            _const_spec((nb, ROPE_HALF, T)),
            _const_spec((nb, ROPE_HALF, T)),
            _const_spec((nb, T, LANES)),
            _const_spec((nb, T, LANES)),
            _const_spec((nb, T, LANES)),
            _const_spec((1, D_GMLP)),
            _const_spec((1, D_GMLP)),
            _const_spec((N_GROUPS, GMLP_CHUNK, GMLP_CHUNK)),
            _const_spec((GMLP_CHUNK, D_GMLP)),
            _const_spec((D_ATTN, 1)),
            _const_spec((1, D_GMLP)),
            hbm,
            _const_spec((1, D)),
            _const_spec((1, D)),
            hbm,
            hbm,
            _const_spec((1, D)),
            hbm,
            _const_spec((1, D)),
            hbm,
            _const_spec((1, D)),
        ],
        out_specs=pl.BlockSpec((1, T, D), lambda s: (*prev(s), 0)),
        out_shape=jax.ShapeDtypeStruct((B, S, D), F32),
        scratch_shapes=([pltpu.VMEM((nb, T, D_ATTN), BF16),
                         pltpu.VMEM((nb, N_HEADS * V_ROWS, T), BF16),
                         pltpu.VMEM((nb, D_ATTN), F32),
                         pltpu.VMEM((T, D_GMLP), BF16),
                         pltpu.VMEM((T, D_GMLP), BF16),
                         pltpu.VMEM((T, D), F32),
                         pltpu.VMEM((T, D), F32),
                         pltpu.VMEM((D, D_IN_PROJ), BF16),
                         pltpu.VMEM((D_ATTN + D_GMLP, D), BF16),
                         pltpu.VMEM((D, D_FF), BF16),
                         pltpu.VMEM((D_FF, D), BF16),
                         pltpu.VMEM((D, D), BF16),
                         pltpu.VMEM((D_PLE, D), BF16),
                         pltpu.VMEM((D_GMLP // LANES, GMLP_CHUNK, GROUPS_PER_LANE_TILE * GMLP_CHUNK), BF16),
                         pltpu.VMEM((2, STAGE_ROWS, D_FF), F32),
                         pltpu.SemaphoreType.DMA((2, COPY_SPLIT))]
                        + [pltpu.VMEM((N_HEADS, nb, T), F32),
                           pltpu.VMEM((N_HEADS, LANES, T), BF16),
                           pltpu.VMEM((N_HEADS, 1, T), F32),
                           pltpu.VMEM((N_HEADS, V_ROWS, T), F32),
                           pltpu.VMEM((N_HEADS, MOBA_BLOCK, T), F32)]),
        compiler_params=pltpu.CompilerParams(dimension_semantics=("arbitrary",),
                                             vmem_limit_bytes=VMEM_LIMIT_BYTES),
        name="layer",
    )(h, h, p, row(mix_pre_g), w_in.astype(F32), jnp.asarray(qscale), cos_t, sin_t, cos_k, sa_k, sb_k,
      row(gmlp_ln_g), row(gmlp_ln_b), w_s.astype(F32), bs_exp,
      attn_out_g.reshape(-1, 1).astype(F32), row(gmlp_out_g), w_o.astype(F32), row(mix_post_g),
      row(mlp_pre_g), w_up.astype(F32), w_down.astype(F32), row(mlp_post_g),
      w_ple_gate.astype(F32), row(b_ple_gate), w_ple.astype(F32), row(ple_post_g))
    return out


def kernel(x, p, mix_pre_g, w_in, gmlp_ln_g, gmlp_ln_b, w_s, b_s, attn_out_g,
           gmlp_out_g, w_o, mix_post_g, mlp_pre_g, w_up, w_down, mlp_post_g,
           w_ple, w_ple_gate, b_ple_gate, ple_post_g):
    h = x
    for i in range(w_in.shape[0]):
        h = _layer(h, p[i], mix_pre_g[i], w_in[i], gmlp_ln_g[i], gmlp_ln_b[i],
                   w_s[i], b_s[i], attn_out_g[i], gmlp_out_g[i], w_o[i],
                   mix_post_g[i], mlp_pre_g[i], w_up[i], w_down[i], mlp_post_g[i],
                   w_ple[i], w_ple_gate[i], b_ple_gate[i], ple_post_g[i])
    return h
```

```python
import functools
import math

import jax
import jax.numpy as jnp
import numpy as np
from jax import lax
from jax.experimental import pallas as pl
from jax.experimental.pallas import tpu as pltpu

D_MODEL = 1024
D_PLE = 256
N_HEADS = 8
HEAD_DIM = 64
D_ATTN = N_HEADS * HEAD_DIM
N_GROUPS = 8
GROUP_DIM = 64
D_GMLP = N_GROUPS * GROUP_DIM
D_IN_PROJ = 3 * D_ATTN + 2 * D_GMLP
D_FF = 4 * D_MODEL
MOBA_BLOCK = 256
MOBA_TOPK = 3
GMLP_CHUNK = 128
ROPE_THETA = 500000.0
ROPE_DIM = HEAD_DIM // 4
ROPE_HALF = ROPE_DIM // 2
NORM_EPS = 1e-6
NEG_INF = -1e30

LANES = 128
HEADS_PER_LANE_TILE = LANES // HEAD_DIM
GROUPS_PER_LANE_TILE = LANES // GROUP_DIM
BF16_SUBLANES = 16
V_ROWS = HEAD_DIM + BF16_SUBLANES
BIG = 1e30
SCORE_LEAD = 2
VMEM_LIMIT_BYTES = 58 * 1024 * 1024
FF_CHUNK = 1024
FINISH_ROWS = 64
STAGE_ROWS = 128

F32 = jnp.float32
BF16 = jnp.bfloat16


def _dot(a, b):
    return jnp.dot(a, b, preferred_element_type=F32)


def _dot_tn(a, b):
    return lax.dot_general(a, b, (((0,), (0,)), ((), ())), preferred_element_type=F32)


def _rms_rows(x, g):
    ms = jnp.mean(x * x, axis=-1, keepdims=True)
    return x * lax.rsqrt(ms + NORM_EPS) * g


def _gelu_exact(x):
    return 0.5 * x * (1.0 + lax.erf(x * math.sqrt(0.5)))


def _load_weight_bf16(src_hbm, dst_ref, slots, sem_ref, col_scale=None, phase="all"):
    n_rows, width = src_hbm.shape
    chunk = min(slots[0].shape[0], n_rows)
    n_chunks = n_rows // chunk
    assert n_chunks * chunk == n_rows and width <= slots[0].shape[1]
    assert n_chunks == 1 or n_chunks % 2 == 0
    assert phase == "all" or n_chunks >= 4

    def copy(c, k):
        return pltpu.make_async_copy(src_hbm.at[pl.ds(c * chunk, chunk), :],
                                     slots[k].at[pl.ds(0, chunk), pl.ds(0, width)],
                                     sem_ref.at[k])

    def convert(c, k):
        w = slots[k][pl.ds(0, chunk), pl.ds(0, width)]
        if col_scale is not None:
            w = w * col_scale
        dst_ref[pl.ds(pl.multiple_of(c * chunk, chunk), chunk), :] = w.astype(BF16)

    if phase == "finish":
        for k in range(2):
            copy(n_chunks - 2 + k, k).wait()
            convert(n_chunks - 2 + k, k)
        return

    copy(0, 0).start()
    if n_chunks == 1:
        copy(0, 0).wait()
        convert(0, 0)
        return

    n_now = n_chunks if phase == "all" else n_chunks - 2

    def body(t, carry):
        c = 2 * t
        copy(c + 1, 1).start()
        copy(c, 0).wait()
        convert(c, 0)

        @pl.when(c + 2 < n_chunks)
        def _():
            copy(c + 2, 0).start()

        copy(c + 1, 1).wait()
        convert(c + 1, 1)
        return carry

    lax.fori_loop(0, n_now // 2, body, 0)
    if phase == "head":
        copy(n_chunks - 1, 1).start()


def _layer_kernel(x_ref, xp_ref, p_ref, g_ref, win_hbm, qscale_ref, cost_ref, sint_ref,
                  cosk_ref, sak_ref, sbk_ref, lng_ref, lnb_ref,
                  ws_ref, bs_ref, ag_ref, gg_ref, wo_hbm, pg_ref,
                  g1_ref, wup_hbm, wdn_hbm, g2_ref, wgate_hbm, bgate_ref, wple_hbm, g3_ref,
                  o_ref,
                  k_scr, vt_scr, km_scr, ug_scr, vn_scr, h1_scr, y_scr,
                  win_ref, wo_ref, wup_ref, wdn_ref, wgate_ref, wple_ref, wsm_ref, stage, sem,
                  *scratch):
    sel_scr, qz_scr, m_scr, acc_scr, s_scr = scratch
    nb = k_scr.shape[0]
    tq = MOBA_BLOCK
    n_tiles = pl.num_programs(0) - 1
    step = pl.program_id(0)
    live = step < n_tiles
    i = lax.rem(jnp.minimum(step, n_tiles - 1), nb)

    @pl.when(step == 0)
    def _():
        km_scr[...] = jnp.zeros(km_scr.shape, F32)
        wide = (stage.at[0], stage.at[1])
        narrow = (h1_scr, y_scr)
        _load_weight_bf16(win_hbm, win_ref, wide, sem, col_scale=qscale_ref[...])
        _load_weight_bf16(wo_hbm, wo_ref, narrow, sem)
        _load_weight_bf16(wdn_hbm, wdn_ref, narrow, sem)
        _load_weight_bf16(wgate_hbm, wgate_ref, narrow, sem)
        _load_weight_bf16(wple_hbm, wple_ref, narrow, sem)
        y_scr[...] = jnp.zeros(y_scr.shape, F32)
        _load_weight_bf16(wup_hbm, wup_ref, wide, sem, phase="head")
        t_idx = lax.broadcasted_iota(jnp.int32, (GMLP_CHUNK, GMLP_CHUNK), 0)
        s_idx = lax.broadcasted_iota(jnp.int32, (GMLP_CHUNK, GMLP_CHUNK), 1)
        for g in range(N_GROUPS):
            tile, part = divmod(g, GROUPS_PER_LANE_TILE)
            wsm_ref[tile, :, part * GMLP_CHUNK:(part + 1) * GMLP_CHUNK] = (
                jnp.where(s_idx <= t_idx, ws_ref[g], 0.0).astype(BF16))

    @pl.when(step == 1)
    def _():
        _load_weight_bf16(wup_hbm, wup_ref, (stage.at[0], stage.at[1]), sem, phase="finish")

    def scores(blk, h):
        hp = h // HEADS_PER_LANE_TILE
        return _dot(k_scr[blk, :, hp * LANES:(hp + 1) * LANES], qz_scr[h])

    def v_rows(blk, h):
        return vt_scr[blk, h * V_ROWS:(h + 1) * V_ROWS, :]

    def finish_prev(r):
        rows = slice(r * FINISH_ROWS, (r + 1) * FINISH_ROWS)
        h1_scr[rows, :] = xp_ref[0, rows, :] + _rms_rows(y_scr[rows, :], pg_ref[...])

    @pl.when(jnp.logical_not(live))
    def _():
        for r in range(tq // FINISH_ROWS):
            finish_prev(r)

    @pl.when(live)
    def _():
        x = x_ref[0]
        hn = _rms_rows(x, g_ref[...]).astype(BF16)

        gv = _dot(hn, win_ref[:, 3 * D_ATTN + D_GMLP:])
        finish_prev(0)
        g2 = _gelu_exact(gv)
        mu = jnp.mean(g2, axis=-1, keepdims=True)
        xc = g2 - mu
        var = jnp.mean(xc * xc, axis=-1, keepdims=True)
        vn = xc * lax.rsqrt(var + NORM_EPS) * lng_ref[...] + lnb_ref[...]
        vn_scr[...] = vn.astype(BF16)

        qt = _dot(hn, win_ref[:, :D_ATTN]).T
        finish_prev(1)

        tok = _dot(hn, win_ref[:, D_ATTN:2 * D_ATTN])
        finish_prev(2)
        cos_k = cosk_ref[i]
        sa_k = sak_ref[i]
        sb_k = sbk_ref[i]
        kparts = []
        for j in range(D_ATTN // LANES):
            kb = tok[:, j * LANES:(j + 1) * LANES]
            up = pltpu.roll(kb, LANES - ROPE_HALF, axis=1)
            dn = pltpu.roll(kb, ROPE_HALF, axis=1)
            kparts.append(kb * cos_k + up * sa_k + dn * sb_k)
        k = jnp.concatenate(kparts, axis=1)
        k_scr[i] = k.astype(BF16)
        km_scr[pl.ds(i, 1), :] = jnp.mean(k, axis=0, keepdims=True)

        u = _dot(hn, win_ref[:, 3 * D_ATTN:3 * D_ATTN + D_GMLP])
        finish_prev(3)
        ug_scr[...] = _gelu_exact(u).astype(BF16)

        vt = _dot(hn, win_ref[:, 2 * D_ATTN:3 * D_ATTN]).T
        ones = jnp.ones((V_ROWS - HEAD_DIM, tq), F32)
        vparts = []
        for h in range(N_HEADS):
            vparts.append(vt[h * HEAD_DIM:(h + 1) * HEAD_DIM])
            vparts.append(ones)
        vt_scr[i] = jnp.concatenate(vparts, axis=0).astype(BF16)

        cos_t = cost_ref[i]
        sin_t = sint_ref[i]
        zeros_head = jnp.zeros((HEAD_DIM, tq), F32)
        for h in range(N_HEADS):
            hh = h % HEADS_PER_LANE_TILE
            r0 = h * HEAD_DIM
            x1 = qt[r0:r0 + ROPE_HALF]
            x2 = qt[r0 + ROPE_HALF:r0 + ROPE_DIM]
            q_h = [x1 * cos_t - x2 * sin_t, x2 * cos_t + x1 * sin_t, qt[r0 + ROPE_DIM:r0 + HEAD_DIM]]
            padded = [zeros_head] * hh + q_h + [zeros_head] * (HEADS_PER_LANE_TILE - 1 - hh)
            qz_scr[h] = jnp.concatenate(padded, axis=0).astype(BF16)

        blk_row = lax.broadcasted_iota(jnp.int32, (nb, tq), 0)
        past = blk_row < i
        km = km_scr[...]
        km_hi = km.astype(BF16)
        r1 = km - km_hi.astype(F32)
        km_mid = r1.astype(BF16)
        km_lo = (r1 - km_mid.astype(F32)).astype(BF16)
        km3 = jnp.concatenate([km_hi, km_mid, km_lo], axis=0)
        for h in range(N_HEADS):
            hp = h // HEADS_PER_LANE_TILE
            s_scr[h] = scores(0, h)

            g3 = _dot(km3[:, hp * LANES:(hp + 1) * LANES], qz_scr[h])
            gate = g3[0:nb] + g3[nb:2 * nb] + g3[2 * nb:3 * nb]
            gate = jnp.where(past, gate, NEG_INF)
            rank = jnp.zeros((nb, tq), jnp.int32)
            for m in range(nb):
                gm = gate[m:m + 1, :]
                beats = (gm > gate) | ((gm == gate) & (m < blk_row))
                rank = rank + beats.astype(jnp.int32)
            sel = past & (rank < MOBA_TOPK)
            sel_scr[h] = sel.astype(F32)
            m_scr[h] = jnp.full((1, tq), NEG_INF, F32)
            acc_scr[h] = jnp.zeros((V_ROWS, tq), F32)

    def past_blocks(kb0, n_blocks):
        units = [(kb0 + j, h) for j in range(n_blocks) for h in range(N_HEADS)]
        ahead = {}
        for u in range(-SCORE_LEAD, len(units)):
            if 0 <= u + SCORE_LEAD < len(units):
                blk, head = units[u + SCORE_LEAD]
                ahead[u + SCORE_LEAD] = scores(blk + 1, head)
            if u < 0:
                continue
            kb, h = units[u]
            s = s_scr[h]
            keep = sel_scr[h, pl.ds(kb, 1), :] > 0.5
            m_run = m_scr[h]
            m_new = jnp.where(keep, jnp.maximum(m_run, jnp.max(s, axis=0, keepdims=True)), m_run)
            m_sub = jnp.where(keep, m_new, BIG)
            alpha = jnp.exp2(m_run - m_new)
            pb = jnp.exp2((s - m_sub).astype(BF16))
            m_scr[h] = m_new
            acc_scr[h] = alpha * acc_scr[h] + _dot(v_rows(kb, h), pb)
            s_scr[h] = ahead.pop(u)

    def four_blocks(t, carry):
        past_blocks(4 * t, 4)
        return carry

    n_past = jnp.where(live, i, 0)
    lax.fori_loop(0, n_past // 4, four_blocks, 0)
    n_done = 4 * (n_past // 4)

    @pl.when(n_past - n_done >= 2)
    def _():
        past_blocks(n_done, 2)

    @pl.when(lax.rem(n_past, 2) == 1)
    def _():
        past_blocks(n_past - 1, 1)

    def mix_tail(with_mlp):
        mlp = {}

        def mlp_begin():
            mlp['h'] = h1_scr[...]
            mlp['hn'] = _rms_rows(mlp['h'], g1_ref[...]).astype(BF16)
            mlp['acc'] = None

        def mlp_up(c):
            up = jnp.maximum(_dot(mlp['hn'], wup_ref[:, c * FF_CHUNK:(c + 1) * FF_CHUNK]), 0.0)
            mlp['up'] = (up * up).astype(BF16)

        def mlp_down(c):
            d = _dot(mlp['up'], wdn_ref[c * FF_CHUNK:(c + 1) * FF_CHUNK, :])
            mlp['acc'] = d if mlp['acc'] is None else mlp['acc'] + d

        def mlp_gate():
            mlp['h2'] = mlp['h'] + _rms_rows(mlp['acc'], g2_ref[...])
            mlp['z'] = _dot(mlp['h2'].astype(BF16), wgate_ref[...]) + bgate_ref[...]

        def mlp_ple():
            mlp['ple'] = _dot(p_ref[0].astype(BF16), wple_ref[...])

        def mlp_end():
            gate = 1.0 / (1.0 + jnp.exp(-mlp['z']))
            o_ref[0] = mlp['h2'] + _rms_rows(mlp['ple'] * gate, g3_ref[...])

        n_ff = D_FF // FF_CHUNK
        heads_per_piece = N_HEADS // (2 * n_ff)
        assert heads_per_piece * 2 * n_ff == N_HEADS
        pieces_before_head = [[mlp_begin]] + [[] for _ in range(N_HEADS - 1)]
        for c in range(n_ff):
            pieces_before_head[2 * c * heads_per_piece].append(functools.partial(mlp_up, c))
            pieces_before_head[(2 * c + 1) * heads_per_piece].append(functools.partial(mlp_down, c))

        kpos = lax.broadcasted_iota(jnp.int32, (MOBA_BLOCK, tq), 0)
        qpos = lax.broadcasted_iota(jnp.int32, (MOBA_BLOCK, tq), 1)
        causal = kpos <= qpos
        lane = lax.broadcasted_iota(jnp.int32, (GMLP_CHUNK, LANES), 1)
        n_lane_tiles = D_GMLP // LANES
        n_chunks = tq // GMLP_CHUNK
        assert n_lane_tiles * n_chunks == N_HEADS
        heads_out = []
        cols = [[None] * n_lane_tiles for _ in range(n_chunks)]
        for h in range(N_HEADS):
            for piece in pieces_before_head[h] if with_mlp else ():
                piece()
            c, gp = divmod(h, n_lane_tiles)
            vp = vn_scr[c * GMLP_CHUNK:(c + 1) * GMLP_CHUNK, gp * LANES:(gp + 1) * LANES]
            stacked = jnp.concatenate(
                [jnp.where(lane // GROUP_DIM == part, vp, jnp.zeros_like(vp))
                 for part in range(GROUPS_PER_LANE_TILE)], axis=0)
            cols[c][gp] = _dot(wsm_ref[gp], stacked)

            st = jnp.where(causal, s_scr[h], NEG_INF)
            m_run = m_scr[h]
            m_new = jnp.maximum(m_run, jnp.max(st, axis=0, keepdims=True))
            alpha = jnp.exp2(m_run - m_new)
            p = jnp.exp2((st - m_new).astype(BF16))
            acc = alpha * acc_scr[h] + _dot(v_rows(i, h), p)
            heads_out.append(acc[0:HEAD_DIM] / acc[HEAD_DIM:HEAD_DIM + 1])

        if with_mlp:
            mlp_ple()
        mixed = jnp.concatenate(
            [jnp.concatenate(cols[c], axis=1) + bs_ref[...] for c in range(n_chunks)], axis=0)
        gm_out = ug_scr[...].astype(F32) * mixed
        gm_n = _rms_rows(gm_out, gg_ref[...]).astype(BF16)
        y = _dot(gm_n, wo_ref[D_ATTN:, :])
        if with_mlp:
            mlp_gate()

        attn_t = jnp.concatenate(heads_out, axis=0)
        ms_a = jnp.mean(attn_t * attn_t, axis=0, keepdims=True)
        attn_n = (attn_t * lax.rsqrt(ms_a + NORM_EPS) * ag_ref[...]).astype(BF16)
        y = y + _dot_tn(attn_n, wo_ref[:D_ATTN, :])
        y_scr[...] = y
        if with_mlp:
            mlp_end()

    @pl.when(step > 0)
    def _():
        mix_tail(True)

    @pl.when(step == 0)
    def _():
        mix_tail(False)


def _const_spec(shape):
    zeros = (0,) * len(shape)
    return pl.BlockSpec(shape, lambda *_: zeros, pipeline_mode=pl.Buffered(1))


def _rope_tables(seq, block):
    inv_freq = ROPE_THETA ** (-np.arange(ROPE_HALF, dtype=np.float64) / ROPE_HALF)
    ang = np.arange(seq, dtype=np.float64)[:, None] * inv_freq[None, :]
    cos, sin = np.cos(ang), np.sin(ang)
    n_blocks = seq // block
    cos_t, sin_t = (t.T.reshape(ROPE_HALF, n_blocks, block).transpose(1, 0, 2) for t in (cos, sin))
    ones = np.ones((seq, HEAD_DIM - ROPE_DIM))
    zeros = np.zeros((seq, HEAD_DIM - ROPE_DIM))
    zh = np.zeros((seq, ROPE_HALF))
    cos_h = np.concatenate([cos, cos, ones], axis=1)
    sa_h = np.concatenate([-sin, zh, zeros], axis=1)
    sb_h = np.concatenate([zh, sin, zeros], axis=1)
    tile = lambda t: np.concatenate([t] * HEADS_PER_LANE_TILE, axis=1).reshape(n_blocks, block, LANES)
    tables = (cos_t, sin_t, tile(cos_h), tile(sa_h), tile(sb_h))
    return tuple(jnp.asarray(np.ascontiguousarray(t), dtype=F32) for t in tables)


def _layer(h, p, mix_pre_g, w_in, gmlp_ln_g, gmlp_ln_b, w_s, b_s, attn_out_g,
           gmlp_out_g, w_o, mix_post_g, mlp_pre_g, w_up, w_down, mlp_post_g,
           w_ple, w_ple_gate, b_ple_gate, ple_post_g):
    B, S, D = h.shape
    assert D == D_MODEL and S % MOBA_BLOCK == 0
    nb = S // MOBA_BLOCK
    T = MOBA_BLOCK
    n_tiles = B * nb
    row = lambda v: v.reshape(1, -1).astype(F32)

    qscale = np.ones((1, D_IN_PROJ), np.float32)
    qscale[:, :D_ATTN] = HEAD_DIM ** -0.5 * math.log2(math.e)
    cos_t, sin_t, cos_k, sa_k, sb_k = _rope_tables(S, T)
    bs_exp = jnp.repeat(b_s.T, GROUP_DIM, axis=1).astype(F32)

    def cur(s):
        t = jnp.minimum(s, n_tiles - 1)
        return t // nb, t % nb

    def prev(s):
        t = jnp.maximum(s - 1, 0)
        return t // nb, t % nb

    hbm = pl.BlockSpec(memory_space=pl.ANY)
    out = pl.pallas_call(
        _layer_kernel,
        grid=(n_tiles + 1,),
        in_specs=[
            pl.BlockSpec((1, T, D), lambda s: (*cur(s), 0)),
            pl.BlockSpec((1, T, D), lambda s: (*prev(s), 0)),
            pl.BlockSpec((1, T, D_PLE), lambda s: (*prev(s), 0)),
            _const_spec((1, D)),
            hbm,
            _const_spec((1, D_IN_PROJ)),
            _const_spec((nb, ROPE_HALF, T)),
            _const_spec((nb, ROPE_HALF, T)),
            _const_spec((nb, T, LANES)),
            _const_spec((nb, T, LANES)),
            _const_spec((nb, T, LANES)),
            _const_spec((1, D_GMLP)),
            _const_spec((1, D_GMLP)),
            _const_spec((N_GROUPS, GMLP_CHUNK, GMLP_CHUNK)),
            _const_spec((GMLP_CHUNK, D_GMLP)),
            _const_spec((D_ATTN, 1)),
            _const_spec((1, D_GMLP)),
            hbm,
            _const_spec((1, D)),
            _const_spec((1, D)),
            hbm,
            hbm,
            _const_spec((1, D)),
            hbm,
            _const_spec((1, D)),
            hbm,
            _const_spec((1, D)),
        ],
        out_specs=pl.BlockSpec((1, T, D), lambda s: (*prev(s), 0)),
        out_shape=jax.ShapeDtypeStruct((B, S, D), F32),
        scratch_shapes=([pltpu.VMEM((nb, T, D_ATTN), BF16),
                         pltpu.VMEM((nb, N_HEADS * V_ROWS, T), BF16),
                         pltpu.VMEM((nb, D_ATTN), F32),
                         pltpu.VMEM((T, D_GMLP), BF16),
                         pltpu.VMEM((T, D_GMLP), BF16),
                         pltpu.VMEM((T, D), F32),
                         pltpu.VMEM((T, D), F32),
                         pltpu.VMEM((D, D_IN_PROJ), BF16),
                         pltpu.VMEM((D_ATTN + D_GMLP, D), BF16),
                         pltpu.VMEM((D, D_FF), BF16),
                         pltpu.VMEM((D_FF, D), BF16),
                         pltpu.VMEM((D, D), BF16),
                         pltpu.VMEM((D_PLE, D), BF16),
                         pltpu.VMEM((D_GMLP // LANES, GMLP_CHUNK, GROUPS_PER_LANE_TILE * GMLP_CHUNK), BF16),
                         pltpu.VMEM((2, STAGE_ROWS, D_FF), F32),
                         pltpu.SemaphoreType.DMA((2,))]
                        + [pltpu.VMEM((N_HEADS, nb, T), F32),
                           pltpu.VMEM((N_HEADS, LANES, T), BF16),
                           pltpu.VMEM((N_HEADS, 1, T), F32),
                           pltpu.VMEM((N_HEADS, V_ROWS, T), F32),
                           pltpu.VMEM((N_HEADS, MOBA_BLOCK, T), F32)]),
        compiler_params=pltpu.CompilerParams(dimension_semantics=("arbitrary",),
                                             vmem_limit_bytes=VMEM_LIMIT_BYTES),
        name="layer",
    )(h, h, p, row(mix_pre_g), w_in.astype(F32), jnp.asarray(qscale), cos_t, sin_t, cos_k, sa_k, sb_k,
      row(gmlp_ln_g), row(gmlp_ln_b), w_s.astype(F32), bs_exp,
      attn_out_g.reshape(-1, 1).astype(F32), row(gmlp_out_g), w_o.astype(F32), row(mix_post_g),
      row(mlp_pre_g), w_up.astype(F32), w_down.astype(F32), row(mlp_post_g),
      w_ple_gate.astype(F32), row(b_ple_gate), w_ple.astype(F32), row(ple_post_g))
    return out


def kernel(x, p, mix_pre_g, w_in, gmlp_ln_g, gmlp_ln_b, w_s, b_s, attn_out_g,
           gmlp_out_g, w_o, mix_post_g, mlp_pre_g, w_up, w_down, mlp_post_g,
           w_ple, w_ple_gate, b_ple_gate, ple_post_g):
    h = x
    for i in range(w_in.shape[0]):
        h = _layer(h, p[i], mix_pre_g[i], w_in[i], gmlp_ln_g[i], gmlp_ln_b[i],
                   w_s[i], b_s[i], attn_out_g[i], gmlp_out_g[i], w_o[i],
                   mix_post_g[i], mlp_pre_g[i], w_up[i], w_down[i], mlp_post_g[i],
                   w_ple[i], w_ple_gate[i], b_ple_gate[i], ple_post_g[i])
    return h
```

```python
import functools
import math

import jax
import jax.numpy as jnp
import numpy as np
from jax import lax
from jax.experimental import pallas as pl
from jax.experimental.pallas import tpu as pltpu

D_MODEL = 1024
D_PLE = 256
N_HEADS = 8
HEAD_DIM = 64
D_ATTN = N_HEADS * HEAD_DIM
N_GROUPS = 8
GROUP_DIM = 64
D_GMLP = N_GROUPS * GROUP_DIM
D_IN_PROJ = 3 * D_ATTN + 2 * D_GMLP
D_FF = 4 * D_MODEL
MOBA_BLOCK = 256
MOBA_TOPK = 3
GMLP_CHUNK = 128
ROPE_THETA = 500000.0
ROPE_DIM = HEAD_DIM // 4
ROPE_HALF = ROPE_DIM // 2
NORM_EPS = 1e-6
NEG_INF = -1e30

LANES = 128
HEADS_PER_LANE_TILE = LANES // HEAD_DIM
GROUPS_PER_LANE_TILE = LANES // GROUP_DIM
BF16_SUBLANES = 16
V_ROWS = HEAD_DIM + BF16_SUBLANES
BIG = 1e30
SCORE_LEAD = 2
VMEM_LIMIT_BYTES = 58 * 1024 * 1024
FF_CHUNK = 1024
FINISH_ROWS = 64
STAGE_ROWS = 128

F32 = jnp.float32
BF16 = jnp.bfloat16


def _dot(a, b):
    return jnp.dot(a, b, preferred_element_type=F32)


def _dot_tn(a, b):
    return lax.dot_general(a, b, (((0,), (0,)), ((), ())), preferred_element_type=F32)


def _rms_rows(x, g):
    ms = jnp.mean(x * x, axis=-1, keepdims=True)
    return x * lax.rsqrt(ms + NORM_EPS) * g


def _gelu_exact(x):
    return 0.5 * x * (1.0 + lax.erf(x * math.sqrt(0.5)))


def _load_weight_bf16(src_hbm, dst_ref, slots, sem_ref, col_scale=None, phase="all"):
    n_rows, width = src_hbm.shape
    chunk = min(slots[0].shape[0], n_rows)
    n_chunks = n_rows // chunk
    assert n_chunks * chunk == n_rows and width <= slots[0].shape[1]
    assert n_chunks == 1 or n_chunks % 2 == 0
    assert phase == "all" or n_chunks >= 6

    def copy(c, k):
        return pltpu.make_async_copy(src_hbm.at[pl.ds(c * chunk, chunk), :],
                                     slots[k].at[pl.ds(0, chunk), pl.ds(0, width)],
                                     sem_ref.at[k])

    def convert(c, k):
        w = slots[k][pl.ds(0, chunk), pl.ds(0, width)]
        if col_scale is not None:
            w = w * col_scale
        dst_ref[pl.ds(pl.multiple_of(c * chunk, chunk), chunk), :] = w.astype(BF16)

    if phase in ("mid", "finish"):
        first = n_chunks - (4 if phase == "mid" else 2)
        for k in range(2):
            copy(first + k, k).wait()
            convert(first + k, k)
            if phase == "mid":
                copy(first + 2 + k, k).start()
        return

    copy(0, 0).start()
    if n_chunks == 1:
        copy(0, 0).wait()
        convert(0, 0)
        return

    n_now = n_chunks if phase == "all" else n_chunks - 4

    def body(t, carry):
        c = 2 * t
        copy(c + 1, 1).start()
        copy(c, 0).wait()
        convert(c, 0)

        @pl.when(c + 2 < n_chunks)
        def _():
            copy(c + 2, 0).start()

        copy(c + 1, 1).wait()
        convert(c + 1, 1)
        return carry

    lax.fori_loop(0, n_now // 2, body, 0)
    if phase == "head":
        copy(n_now + 1, 1).start()


def _layer_kernel(x_ref, xp_ref, p_ref, g_ref, win_hbm, qscale_ref, cost_ref, sint_ref,
                  cosk_ref, sak_ref, sbk_ref, lng_ref, lnb_ref,
                  ws_ref, bs_ref, ag_ref, gg_ref, wo_hbm, pg_ref,
                  g1_ref, wup_hbm, wdn_hbm, g2_ref, wgate_hbm, bgate_ref, wple_hbm, g3_ref,
                  o_ref,
                  k_scr, vt_scr, km_scr, ug_scr, vn_scr, h1_scr, y_scr,
                  win_ref, wo_ref, wup_ref, wdn_ref, wgate_ref, wple_ref, wsm_ref, stage, sem,
                  *scratch):
    sel_scr, qz_scr, m_scr, acc_scr, s_scr = scratch
    nb = k_scr.shape[0]
    tq = MOBA_BLOCK
    n_tiles = pl.num_programs(0) - 1
    step = pl.program_id(0)
    live = step < n_tiles
    i = lax.rem(jnp.minimum(step, n_tiles - 1), nb)

    @pl.when(step == 0)
    def _():
        km_scr[...] = jnp.zeros(km_scr.shape, F32)
        wide = (stage.at[0], stage.at[1])
        narrow = (h1_scr, y_scr)
        _load_weight_bf16(win_hbm, win_ref, wide, sem, col_scale=qscale_ref[...])
        _load_weight_bf16(wo_hbm, wo_ref, narrow, sem)
        _load_weight_bf16(wdn_hbm, wdn_ref, narrow, sem)
        _load_weight_bf16(wgate_hbm, wgate_ref, narrow, sem)
        _load_weight_bf16(wple_hbm, wple_ref, narrow, sem)
        y_scr[...] = jnp.zeros(y_scr.shape, F32)
        _load_weight_bf16(wup_hbm, wup_ref, wide, sem, phase="head")
        t_idx = lax.broadcasted_iota(jnp.int32, (GMLP_CHUNK, GMLP_CHUNK), 0)
        s_idx = lax.broadcasted_iota(jnp.int32, (GMLP_CHUNK, GMLP_CHUNK), 1)
        for g in range(N_GROUPS):
            tile, part = divmod(g, GROUPS_PER_LANE_TILE)
            wsm_ref[tile, :, part * GMLP_CHUNK:(part + 1) * GMLP_CHUNK] = (
                jnp.where(s_idx <= t_idx, ws_ref[g], 0.0).astype(BF16))

    @pl.when(step == 1)
    def _():
        _load_weight_bf16(wup_hbm, wup_ref, (stage.at[0], stage.at[1]), sem, phase="mid")

    def scores(blk, h):
        hp = h // HEADS_PER_LANE_TILE
        return _dot(k_scr[blk, :, hp * LANES:(hp + 1) * LANES], qz_scr[h])

    def v_rows(blk, h):
        return vt_scr[blk, h * V_ROWS:(h + 1) * V_ROWS, :]

    def finish_prev(r):
        rows = slice(r * FINISH_ROWS, (r + 1) * FINISH_ROWS)
        h1_scr[rows, :] = xp_ref[0, rows, :] + _rms_rows(y_scr[rows, :], pg_ref[...])

    @pl.when(jnp.logical_not(live))
    def _():
        for r in range(tq // FINISH_ROWS):
            finish_prev(r)

    @pl.when(live)
    def _():
        x = x_ref[0]
        hn = _rms_rows(x, g_ref[...]).astype(BF16)

        gv = _dot(hn, win_ref[:, 3 * D_ATTN + D_GMLP:])
        finish_prev(0)
        g2 = _gelu_exact(gv)
        mu = jnp.mean(g2, axis=-1, keepdims=True)
        xc = g2 - mu
        var = jnp.mean(xc * xc, axis=-1, keepdims=True)
        vn = xc * lax.rsqrt(var + NORM_EPS) * lng_ref[...] + lnb_ref[...]
        vn_scr[...] = vn.astype(BF16)

        qt = _dot(hn, win_ref[:, :D_ATTN]).T
        finish_prev(1)

        tok = _dot(hn, win_ref[:, D_ATTN:2 * D_ATTN])
        finish_prev(2)
        cos_k = cosk_ref[i]
        sa_k = sak_ref[i]
        sb_k = sbk_ref[i]
        kparts = []
        for j in range(D_ATTN // LANES):
            kb = tok[:, j * LANES:(j + 1) * LANES]
            up = pltpu.roll(kb, LANES - ROPE_HALF, axis=1)
            dn = pltpu.roll(kb, ROPE_HALF, axis=1)
            kparts.append(kb * cos_k + up * sa_k + dn * sb_k)
        k = jnp.concatenate(kparts, axis=1)
        k_scr[i] = k.astype(BF16)
        km_scr[pl.ds(i, 1), :] = jnp.mean(k, axis=0, keepdims=True)

        u = _dot(hn, win_ref[:, 3 * D_ATTN:3 * D_ATTN + D_GMLP])
        finish_prev(3)
        ug_scr[...] = _gelu_exact(u).astype(BF16)

        vt = _dot(hn, win_ref[:, 2 * D_ATTN:3 * D_ATTN]).T
        ones = jnp.ones((V_ROWS - HEAD_DIM, tq), F32)
        vparts = []
        for h in range(N_HEADS):
            vparts.append(vt[h * HEAD_DIM:(h + 1) * HEAD_DIM])
            vparts.append(ones)
        vt_scr[i] = jnp.concatenate(vparts, axis=0).astype(BF16)

        cos_t = cost_ref[i]
        sin_t = sint_ref[i]
        zeros_head = jnp.zeros((HEAD_DIM, tq), F32)
        for h in range(N_HEADS):
            hh = h % HEADS_PER_LANE_TILE
            r0 = h * HEAD_DIM
            x1 = qt[r0:r0 + ROPE_HALF]
            x2 = qt[r0 + ROPE_HALF:r0 + ROPE_DIM]
            q_h = [x1 * cos_t - x2 * sin_t, x2 * cos_t + x1 * sin_t, qt[r0 + ROPE_DIM:r0 + HEAD_DIM]]
            padded = [zeros_head] * hh + q_h + [zeros_head] * (HEADS_PER_LANE_TILE - 1 - hh)
            qz_scr[h] = jnp.concatenate(padded, axis=0).astype(BF16)

        blk_row = lax.broadcasted_iota(jnp.int32, (nb, tq), 0)
        past = blk_row < i
        km = km_scr[...]
        km_hi = km.astype(BF16)
        r1 = km - km_hi.astype(F32)
        km_mid = r1.astype(BF16)
        km_lo = (r1 - km_mid.astype(F32)).astype(BF16)
        km3 = jnp.concatenate([km_hi, km_mid, km_lo], axis=0)
        for h in range(N_HEADS):
            hp = h // HEADS_PER_LANE_TILE
            s_scr[h] = scores(0, h)

            g3 = _dot(km3[:, hp * LANES:(hp + 1) * LANES], qz_scr[h])
            gate = g3[0:nb] + g3[nb:2 * nb] + g3[2 * nb:3 * nb]
            gate = jnp.where(past, gate, NEG_INF)
            rank = jnp.zeros((nb, tq), jnp.int32)
            for m in range(nb):
                gm = gate[m:m + 1, :]
                beats = (gm > gate) | ((gm == gate) & (m < blk_row))
                rank = rank + beats.astype(jnp.int32)
            sel = past & (rank < MOBA_TOPK)
            sel_scr[h] = sel.astype(F32)
            m_scr[h] = jnp.full((1, tq), NEG_INF, F32)
            acc_scr[h] = jnp.zeros((V_ROWS, tq), F32)

    def past_blocks(kb0, n_blocks):
        units = [(kb0 + j, h) for j in range(n_blocks) for h in range(N_HEADS)]
        ahead = {}
        for u in range(-SCORE_LEAD, len(units)):
            if 0 <= u + SCORE_LEAD < len(units):
                blk, head = units[u + SCORE_LEAD]
                ahead[u + SCORE_LEAD] = scores(blk + 1, head)
            if u < 0:
                continue
            kb, h = units[u]
            s = s_scr[h]
            keep = sel_scr[h, pl.ds(kb, 1), :] > 0.5
            m_run = m_scr[h]
            m_new = jnp.where(keep, jnp.maximum(m_run, jnp.max(s, axis=0, keepdims=True)), m_run)
            m_sub = jnp.where(keep, m_new, BIG)
            alpha = jnp.exp2(m_run - m_new)
            pb = jnp.exp2((s - m_sub).astype(BF16))
            m_scr[h] = m_new
            acc_scr[h] = alpha * acc_scr[h] + _dot(v_rows(kb, h), pb)
            s_scr[h] = ahead.pop(u)

    def four_blocks(t, carry):
        past_blocks(4 * t, 4)
        return carry

    n_past = jnp.where(live, i, 0)
    lax.fori_loop(0, n_past // 4, four_blocks, 0)
    n_done = 4 * (n_past // 4)

    @pl.when(n_past - n_done >= 2)
    def _():
        past_blocks(n_done, 2)

    @pl.when(lax.rem(n_past, 2) == 1)
    def _():
        past_blocks(n_past - 1, 1)

    def mix_tail(with_mlp):
        mlp = {}

        def mlp_begin():
            mlp['h'] = h1_scr[...]
            mlp['hn'] = _rms_rows(mlp['h'], g1_ref[...]).astype(BF16)
            mlp['acc'] = None

        def mlp_up(c):
            up = jnp.maximum(_dot(mlp['hn'], wup_ref[:, c * FF_CHUNK:(c + 1) * FF_CHUNK]), 0.0)
            mlp['up'] = (up * up).astype(BF16)

        def mlp_down(c):
            d = _dot(mlp['up'], wdn_ref[c * FF_CHUNK:(c + 1) * FF_CHUNK, :])
            mlp['acc'] = d if mlp['acc'] is None else mlp['acc'] + d

        def mlp_gate():
            mlp['h2'] = mlp['h'] + _rms_rows(mlp['acc'], g2_ref[...])
            mlp['z'] = _dot(mlp['h2'].astype(BF16), wgate_ref[...]) + bgate_ref[...]

        def mlp_ple():
            mlp['ple'] = _dot(p_ref[0].astype(BF16), wple_ref[...])

        def mlp_end():
            gate = 1.0 / (1.0 + jnp.exp(-mlp['z']))
            o_ref[0] = mlp['h2'] + _rms_rows(mlp['ple'] * gate, g3_ref[...])

        n_ff = D_FF // FF_CHUNK
        heads_per_piece = N_HEADS // (2 * n_ff)
        assert heads_per_piece * 2 * n_ff == N_HEADS
        pieces_before_head = [[mlp_begin]] + [[] for _ in range(N_HEADS - 1)]
        for c in range(n_ff):
            pieces_before_head[2 * c * heads_per_piece].append(functools.partial(mlp_up, c))
            pieces_before_head[(2 * c + 1) * heads_per_piece].append(functools.partial(mlp_down, c))

        kpos = lax.broadcasted_iota(jnp.int32, (MOBA_BLOCK, tq), 0)
        qpos = lax.broadcasted_iota(jnp.int32, (MOBA_BLOCK, tq), 1)
        causal = kpos <= qpos
        lane = lax.broadcasted_iota(jnp.int32, (GMLP_CHUNK, LANES), 1)
        n_lane_tiles = D_GMLP // LANES
        n_chunks = tq // GMLP_CHUNK
        assert n_lane_tiles * n_chunks == N_HEADS
        heads_out = []
        cols = [[None] * n_lane_tiles for _ in range(n_chunks)]
        for h in range(N_HEADS):
            for piece in pieces_before_head[h] if with_mlp else ():
                piece()
            c, gp = divmod(h, n_lane_tiles)
            vp = vn_scr[c * GMLP_CHUNK:(c + 1) * GMLP_CHUNK, gp * LANES:(gp + 1) * LANES]
            stacked = jnp.concatenate(
                [jnp.where(lane // GROUP_DIM == part, vp, jnp.zeros_like(vp))
                 for part in range(GROUPS_PER_LANE_TILE)], axis=0)
            cols[c][gp] = _dot(wsm_ref[gp], stacked)

            st = jnp.where(causal, s_scr[h], NEG_INF)
            m_run = m_scr[h]
            m_new = jnp.maximum(m_run, jnp.max(st, axis=0, keepdims=True))
            alpha = jnp.exp2(m_run - m_new)
            p = jnp.exp2((st - m_new).astype(BF16))
            acc = alpha * acc_scr[h] + _dot(v_rows(i, h), p)
            heads_out.append(acc[0:HEAD_DIM] / acc[HEAD_DIM:HEAD_DIM + 1])

        if with_mlp:
            mlp_ple()
        mixed = jnp.concatenate(
            [jnp.concatenate(cols[c], axis=1) + bs_ref[...] for c in range(n_chunks)], axis=0)
        gm_out = ug_scr[...].astype(F32) * mixed
        gm_n = _rms_rows(gm_out, gg_ref[...]).astype(BF16)
        y = _dot(gm_n, wo_ref[D_ATTN:, :])
        if with_mlp:
            mlp_gate()

        attn_t = jnp.concatenate(heads_out, axis=0)
        ms_a = jnp.mean(attn_t * attn_t, axis=0, keepdims=True)
        attn_n = (attn_t * lax.rsqrt(ms_a + NORM_EPS) * ag_ref[...]).astype(BF16)
        y = y + _dot_tn(attn_n, wo_ref[:D_ATTN, :])
        y_scr[...] = y
        if with_mlp:
            mlp_end()

    @pl.when(step == 1)
    def _():
        _load_weight_bf16(wup_hbm, wup_ref, (stage.at[0], stage.at[1]), sem, phase="finish")

    @pl.when(step > 0)
    def _():
        mix_tail(True)

    @pl.when(step == 0)
    def _():
        mix_tail(False)


def _const_spec(shape):
    zeros = (0,) * len(shape)
    return pl.BlockSpec(shape, lambda *_: zeros, pipeline_mode=pl.Buffered(1))


def _rope_tables(seq, block):
    inv_freq = ROPE_THETA ** (-np.arange(ROPE_HALF, dtype=np.float64) / ROPE_HALF)
    ang = np.arange(seq, dtype=np.float64)[:, None] * inv_freq[None, :]
    cos, sin = np.cos(ang), np.sin(ang)
    n_blocks = seq // block
    cos_t, sin_t = (t.T.reshape(ROPE_HALF, n_blocks, block).transpose(1, 0, 2) for t in (cos, sin))
    ones = np.ones((seq, HEAD_DIM - ROPE_DIM))
    zeros = np.zeros((seq, HEAD_DIM - ROPE_DIM))
    zh = np.zeros((seq, ROPE_HALF))
    cos_h = np.concatenate([cos, cos, ones], axis=1)
    sa_h = np.concatenate([-sin, zh, zeros], axis=1)
    sb_h = np.concatenate([zh, sin, zeros], axis=1)
    tile = lambda t: np.concatenate([t] * HEADS_PER_LANE_TILE, axis=1).reshape(n_blocks, block, LANES)
    tables = (cos_t, sin_t, tile(cos_h), tile(sa_h), tile(sb_h))
    return tuple(jnp.asarray(np.ascontiguousarray(t), dtype=F32) for t in tables)


def _layer(h, p, mix_pre_g, w_in, gmlp_ln_g, gmlp_ln_b, w_s, b_s, attn_out_g,
           gmlp_out_g, w_o, mix_post_g, mlp_pre_g, w_up, w_down, mlp_post_g,
           w_ple, w_ple_gate, b_ple_gate, ple_post_g):
    B, S, D = h.shape
    assert D == D_MODEL and S % MOBA_BLOCK == 0
    nb = S // MOBA_BLOCK
    T = MOBA_BLOCK
    n_tiles = B * nb
    row = lambda v: v.reshape(1, -1).astype(F32)

    qscale = np.ones((1, D_IN_PROJ), np.float32)
    qscale[:, :D_ATTN] = HEAD_DIM ** -0.5 * math.log2(math.e)
    cos_t, sin_t, cos_k, sa_k, sb_k = _rope_tables(S, T)
    bs_exp = jnp.repeat(b_s.T, GROUP_DIM, axis=1).astype(F32)

    def cur(s):
        t = jnp.minimum(s, n_tiles - 1)
        return t // nb, t % nb

    def prev(s):
        t = jnp.maximum(s - 1, 0)
        return t // nb, t % nb

    hbm = pl.BlockSpec(memory_space=pl.ANY)
    out = pl.pallas_call(
        _layer_kernel,
        grid=(n_tiles + 1,),
        in_specs=[
            pl.BlockSpec((1, T, D), lambda s: (*cur(s), 0)),
            pl.BlockSpec((1, T, D), lambda s: (*prev(s), 0)),
            pl.BlockSpec((1, T, D_PLE), lambda s: (*prev(s), 0)),
            _const_spec((1, D)),
            hbm,
            _const_spec((1, D_IN_PROJ)),
            _const_spec((nb, ROPE_HALF, T)),
            _const_spec((nb, ROPE_HALF, T)),
            _const_spec((nb, T, LANES)),
            _const_spec((nb, T, LANES)),
            _const_spec((nb, T, LANES)),
            _const_spec((1, D_GMLP)),
            _const_spec((1, D_GMLP)),
            _const_spec((N_GROUPS, GMLP_CHUNK, GMLP_CHUNK)),
            _const_spec((GMLP_CHUNK, D_GMLP)),
            _const_spec((D_ATTN, 1)),
            _const_spec((1, D_GMLP)),
            hbm,
            _const_spec((1, D)),
            _const_spec((1, D)),
            hbm,
            hbm,
            _const_spec((1, D)),
            hbm,
            _const_spec((1, D)),
            hbm,
            _const_spec((1, D)),
        ],
        out_specs=pl.BlockSpec((1, T, D), lambda s: (*prev(s), 0)),
        out_shape=jax.ShapeDtypeStruct((B, S, D), F32),
        scratch_shapes=([pltpu.VMEM((nb, T, D_ATTN), BF16),
                         pltpu.VMEM((nb, N_HEADS * V_ROWS, T), BF16),
                         pltpu.VMEM((nb, D_ATTN), F32),
                         pltpu.VMEM((T, D_GMLP), BF16),
                         pltpu.VMEM((T, D_GMLP), BF16),
                         pltpu.VMEM((T, D), F32),
                         pltpu.VMEM((T, D), F32),
                         pltpu.VMEM((D, D_IN_PROJ), BF16),
                         pltpu.VMEM((D_ATTN + D_GMLP, D), BF16),
                         pltpu.VMEM((D, D_FF), BF16),
                         pltpu.VMEM((D_FF, D), BF16),
                         pltpu.VMEM((D, D), BF16),
                         pltpu.VMEM((D_PLE, D), BF16),
                         pltpu.VMEM((D_GMLP // LANES, GMLP_CHUNK, GROUPS_PER_LANE_TILE * GMLP_CHUNK), BF16),
                         pltpu.VMEM((2, STAGE_ROWS, D_FF), F32),
                         pltpu.SemaphoreType.DMA((2,))]
                        + [pltpu.VMEM((N_HEADS, nb, T), F32),
                           pltpu.VMEM((N_HEADS, LANES, T), BF16),
                           pltpu.VMEM((N_HEADS, 1, T), F32),
                           pltpu.VMEM((N_HEADS, V_ROWS, T), F32),
                           pltpu.VMEM((N_HEADS, MOBA_BLOCK, T), F32)]),
        compiler_params=pltpu.CompilerParams(dimension_semantics=("arbitrary",),
                                             vmem_limit_bytes=VMEM_LIMIT_BYTES),
        name="layer",
    )(h, h, p, row(mix_pre_g), w_in.astype(F32), jnp.asarray(qscale), cos_t, sin_t, cos_k, sa_k, sb_k,
      row(gmlp_ln_g), row(gmlp_ln_b), w_s.astype(F32), bs_exp,
      attn_out_g.reshape(-1, 1).astype(F32), row(gmlp_out_g), w_o.astype(F32), row(mix_post_g),
      row(mlp_pre_g), w_up.astype(F32), w_down.astype(F32), row(mlp_post_g),
      w_ple_gate.astype(F32), row(b_ple_gate), w_ple.astype(F32), row(ple_post_g))
    return out


def kernel(x, p, mix_pre_g, w_in, gmlp_ln_g, gmlp_ln_b, w_s, b_s, attn_out_g,
           gmlp_out_g, w_o, mix_post_g, mlp_pre_g, w_up, w_down, mlp_post_g,
           w_ple, w_ple_gate, b_ple_gate, ple_post_g):
    h = x
    for i in range(w_in.shape[0]):
        h = _layer(h, p[i], mix_pre_g[i], w_in[i], gmlp_ln_g[i], gmlp_ln_b[i],
                   w_s[i], b_s[i], attn_out_g[i], gmlp_out_g[i], w_o[i],
                   mix_post_g[i], mlp_pre_g[i], w_up[i], w_down[i], mlp_post_g[i],
                   w_ple[i], w_ple_gate[i], b_ple_gate[i], ple_post_g[i])
    return h
```
